```python
import math
import jax
import jax.numpy as jnp
from jax import lax
import numpy as np

D_MODEL = 2048
BATCH = 4
SEQ = 8192
DEPTH = 4

CHUNK = 64
N_MIXERS = 4
DN_ALPHA = (2 * DEPTH) ** 0.25
DN_BETA = (8 * DEPTH) ** -0.25
LN_EPS = 1e-5

D_FF = int(math.ceil(8 * D_MODEL / 3 / 256)) * 256

RW_HEAD = 64
RW_HEADS = D_MODEL // RW_HEAD
RW_DECAY_LORA = max(32, int(round(1.8 * D_MODEL ** 0.5 / 32)) * 32)
RW_ICLR_LORA = RW_DECAY_LORA
RW_GATE_LORA = max(32, int(round(0.6 * D_MODEL ** 0.8 / 32)) * 32)
RW_GN_EPS = 64e-5

SSD_D_INNER = 2 * D_MODEL
SSD_HEAD = 64
SSD_HEADS = SSD_D_INNER // SSD_HEAD
SSD_GROUPS = 8
SSD_HPG = SSD_HEADS // SSD_GROUPS
SSD_STATE = 128
SSD_CONV = 4
SSD_CONV_DIM = SSD_D_INNER + 2 * SSD_GROUPS * SSD_STATE
SSD_IN_DIM = SSD_D_INNER + SSD_CONV_DIM + SSD_HEADS
SSD_NORM_GROUP = SSD_D_INNER // SSD_GROUPS

DIF_HEADS = 8
DIF_HEAD = D_MODEL // DIF_HEADS // 2
DIF_ROPE = DIF_HEAD // 4
ROPE_THETA = 500000.0
Q_BLOCK = 128
DIF_LAYER = 2
DIF_LAMBDA_INIT = 0.8 - 0.6 * math.exp(-0.3 * DIF_LAYER)

LRU_WIDTH = D_MODEL
LRU_BLOCKS = 16
LRU_BLOCK = LRU_WIDTH // LRU_BLOCKS
LRU_CONV = 4
LRU_C = 8.0

kernel_name = 'hybrid_chunk_causal_trunk'


def layer_norm(x, g, b):
    xf = x.astype(jnp.float32)
    mu = jnp.mean(xf, -1, keepdims=True)
    var = jnp.mean(jnp.square(xf - mu), -1, keepdims=True)
    return ((xf - mu) * lax.rsqrt(var + LN_EPS) * g + b).astype(x.dtype)


def rms_norm(x, g, eps):
    xf = x.astype(jnp.float32)
    return xf * lax.rsqrt(jnp.mean(xf * xf, -1, keepdims=True) + eps) * g


def causal_dwconv(x, w, b):
    k = w.shape[0]
    y = lax.conv_general_dilated(x, w[:, None, :].astype(x.dtype), window_strides=(1,),
                                 padding=[(k - 1, 0)], dimension_numbers=('NWC', 'WIO', 'NWC'),
                                 feature_group_count=x.shape[-1])
    return y + b


def rwkv7_mixer(x, mu, w_in, w0, w1, w2, a0, a1, a2, g1, g2, k_k, k_a, r_k, ln_g, ln_b, w_out):
    bsz, seq, d = x.shape
    xx = jnp.pad(x, ((0, 0), (1, 0), (0, 0)))[:, :-1] - x
    xr, xw, xk, xv, xa, xg = (x + xx * mu[j] for j in range(6))
    r = xr @ w_in[0]
    k = xk @ w_in[1]
    v = xv @ w_in[2]
    w = -jax.nn.softplus(-(w0 + jnp.tanh(xw @ w1) @ w2)) - 0.5
    a = jax.nn.sigmoid(a0 + (xa @ a1) @ a2)
    g = jax.nn.sigmoid(xg @ g1) @ g2
    heads = lambda t: t.reshape(bsz, seq, RW_HEADS, RW_HEAD).astype(jnp.float32)
    kk = heads(k * k_k)
    kk = kk / jnp.maximum(jnp.linalg.norm(kk, axis=-1, keepdims=True), 1e-12)
    k = heads(k * (1 + (a - 1) * k_a))
    r = heads(r)
    v = heads(v)
    a = heads(a)
    decay = jnp.exp(-jnp.exp(heads(w)))

    def step(state, inp):
        r_t, w_t, k_t, v_t, a_t, b_t = inp
        sa = jnp.einsum('bhij,bhj->bhi', state, a_t)
        state = (state * w_t[:, :, None, :] + sa[..., None] * b_t[:, :, None, :]
                 + v_t[..., None] * k_t[:, :, None, :])
        return state, jnp.einsum('bhij,bhj->bhi', state, r_t)

    seq_first = lambda t: jnp.swapaxes(t, 0, 1)
    s0 = jnp.zeros((bsz, RW_HEADS, RW_HEAD, RW_HEAD), jnp.float32)
    _, y = lax.scan(step, s0, tuple(seq_first(t) for t in (r, decay, k, v, -kk, kk * a)))
    y = seq_first(y)
    mu_y = jnp.mean(y, -1, keepdims=True)
    var_y = jnp.mean(jnp.square(y - mu_y), -1, keepdims=True)
    y = ((y - mu_y) * lax.rsqrt(var_y + RW_GN_EPS)).reshape(bsz, seq, d) * ln_g + ln_b
    bonus = jnp.sum(r * k * r_k, -1, keepdims=True) * v
    y = y + bonus.reshape(bsz, seq, d)
    return (y * g).astype(x.dtype) @ w_out


def ssd_mixer(x, w_in, conv_w, conv_b, dt_bias, a_log, d_skip, norm_g, w_out):
    bsz, seq, _ = x.shape
    zxbcdt = x @ w_in
    z = zxbcdt[..., :SSD_D_INNER]
    xbc = zxbcdt[..., SSD_D_INNER:SSD_D_INNER + SSD_CONV_DIM]
    dt = zxbcdt[..., SSD_D_INNER + SSD_CONV_DIM:]
    xbc = jax.nn.silu(causal_dwconv(xbc, conv_w, conv_b))
    gn = SSD_GROUPS * SSD_STATE
    xs = xbc[..., :SSD_D_INNER].reshape(bsz, seq, SSD_GROUPS, SSD_HPG, SSD_HEAD)
    bm = xbc[..., SSD_D_INNER:SSD_D_INNER + gn].reshape(bsz, seq, SSD_GROUPS, SSD_STATE).astype(jnp.float32)
    cm = xbc[..., SSD_D_INNER + gn:].reshape(bsz, seq, SSD_GROUPS, SSD_STATE).astype(jnp.float32)
    dt = jax.nn.softplus(dt.astype(jnp.float32) + dt_bias).reshape(bsz, seq, SSD_GROUPS, SSD_HPG)
    a_neg = -jnp.exp(a_log.astype(jnp.float32)).reshape(SSD_GROUPS, SSD_HPG)
    da = dt * a_neg
    xdt = xs.astype(jnp.float32) * dt[..., None]
    nc = seq // CHUNK
    to_chunks = lambda t: jnp.swapaxes(t.reshape((bsz, nc, CHUNK) + t.shape[2:]), 0, 1)
    tril = jnp.tril(jnp.ones((CHUNK, CHUNK), bool))[None, :, :, None, None]

    def step(state, inp):
        xdt_c, da_c, b_c, c_c = inp
        cs = jnp.cumsum(da_c, axis=1)
        seg = cs[:, :, None] - cs[:, None, :]
        decay = jnp.exp(jnp.where(tril, seg, -jnp.inf))
        cb = jnp.einsum('btgn,bsgn->btsg', c_c, b_c)
        y = jnp.einsum('btsg,btsgh,bsghp->btghp', cb, decay, xdt_c)
        y = y + jnp.einsum('btgn,bghpn->btghp', c_c, state) * jnp.exp(cs)[..., None]
        to_end = jnp.exp(cs[:, -1:] - cs)
        state = (state * jnp.exp(cs[:, -1])[..., None, None]
                 + jnp.einsum('bsgn,bsgh,bsghp->bghpn', b_c, to_end, xdt_c))
        return state, y

    s0 = jnp.zeros((bsz, SSD_GROUPS, SSD_HPG, SSD_HEAD, SSD_STATE), jnp.float32)
    _, y = lax.scan(step, s0, (to_chunks(xdt), to_chunks(da), to_chunks(bm), to_chunks(cm)))
    y = jnp.swapaxes(y, 0, 1).reshape(bsz, seq, SSD_GROUPS, SSD_HPG, SSD_HEAD)
    y = y + xs.astype(jnp.float32) * d_skip.reshape(SSD_GROUPS, SSD_HPG, 1)
    y = y.reshape(bsz, seq, SSD_D_INNER) * jax.nn.silu(z.astype(jnp.float32))
    y = rms_norm(y.reshape(bsz, seq, SSD_GROUPS, SSD_NORM_GROUP), norm_g.reshape(SSD_GROUPS, SSD_NORM_GROUP), 1e-5)
    return y.reshape(bsz, seq, SSD_D_INNER).astype(x.dtype) @ w_out


def partial_rotary(t, cos, sin):
    half = DIF_ROPE // 2
    t1 = t[..., :half]
    t2 = t[..., half:DIF_ROPE]
    return jnp.concatenate([t1 * cos - t2 * sin, t2 * cos + t1 * sin, t[..., DIF_ROPE:]], axis=-1)


def diff_attn_mixer(x, positions, w_in, lq1, lk1, lq2, lk2, subln_g, w_out):
    bsz, seq, d = x.shape
    qkv = x @ w_in
    q = qkv[..., :d].reshape(bsz, seq, 2 * DIF_HEADS, DIF_HEAD)
    k = qkv[..., d:2 * d].reshape(bsz, seq, 2 * DIF_HEADS, DIF_HEAD)
    v = qkv[..., 2 * d:].reshape(bsz, seq, DIF_HEADS, 2 * DIF_HEAD)
    inv_freq = ROPE_THETA ** (-jnp.arange(0, DIF_ROPE, 2, dtype=jnp.float32) / DIF_ROPE)
    ang = positions.astype(jnp.float32)[..., None] * inv_freq
    cos = jnp.cos(ang)[:, :, None, :].astype(x.dtype)
    sin = jnp.sin(ang)[:, :, None, :].astype(x.dtype)
    q = partial_rotary(q, cos, sin) * (DIF_HEAD ** -0.5)
    k = partial_rotary(k, cos, sin)
    lam = (jnp.exp(jnp.sum(lq1.astype(jnp.float32) * lk1)) - jnp.exp(jnp.sum(lq2.astype(jnp.float32) * lk2))
           + DIF_LAMBDA_INIT)
    nb = seq // Q_BLOCK
    qb = q.reshape(bsz, nb, Q_BLOCK, 2 * DIF_HEADS, DIF_HEAD).transpose(1, 0, 3, 2, 4)
    kt = k.transpose(0, 2, 1, 3)
    vt = v.transpose(0, 2, 1, 3)
    key_chunk = jnp.arange(seq) // CHUNK

    def block(args):
        q_blk, blk = args
        s = jnp.einsum('bhqd,bhkd->bhqk', q_blk, kt).astype(jnp.float32)
        q_chunk = (blk * Q_BLOCK + jnp.arange(Q_BLOCK)) // CHUNK
        s = jnp.where(key_chunk[None, :] <= q_chunk[:, None], s, -jnp.inf)
        p = jax.nn.softmax(s, axis=-1).reshape(bsz, DIF_HEADS, 2, Q_BLOCK, seq)
        attn = p[:, :, 0] - lam * p[:, :, 1]
        return jnp.einsum('bhqk,bhkd->bhqd', attn.astype(vt.dtype), vt)

    o = lax.map(block, (qb, jnp.arange(nb)))
    o = o.transpose(1, 0, 3, 2, 4).reshape(bsz, seq, DIF_HEADS, 2 * DIF_HEAD)
    o = rms_norm(o, subln_g, 1e-5) * (1.0 - DIF_LAMBDA_INIT)
    return o.reshape(bsz, seq, d).astype(x.dtype) @ w_out


def rglru_mixer(x, w_in, conv_w, conv_b, w_a, b_a, w_x, b_x, lam, w_out):
    bsz, seq, _ = x.shape
    gu = x @ w_in
    gate = jax.nn.gelu(gu[..., :LRU_WIDTH], approximate=True)
    u = causal_dwconv(gu[..., LRU_WIDTH:], conv_w, conv_b)
    ub = u.reshape(bsz, seq, LRU_BLOCKS, LRU_BLOCK)
    r = jax.nn.sigmoid(jnp.einsum('bshi,hij->bshj', ub, w_a).reshape(bsz, seq, LRU_WIDTH) + b_a)
    i = jax.nn.sigmoid(jnp.einsum('bshi,hij->bshj', ub, w_x).reshape(bsz, seq, LRU_WIDTH) + b_x)
    log_a = -LRU_C * r.astype(jnp.float32) * jax.nn.softplus(-lam.astype(jnp.float32))
    a = jnp.exp(log_a)
    b = jnp.sqrt(-jnp.expm1(2.0 * log_a)) * (i * u).astype(jnp.float32)

    def combine(left, right):
        a1, b1 = left
        a2, b2 = right
        return a1 * a2, a2 * b1 + b2

    _, h = lax.associative_scan(combine, (a, b), axis=1)
    return (gate * h.astype(x.dtype)) @ w_out


def swiglu_ffn(x, w_in, w_out):
    gu = x @ w_in
    return (jax.nn.silu(gu[..., :D_FF]) * gu[..., D_FF:]) @ w_out


def setup_inputs(seed: int = 0) -> dict:
    key = jax.random.key(seed)
    kit = iter(jax.random.split(key, 64))

    def nrm(shape, scale):
        return jax.random.normal(next(kit), shape, jnp.float32) * scale

    def uni(shape, lo, hi):
        return jax.random.uniform(next(kit), shape, jnp.float32, lo, hi)

    d = D_MODEL
    x = nrm((BATCH, SEQ, d), 1.0)
    offsets = jax.random.randint(next(kit), (BATCH, 1), 0, 1024) * CHUNK
    positions = (offsets + jnp.arange(SEQ)[None, :]).astype(jnp.int32)
    rwkv_mu = uni((6, d), 0.0, 1.0)
    rwkv_w_in = nrm((3, d, d), d ** -0.5)
    rwkv_w0 = uni((d,), -6.0, -1.0)
    rwkv_w1 = nrm((d, RW_DECAY_LORA), d ** -0.5)
    rwkv_w2 = nrm((RW_DECAY_LORA, d), 0.1 * RW_DECAY_LORA ** -0.5)
    rwkv_a0 = nrm((d,), 0.1)
    rwkv_a1 = nrm((d, RW_ICLR_LORA), d ** -0.5)
    rwkv_a2 = nrm((RW_ICLR_LORA, d), 0.1 * RW_ICLR_LORA ** -0.5)
    rwkv_g1 = nrm((d, RW_GATE_LORA), d ** -0.5)
    rwkv_g2 = nrm((RW_GATE_LORA, d), RW_GATE_LORA ** -0.5)
    rwkv_k_k = 0.85 + nrm((d,), 0.02)
    rwkv_k_a = 1.0 + nrm((d,), 0.02)
    rwkv_r_k = nrm((RW_HEADS, RW_HEAD), 0.1)
    rwkv_ln_g = 1.0 + nrm((d,), 0.02)
    rwkv_ln_b = nrm((d,), 0.02)
    rwkv_w_out = nrm((d, d), d ** -0.5 * DN_BETA)
    ssd_w_in = nrm((d, SSD_IN_DIM), d ** -0.5)
    ssd_conv_w = nrm((SSD_CONV, SSD_CONV_DIM), SSD_CONV ** -0.5)
    ssd_conv_b = nrm((SSD_CONV_DIM,), 0.02)
    dt0 = jnp.exp(uni((SSD_HEADS,), math.log(1e-3), math.log(1e-1)))
    ssd_dt_bias = dt0 + jnp.log(-jnp.expm1(-dt0))
    ssd_a_log = jnp.log(uni((SSD_HEADS,), 1.0, 16.0))
    ssd_d = 1.0 + nrm((SSD_HEADS,), 0.02)
    ssd_norm_g = 1.0 + nrm((SSD_D_INNER,), 0.02)
    ssd_w_out = nrm((SSD_D_INNER, d), SSD_D_INNER ** -0.5 * DN_BETA)
    dif_w_in = nrm((d, 3 * d), d ** -0.5)
    dif_lq1 = nrm((DIF_HEAD,), 0.1)
    dif_lk1 = nrm((DIF_HEAD,), 0.1)
    dif_lq2 = nrm((DIF_HEAD,), 0.1)
    dif_lk2 = nrm((DIF_HEAD,), 0.1)
    dif_subln_g = 1.0 + nrm((2 * DIF_HEAD,), 0.02)
    dif_w_out = nrm((d, d), d ** -0.5 * DN_BETA)
    lru_w_in = nrm((d, 2 * LRU_WIDTH), d ** -0.5)
    lru_conv_w = nrm((LRU_CONV, LRU_WIDTH), LRU_CONV ** -0.5)
    lru_conv_b = nrm((LRU_WIDTH,), 0.02)
    lru_w_a = nrm((LRU_BLOCKS, LRU_BLOCK, LRU_BLOCK), LRU_BLOCK ** -0.5)
    lru_b_a = nrm((LRU_WIDTH,), 0.02)
    lru_w_x = nrm((LRU_BLOCKS, LRU_BLOCK, LRU_BLOCK), LRU_BLOCK ** -0.5)
    lru_b_x = nrm((LRU_WIDTH,), 0.02)
    a_init = uni((LRU_WIDTH,), 0.9, 0.999)
    lru_lam = jnp.log(a_init) - jnp.log1p(-a_init)
    lru_w_out = nrm((LRU_WIDTH, d), LRU_WIDTH ** -0.5 * DN_BETA)
    ffn_w_in = nrm((DEPTH, d, 2 * D_FF), d ** -0.5)
    ffn_w_out = nrm((DEPTH, D_FF, d), D_FF ** -0.5 * DN_BETA)
    ln_g = 1.0 + nrm((DEPTH, 2, d), 0.02)
    ln_b = nrm((DEPTH, 2, d), 0.02)
    return {'x': x, 'positions': positions,
            'rwkv_mu': rwkv_mu, 'rwkv_w_in': rwkv_w_in, 'rwkv_w0': rwkv_w0, 'rwkv_w1': rwkv_w1,
            'rwkv_w2': rwkv_w2, 'rwkv_a0': rwkv_a0, 'rwkv_a1': rwkv_a1, 'rwkv_a2': rwkv_a2,
            'rwkv_g1': rwkv_g1, 'rwkv_g2': rwkv_g2, 'rwkv_k_k': rwkv_k_k, 'rwkv_k_a': rwkv_k_a,
            'rwkv_r_k': rwkv_r_k, 'rwkv_ln_g': rwkv_ln_g, 'rwkv_ln_b': rwkv_ln_b, 'rwkv_w_out': rwkv_w_out,
            'ssd_w_in': ssd_w_in, 'ssd_conv_w': ssd_conv_w, 'ssd_conv_b': ssd_conv_b,
            'ssd_dt_bias': ssd_dt_bias, 'ssd_a_log': ssd_a_log, 'ssd_d': ssd_d,
            'ssd_norm_g': ssd_norm_g, 'ssd_w_out': ssd_w_out,
            'dif_w_in': dif_w_in, 'dif_lq1': dif_lq1, 'dif_lk1': dif_lk1, 'dif_lq2': dif_lq2,
            'dif_lk2': dif_lk2, 'dif_subln_g': dif_subln_g, 'dif_w_out': dif_w_out,
            'lru_w_in': lru_w_in, 'lru_conv_w': lru_conv_w, 'lru_conv_b': lru_conv_b,
            'lru_w_a': lru_w_a, 'lru_b_a': lru_b_a, 'lru_w_x': lru_w_x, 'lru_b_x': lru_b_x,
            'lru_lam': lru_lam, 'lru_w_out': lru_w_out,
            'ffn_w_in': ffn_w_in, 'ffn_w_out': ffn_w_out, 'ln_g': ln_g, 'ln_b': ln_b}


def reference(x, positions,
              rwkv_mu, rwkv_w_in, rwkv_w0, rwkv_w1, rwkv_w2, rwkv_a0, rwkv_a1, rwkv_a2,
              rwkv_g1, rwkv_g2, rwkv_k_k, rwkv_k_a, rwkv_r_k, rwkv_ln_g, rwkv_ln_b, rwkv_w_out,
              ssd_w_in, ssd_conv_w, ssd_conv_b, ssd_dt_bias, ssd_a_log, ssd_d, ssd_norm_g, ssd_w_out,
              dif_w_in, dif_lq1, dif_lk1, dif_lq2, dif_lk2, dif_subln_g, dif_w_out,
              lru_w_in, lru_conv_w, lru_conv_b, lru_w_a, lru_b_a, lru_w_x, lru_b_x, lru_lam, lru_w_out,
              ffn_w_in, ffn_w_out, ln_g, ln_b):
    h = x
    for i in range(DEPTH):
        m = i % N_MIXERS
        if m == 0:
            y = rwkv7_mixer(h, rwkv_mu, rwkv_w_in, rwkv_w0, rwkv_w1, rwkv_w2, rwkv_a0, rwkv_a1, rwkv_a2,
                            rwkv_g1, rwkv_g2, rwkv_k_k, rwkv_k_a, rwkv_r_k, rwkv_ln_g, rwkv_ln_b, rwkv_w_out)
        elif m == 1:
            y = ssd_mixer(h, ssd_w_in, ssd_conv_w, ssd_conv_b, ssd_dt_bias, ssd_a_log, ssd_d,
                          ssd_norm_g, ssd_w_out)
        elif m == 2:
            y = diff_attn_mixer(h, positions, dif_w_in, dif_lq1, dif_lk1, dif_lq2, dif_lk2,
                                dif_subln_g, dif_w_out)
        else:
            y = rglru_mixer(h, lru_w_in, lru_conv_w, lru_conv_b, lru_w_a, lru_b_a, lru_w_x, lru_b_x,
                            lru_lam, lru_w_out)
        h = layer_norm(DN_ALPHA * h + y, ln_g[i, 0], ln_b[i, 0])
        h = layer_norm(DN_ALPHA * h + swiglu_ffn(h, ffn_w_in[i], ffn_w_out[i]), ln_g[i, 1], ln_b[i, 1])
    return h
```

```python
import functools
import math

import numpy as np
import jax
import jax.numpy as jnp
from jax import lax
from jax.experimental import pallas as pl
from jax.experimental.pallas import tpu as pltpu

F32 = jnp.float32
BF16 = jnp.bfloat16
HI = lax.Precision.HIGHEST

LANES = 128
SUBLANES = 8
VMEM_LIMIT_BYTES = 56 * 1024 * 1024

LN_EPS = 1e-5
CHUNK = 64
RW_HEAD = 64
RW_GN_EPS = 64e-5
SSD_HEAD = 64
SSD_GROUPS = 8
SSD_STATE = 128
DIF_HEADS = 8
ROPE_THETA = 500000.0
DIF_LAYER = 2
DIF_LAMBDA_INIT = 0.8 - 0.6 * math.exp(-0.3 * DIF_LAYER)
LRU_BLOCKS = 16
LRU_C = 8.0
NEG_BIG = -1e30


def _cparams(sem):
    return pltpu.CompilerParams(dimension_semantics=sem, vmem_limit_bytes=VMEM_LIMIT_BYTES)


def _dot(a, b):
    return jnp.dot(a, b, preferred_element_type=F32)


def _dot_nt(a, b):
    return lax.dot_general(a, b, (((1,), (1,)), ((), ())), preferred_element_type=F32)


def _dot_tn(a, b):
    return lax.dot_general(a, b, (((0,), (0,)), ((), ())), preferred_element_type=F32)


def _dot_hi(a, b):
    return jnp.dot(a, b, preferred_element_type=F32, precision=HI)


def _layer_norm(t, g, b):
    mu = jnp.mean(t, -1, keepdims=True)
    d = t - mu
    var = jnp.mean(d * d, -1, keepdims=True)
    return d * lax.rsqrt(var + LN_EPS) * g + b


def _neg_softplus_neg(z):
    return jnp.minimum(z, 0.0) - jnp.log(1.0 + jnp.exp(-jnp.abs(z)))


def _softplus(z):
    return jnp.maximum(z, 0.0) + jnp.log(1.0 + jnp.exp(-jnp.abs(z)))


def _silu(z):
    return z * jax.nn.sigmoid(z)


def _shifted(x, halo, d):
    body = pltpu.roll(x, d, axis=0)
    head = pltpu.roll(jnp.concatenate([halo, x[:SUBLANES]], axis=0), d, axis=0)[SUBLANES:]
    return jnp.concatenate([head, body[SUBLANES:]], axis=0)


def _causal_conv4(x, halo, cw, cb):
    out = x * cw[3:4, :] + cb
    for d in (1, 2, 3):
        out = out + _shifted(x, halo, d) * cw[3 - d:4 - d, :]
    return out


def _pair_cols(lo_mask, mat, h0):
    return jnp.where(lo_mask, mat[:, h0:h0 + 1], mat[:, h0 + 1:h0 + 2])


def _mm_body(*refs, n_a, n_extra, n_out, pre, post, nk):
    a = refs[:n_a]
    w = refs[n_a]
    extra = refs[n_a + 1:n_a + 1 + n_extra]
    outs = refs[n_a + 1 + n_extra:n_a + 1 + n_extra + n_out]

    lhs = pre(*[r[...] for r in a])
    part = _dot(lhs, w[...])

    def finish(total):
        res = post(total, *[r[...] for r in extra])
        for o, v in zip(outs, res):
            o[...] = v.astype(o.dtype)

    if nk == 1:
        finish(part)
    else:
        acc = refs[-1]
        k = pl.program_id(2)

        @pl.when(k == 0)
        def _():
            acc[...] = part

        @pl.when(k > 0)
        def _():
            acc[...] += part

        @pl.when(k == nk - 1)
        def _():
            finish(acc[...])


def fused_matmul(a_list, w, *, rows=(), bands=(), cols=(), pre, post, out_dtypes, tm, tn, tk):
    m, kdim = a_list[0].shape
    n = w.shape[1]
    tm, tn, tk = min(tm, m), min(tn, n), min(tk, kdim)
    assert m % tm == 0 and n % tn == 0 and kdim % tk == 0
    nk = kdim // tk
    in_specs = [pl.BlockSpec((tm, tk), lambda i, j, k: (i, k)) for _ in a_list]
    in_specs.append(pl.BlockSpec((tk, tn), lambda i, j, k: (k, j)))
    in_specs += [pl.BlockSpec((tm, tn), lambda i, j, k: (i, j)) for _ in rows]
    in_specs += [pl.BlockSpec((tm, b.shape[1]), lambda i, j, k: (i, 0)) for b in bands]
    in_specs += [pl.BlockSpec((1, tn), lambda i, j, k: (0, j)) for _ in cols]
    out_specs = [pl.BlockSpec((tm, tn), lambda i, j, k: (i, j)) for _ in out_dtypes]
    out_shape = [jax.ShapeDtypeStruct((m, n), dt) for dt in out_dtypes]
    body = functools.partial(_mm_body, n_a=len(a_list), n_extra=len(rows) + len(bands) + len(cols),
                             n_out=len(out_dtypes), pre=pre, post=post, nk=nk)
    return pl.pallas_call(
        body,
        grid=(m // tm, n // tn, nk),
        in_specs=in_specs,
        out_specs=out_specs,
        out_shape=out_shape,
        scratch_shapes=[pltpu.VMEM((tm, tn), F32)] if nk > 1 else [],
        compiler_params=_cparams(("parallel", "parallel", "arbitrary")),
    )(*a_list, w, *rows, *bands, *cols)


def _pre_cast(x):
    return x.astype(BF16)


def _post_id(acc):
    return (acc,)


def _post_ln(alpha, acc, h, g, b):
    return (_layer_norm(alpha * h + acc, g, b),)


def proj(h, w, out_dtype, *, tn=1024):
    (out,) = fused_matmul([h], w, pre=_pre_cast, post=_post_id, out_dtypes=[out_dtype], tm=512, tn=tn, tk=2048)
    return out


def out_proj_ln(z, w, h, g, b, alpha):
    n = w.shape[1]
    (out,) = fused_matmul([z], w, rows=[h], cols=[g, b], pre=_pre_cast,
                          post=functools.partial(_post_ln, alpha), out_dtypes=[F32],
                          tm=512, tn=n, tk=1024)
    return out


def _ffn_body(x_ref, wg_ref, wu_ref, wo_ref, g_ref, b_ref, out_ref, acc_ref, xb_ref, *, nf, alpha):
    f = pl.program_id(1)

    @pl.when(f == 0)
    def _():
        xb_ref[...] = x_ref[...].astype(BF16)

    xb = xb_ref[...]
    gate = _dot(xb, wg_ref[...])
    up = _dot(xb, wu_ref[...])
    act = (_silu(gate) * up).astype(BF16)
    part = _dot(act, wo_ref[...])

    @pl.when(f == 0)
    def _():
        acc_ref[...] = part

    @pl.when(f > 0)
    def _():
        acc_ref[...] += part

    @pl.when(f == nf - 1)
    def _():
        out_ref[...] = _layer_norm(alpha * x_ref[...] + acc_ref[...], g_ref[...], b_ref[...])


def ffn_ln(x, w_in, w_out, g, b, alpha, *, tm=512, tf=512):
    m, d = x.shape
    dff = w_out.shape[0]
    tm = min(tm, m)
    nf = dff // tf
    assert dff % tf == 0 and m % tm == 0
    return pl.pallas_call(
        functools.partial(_ffn_body, nf=nf, alpha=alpha),
        grid=(m // tm, nf),
        in_specs=[
            pl.BlockSpec((tm, d), lambda i, f: (i, 0)),
            pl.BlockSpec((d, tf), lambda i, f: (0, f)),
            pl.BlockSpec((d, tf), lambda i, f: (0, nf + f)),
            pl.BlockSpec((tf, d), lambda i, f: (f, 0)),
            pl.BlockSpec((1, d), lambda i, f: (0, 0)),
            pl.BlockSpec((1, d), lambda i, f: (0, 0)),
        ],
        out_specs=pl.BlockSpec((tm, d), lambda i, f: (i, 0)),
        out_shape=jax.ShapeDtypeStruct((m, d), F32),
        scratch_shapes=[pltpu.VMEM((tm, d), F32), pltpu.VMEM((tm, d), BF16)],
        compiler_params=_cparams(("parallel", "arbitrary")),
    )(x, w_in, w_in, w_out, g, b)


def _rwkv_proj_body(x_ref, halo_ref, mu_ref, w_ref, out_ref, lhs_ref, *, tm, seq):
    i = pl.program_id(0)
    n = pl.program_id(2)

    @pl.when(n == 0)
    def _():
        x = x_ref[...]
        halo = jnp.where((i * tm) % seq == 0, 0.0, halo_ref[...])
        xx = _shifted(x, halo, 1) - x
        lhs_ref[...] = (x + xx * mu_ref[...]).astype(BF16)

    out_ref[...] = _dot(lhs_ref[...], w_ref[...])


def rwkv_proj(x, mu3, w3, seq, *, tm=512, tn=512):
    m, d = x.shape
    n = w3.shape[2]
    tm = min(tm, seq)
    per = tm // SUBLANES
    return pl.pallas_call(
        functools.partial(_rwkv_proj_body, tm=tm, seq=seq),
        grid=(m // tm, 3, n // tn),
        in_specs=[
            pl.BlockSpec((tm, d), lambda i, j, c: (i, 0)),
            pl.BlockSpec((SUBLANES, d), lambda i, j, c: (jnp.maximum(i * per - 1, 0), 0)),
            pl.BlockSpec((None, 1, d), lambda i, j, c: (j, 0, 0)),
            pl.BlockSpec((None, d, tn), lambda i, j, c: (j, 0, c)),
        ],
        out_specs=pl.BlockSpec((None, tm, tn), lambda i, j, c: (j, i, c)),
        out_shape=jax.ShapeDtypeStruct((3, m, n), F32),
        scratch_shapes=[pltpu.VMEM((tm, d), BF16)],
        compiler_params=_cparams(("parallel", "arbitrary", "arbitrary")),
    )(x, x, mu3, w3)


def _rwkv_lora_body(x_ref, halo_ref, mu_ref, w1_ref, w2_ref, a1_ref, a2_ref, g1_ref, g2_ref, w0_ref, a0_ref,
                    wl_ref, a_ref, g_ref, *, tm, seq):
    i = pl.program_id(0)
    x = x_ref[...]
    halo = jnp.where((i * tm) % seq == 0, 0.0, halo_ref[...])
    xx = _shifted(x, halo, 1) - x
    mu = mu_ref[...]
    xw = (x + xx * mu[0:1, :]).astype(BF16)
    xa = (x + xx * mu[1:2, :]).astype(BF16)
    xg = (x + xx * mu[2:3, :]).astype(BF16)
    zw = w0_ref[...] + _dot(jnp.tanh(_dot(xw, w1_ref[...])).astype(BF16), w2_ref[...])
    wl_ref[...] = _neg_softplus_neg(zw) - 0.5
    a_ref[...] = jax.nn.sigmoid(a0_ref[...] + _dot(_dot(xa, a1_ref[...]).astype(BF16), a2_ref[...]))
    g_ref[...] = _dot(jax.nn.sigmoid(_dot(xg, g1_ref[...])).astype(BF16), g2_ref[...])


def rwkv_lora(x, mu3, w1, w2, a1, a2, g1, g2, w0, a0, seq, *, tm=256):
    m, d = x.shape
    tm = min(tm, seq)
    per = tm // SUBLANES
    full = lambda arr: pl.BlockSpec(arr.shape, lambda i: (0,) * arr.ndim)
    row = pl.BlockSpec((tm, d), lambda i: (i, 0))
    return pl.pallas_call(
        functools.partial(_rwkv_lora_body, tm=tm, seq=seq),
        grid=(m // tm,),
        in_specs=[row, pl.BlockSpec((SUBLANES, d), lambda i: (jnp.maximum(i * per - 1, 0), 0)),
                  full(mu3), full(w1), full(w2), full(a1), full(a2), full(g1), full(g2), full(w0), full(a0)],
        out_specs=[row, row, row],
        out_shape=[jax.ShapeDtypeStruct((m, d), F32)] * 3,
        compiler_params=_cparams(("parallel",)),
    )(x, x, mu3, w1, w2, a1, a2, g1, g2, w0, a0)


RW_PAIRS_PER_STEP = 4
RW_LEVELS = 6


def _rwkv_masks():
    n = 2 * CHUNK
    t = np.arange(n)[:, None]
    s = np.arange(n)[None, :]
    strict = (t > s).astype(np.float32)
    incl = (t >= s).astype(np.float32)
    bones = ((t // RW_HEAD) == (s // RW_HEAD)).astype(np.float32)
    tri = np.stack([strict, incl, bones])
    lvl = []
    for lv in range(RW_LEVELS):
        sz = 1 << lv
        lvl.append(((t // (2 * sz) == s // (2 * sz)) & ((t // sz) % 2 == 1) & ((s // sz) % 2 == 0)).astype(np.float32))
    t64 = incl[:CHUNK, :CHUNK]
    return jnp.asarray(tri), jnp.asarray(np.stack(lvl)), jnp.asarray(t64)


def _rwkv_core_body(r_ref, k_ref, v_ref, wl_ref, a_ref, g_ref, kk_ref, ka_ref, rk_ref, lng_ref, lnb_ref,
                    tri_ref, lvl_ref, t64_ref, out_ref, st_ref):
    c = pl.program_id(2)

    @pl.when(c == 0)
    def _():
        st_ref[...] = jnp.zeros_like(st_ref)

    strict, incl, bones = tri_ref[0], tri_ref[1], tri_ref[2]
    eye = incl - strict
    t64 = t64_ref[...]
    lo = lax.broadcasted_iota(jnp.int32, (1, LANES), 1) < RW_HEAD
    inv_head = 1.0 / RW_HEAD

    def stack(x):
        return jnp.concatenate([jnp.where(lo, x, 0.0), jnp.where(lo, 0.0, x)], axis=0).astype(BF16)

    for q in range(RW_PAIRS_PER_STEP):
        sl = slice(q * LANES, (q + 1) * LANES)
        r, k, v, wl, a = r_ref[:, sl], k_ref[:, sl], v_ref[:, sl], wl_ref[:, sl], a_ref[:, sl]
        kk = k * kk_ref[:, sl]
        kk = kk / jnp.maximum(jnp.sqrt(_dot_hi(kk * kk, bones)), 1e-12)
        k2 = k * (1.0 + (a - 1.0) * ka_ref[:, sl])
        lw = -jnp.exp(wl)
        cum = _dot_hi(t64, lw)
        cl = cum[CHUNK - 1:CHUNK, :]
        e_neg = jnp.exp(-cum)
        e_end = jnp.exp(cl - cum)
        kb = kk * a
        a_s = stack(-kk * jnp.exp(cum - lw))
        r_s = stack(r * jnp.exp(cum))
        b_s = stack(kb * e_neg)
        k_s = stack(k2 * e_neg)
        v_s = stack(v)
        bh_s = stack(kb * e_end)
        kh_s = stack(k2 * e_end)

        a_ab = strict * _dot_nt(a_s, b_s)
        a_ak = strict * _dot_nt(a_s, k_s)
        r_b = incl * _dot_nt(r_s, b_s)
        r_k = incl * _dot_nt(r_s, k_s)

        t_inv = eye + a_ab * lvl_ref[0]
        for lv in range(1, RW_LEVELS):
            tb = t_inv.astype(BF16)
            off = (a_ab * lvl_ref[lv]).astype(BF16)
            t_inv = t_inv + _dot(_dot(tb, off).astype(BF16), tb)

        s_old = st_ref[q]
        s_b = s_old.astype(BF16)
        rhs = _dot_nt(a_s, s_b) + _dot(a_ak.astype(BF16), v_s)
        u_b = _dot(t_inv.astype(BF16), rhs.astype(BF16)).astype(BF16)
        y_s = _dot_nt(r_s, s_b) + _dot(r_b.astype(BF16), u_b) + _dot(r_k.astype(BF16), v_s)
        st_ref[q] = s_old * jnp.exp(cl) + _dot_tn(u_b, bh_s) + _dot_tn(v_s, kh_s)

        y = y_s[:CHUNK] + y_s[CHUNK:]
        mu = _dot_hi(y, bones) * inv_head
        d = y - mu
        var = _dot_hi(d * d, bones) * inv_head
        yn = d * lax.rsqrt(var + RW_GN_EPS) * lng_ref[:, sl] + lnb_ref[:, sl]
        bonus = _dot_hi(r * k2 * rk_ref[:, sl], bones) * v
        out_ref[:, sl] = ((yn + bonus) * g_ref[:, sl]).astype(out_ref.dtype)


def rwkv_core(rkv, wl, a, g, k_k, k_a, r_k, ln_g, ln_b, bsz, seq):
    _, m, d = rkv.shape
    width = RW_PAIRS_PER_STEP * LANES
    nc = seq // CHUNK
    tri, lvl, t64 = _rwkv_masks()
    row = lambda b, p, c: (b * nc + c, p)
    rkv_spec = lambda j: pl.BlockSpec((None, CHUNK, width), lambda b, p, c: (j, b * nc + c, p))
    act = pl.BlockSpec((CHUNK, width), row)
    par = pl.BlockSpec((1, width), lambda b, p, c: (0, p))
    full = lambda arr: pl.BlockSpec(arr.shape, lambda b, p, c: (0,) * arr.ndim)
    return pl.pallas_call(
        _rwkv_core_body,
        grid=(bsz, d // width, nc),
        in_specs=[rkv_spec(0), rkv_spec(1), rkv_spec(2), act, act, act, par, par, par, par, par,
                  full(tri), full(lvl), full(t64)],
        out_specs=act,
        out_shape=jax.ShapeDtypeStruct((m, d), BF16),
        scratch_shapes=[pltpu.VMEM((RW_PAIRS_PER_STEP, LANES, LANES), F32)],
        compiler_params=_cparams(("parallel", "parallel", "arbitrary")),
    )(rkv, rkv, rkv, wl, a, g, k_k, k_a, r_k, ln_g, ln_b, tri, lvl, t64)


def _pad_to(x, axis, size):
    pad = [(0, 0)] * x.ndim
    pad[axis] = (0, size - x.shape[axis])
    return jnp.pad(x, pad)


def rwkv_layer(h, bsz, seq, mu, w_in, w0, w1, w2, a0, a1, a2, g1, g2, k_k, k_a, r_k, gn_g, gn_b, w_out,
               ln_g, ln_b, alpha):
    d = h.shape[1]
    v2 = lambda t: t.reshape(1, d)
    mu_rkv = jnp.stack([mu[0], mu[2], mu[3]]).reshape(3, 1, d)
    mu_lora = jnp.stack([mu[1], mu[4], mu[5]])
    rkv = rwkv_proj(h, mu_rkv, w_in.astype(BF16), seq)
    wl, a, g = rwkv_lora(
        h, mu_lora,
        _pad_to(w1, 1, LANES).astype(BF16), _pad_to(w2, 0, LANES).astype(BF16),
        _pad_to(a1, 1, LANES).astype(BF16), _pad_to(a2, 0, LANES).astype(BF16),
        g1.astype(BF16), g2.astype(BF16), v2(w0), v2(a0), seq)
    z = rwkv_core(rkv, wl, a, g, v2(k_k), v2(k_a), v2(r_k), v2(gn_g), v2(gn_b), bsz, seq)
    return out_proj_ln(z, w_out.astype(BF16), h, ln_g, ln_b, alpha)


def _ssd_body(xbc_ref, halo_ref, z_ref, dtp_ref, cw_ref, cb_ref, dtb_ref, alog_ref, dsk_ref, ng_ref, t64_ref,
              out_ref, st_ref, xc_ref, dt_ref, da_ref, *, tq, d_inner):
    s = pl.program_id(1)

    @pl.when(s == 0)
    def _():
        st_ref[...] = jnp.zeros_like(st_ref)

    halo = jnp.where(s == 0, 0.0, halo_ref[...])
    xc_ref[...] = _silu(_causal_conv4(xbc_ref[...], halo, cw_ref[...], cb_ref[...]))
    dt_all = _softplus(dtp_ref[...] + dtb_ref[...])
    dt_ref[...] = dt_all
    da_ref[...] = dt_all * (-jnp.exp(alog_ref[...]))

    t64 = t64_ref[...]
    lo = lax.broadcasted_iota(jnp.int32, (1, LANES), 1) < SSD_HEAD
    tril = t64 > 0.5
    dsk = dsk_ref[...]
    gn = SSD_GROUPS * SSD_STATE
    pairs_per_group = d_inner // SSD_GROUPS // LANES

    def chunk(ci, carry):
        r0 = pl.multiple_of(ci * CHUNK, CHUNK)
        rows = pl.ds(r0, CHUNK)
        dt_c = dt_ref[rows, :]
        cs = _dot_hi(t64, da_ref[rows, :])
        cs_t = cs.T
        cs_last = cs[CHUNK - 1:CHUNK, :]
        ecs = jnp.exp(cs)
        to_end = jnp.exp(cs_last - cs)
        e_last = jnp.exp(cs_last)
        for g in range(SSD_GROUPS):
            bg = xc_ref[rows, d_inner + g * SSD_STATE:d_inner + (g + 1) * SSD_STATE]
            cg = xc_ref[rows, d_inner + gn + g * SSD_STATE:d_inner + gn + (g + 1) * SSD_STATE].astype(BF16)
            cb = _dot_nt(cg, bg.astype(BF16))
            bg_t = bg.T.astype(BF16)
            ssq = jnp.zeros((CHUNK, 1), F32)
            ys = []
            for q in range(pairs_per_group):
                p = g * pairs_per_group + q
                h0 = 2 * p
                sl = slice(p * LANES, (p + 1) * LANES)
                x = xc_ref[rows, sl]
                xdt = x * _pair_cols(lo, dt_c, h0)
                y = None
                for hh, keep in ((h0, lo), (h0 + 1, jnp.logical_not(lo))):
                    seg = jnp.minimum(cs[:, hh:hh + 1] - cs_t[hh:hh + 1, :], 0.0)
                    mat = jnp.where(tril, cb * jnp.exp(seg), 0.0).astype(BF16)
                    part = _dot(mat, jnp.where(keep, xdt, 0.0).astype(BF16))
                    y = part if y is None else y + part
                st = st_ref[p]
                y = y + _dot(cg, st.astype(BF16)) * _pair_cols(lo, ecs, h0)
                st_ref[p] = st * _pair_cols(lo, e_last, h0) + _dot(bg_t, (xdt * _pair_cols(lo, to_end, h0)).astype(BF16))
                y = y + x * _pair_cols(lo, dsk, h0)
                y = y * _silu(z_ref[rows, sl].astype(F32))
                ssq = ssq + jnp.sum(y * y, axis=-1, keepdims=True)
                ys.append((sl, y))
            inv = lax.rsqrt(ssq * (1.0 / (pairs_per_group * LANES)) + 1e-5)
            for sl, y in ys:
                out_ref[rows, sl] = (y * inv * ng_ref[:, sl]).astype(out_ref.dtype)
        return carry

    lax.fori_loop(0, tq // CHUNK, chunk, 0)


def ssd_core(xbc_pre, z, dt_pre, conv_w, conv_b, dt_bias, a_log, d_skip, norm_g, bsz, seq, *, tq=256):
    n, conv_dim = xbc_pre.shape
    d_inner = z.shape[1]
    tq = min(tq, seq)
    nblk = seq // tq
    per = tq // SUBLANES
    npairs = d_inner // LANES
    t64 = _rwkv_masks()[2]
    row = lambda b, s: (b * nblk + s, 0)
    full = lambda arr: pl.BlockSpec(arr.shape, lambda b, s: (0,) * arr.ndim)
    return pl.pallas_call(
        functools.partial(_ssd_body, tq=tq, d_inner=d_inner),
        grid=(bsz, nblk),
        in_specs=[
            pl.BlockSpec((tq, conv_dim), row),
            pl.BlockSpec((SUBLANES, conv_dim), lambda b, s: (jnp.maximum((b * nblk + s) * per - 1, 0), 0)),
            pl.BlockSpec((tq, d_inner), row),
            pl.BlockSpec((tq, LANES), row),
            full(conv_w), full(conv_b), full(dt_bias), full(a_log), full(d_skip), full(norm_g), full(t64),
        ],
        out_specs=pl.BlockSpec((tq, d_inner), row),
        out_shape=jax.ShapeDtypeStruct((n, d_inner), BF16),
        scratch_shapes=[pltpu.VMEM((npairs, SSD_STATE, LANES), F32), pltpu.VMEM((tq, conv_dim), F32),
                        pltpu.VMEM((tq, LANES), F32), pltpu.VMEM((tq, LANES), F32)],
        compiler_params=_cparams(("arbitrary", "arbitrary")),
    )(xbc_pre, xbc_pre, z, dt_pre, conv_w, conv_b, dt_bias, a_log, d_skip, norm_g, t64)


def ssd_layer(h, bsz, seq, w_in, conv_w, conv_b, dt_bias, a_log, d_skip, norm_g, w_out, ln_g, ln_b, alpha):
    d_inner = w_out.shape[0]
    conv_dim = conv_w.shape[1]
    w_in = w_in.astype(BF16)
    z = proj(h, w_in[:, :d_inner], BF16)
    xbc_pre = proj(h, w_in[:, d_inner:d_inner + conv_dim], F32)
    dt_pre = proj(h, _pad_to(w_in[:, d_inner + conv_dim:], 1, LANES), F32)
    pad_row = lambda t: _pad_to(t.reshape(1, -1), 1, LANES)
    y = ssd_core(xbc_pre, z, dt_pre, conv_w, conv_b.reshape(1, -1), pad_row(dt_bias), pad_row(a_log),
                 pad_row(d_skip), norm_g.reshape(1, -1), bsz, seq)
    return out_proj_ln(y, w_out.astype(BF16), h, ln_g, ln_b, alpha)


ROPE_HALF = 16


def _rope_table_body(pos_ref, freq_ref, cos_ref, sin_ref):
    ang = pos_ref[...].astype(F32) * freq_ref[...]
    lane = lax.broadcasted_iota(jnp.int32, (1, LANES), 1)
    sn = jnp.sin(ang)
    cos_ref[...] = jnp.cos(ang)
    sin_ref[...] = jnp.where(lane < ROPE_HALF, -sn, jnp.where(lane < 2 * ROPE_HALF, sn, 0.0))


def rope_tables(pos, freq, *, tm=1024):
    n = pos.shape[0]
    tm = min(tm, n)
    band = pl.BlockSpec((tm, LANES), lambda i: (i, 0))
    return pl.pallas_call(
        _rope_table_body,
        grid=(n // tm,),
        in_specs=[pl.BlockSpec((tm, 1), lambda i: (i, 0)), pl.BlockSpec((1, LANES), lambda i: (0, 0))],
        out_specs=[band, band],
        out_shape=[jax.ShapeDtypeStruct((n, LANES), F32)] * 2,
        compiler_params=_cparams(("parallel",)),
    )(pos, freq)


def _post_rope(scale, acc, cos, sin):
    lane = lax.broadcasted_iota(jnp.int32, (1, LANES), 1)
    outs = []
    for j in range(acc.shape[1] // LANES):
        t = acc[:, j * LANES:(j + 1) * LANES]
        partner = jnp.where(lane < ROPE_HALF, pltpu.roll(t, LANES - ROPE_HALF, axis=1), pltpu.roll(t, ROPE_HALF, axis=1))
        outs.append((t * cos + partner * sin) * scale)
    return (jnp.concatenate(outs, axis=1),)


def _attn_body(qi_ref, ki_ref, last_ref, q_ref, k_ref, v_ref, lq1_ref, lk1_ref, lq2_ref, lk2_ref, sg_ref,
               out_ref, m_ref, l_ref, acc_ref, *, tq, tk, hd):
    p = pl.program_id(2)
    qi, ki = qi_ref[p], ki_ref[p]

    @pl.when(ki == 0)
    def _():
        m_ref[...] = jnp.full_like(m_ref, NEG_BIG)
        l_ref[...] = jnp.zeros_like(l_ref)
        acc_ref[...] = jnp.zeros_like(acc_ref)

    q_chunk = (qi * tq + lax.broadcasted_iota(jnp.int32, (tq, 1), 0)) // CHUNK
    k_chunk = (ki * tk + lax.broadcasted_iota(jnp.int32, (1, tk), 1)) // CHUNK
    visible = k_chunk <= q_chunk
    v = v_ref[...]
    for j in range(2):
        sl = slice(j * hd, (j + 1) * hd)
        s = jnp.where(visible, _dot_nt(q_ref[:, sl], k_ref[:, sl]), NEG_BIG)
        m_old = m_ref[j]
        m_new = jnp.maximum(m_old, jnp.max(s, axis=-1, keepdims=True))
        scale = jnp.exp(m_old - m_new)
        pe = jnp.exp(s - m_new)
        l_ref[j] = scale * l_ref[j] + jnp.sum(pe, axis=-1, keepdims=True)
        acc_ref[j] = scale * acc_ref[j] + _dot(pe.astype(BF16), v)
        m_ref[j] = m_new

    @pl.when(last_ref[p] == 1)
    def _():
        lam = (jnp.exp(jnp.sum(lq1_ref[...] * lk1_ref[...], axis=-1, keepdims=True))
               - jnp.exp(jnp.sum(lq2_ref[...] * lk2_ref[...], axis=-1, keepdims=True)) + DIF_LAMBDA_INIT)
        o = acc_ref[0] / l_ref[0] - lam * (acc_ref[1] / l_ref[1])
        o = o * lax.rsqrt(jnp.mean(o * o, axis=-1, keepdims=True) + 1e-5) * sg_ref[...]
        out_ref[...] = (o * (1.0 - DIF_LAMBDA_INIT)).astype(out_ref.dtype)


def diff_attention(q, k, v, lq1, lk1, lq2, lk2, subln_g, bsz, seq, *, tq=512, tk=512):
    n, d = q.shape
    hw = d // DIF_HEADS
    hd = hw // 2
    tq, tk = min(tq, seq), min(tk, seq)
    assert tq % tk == 0 and tk % CHUNK == 0
    nq, nk = seq // tq, seq // tk
    pairs = [(a, b) for a in range(nq) for b in range((a + 1) * tq // tk)]
    qi = jnp.asarray([a for a, _ in pairs], jnp.int32)
    ki = jnp.asarray([b for _, b in pairs], jnp.int32)
    last = jnp.asarray([1 if b == (a + 1) * tq // tk - 1 else 0 for a, b in pairs], jnp.int32)
    vec = pl.BlockSpec((1, hd), lambda b, h, p, qi, ki, last: (0, 0))
    grid_spec = pltpu.PrefetchScalarGridSpec(
        num_scalar_prefetch=3,
        grid=(bsz, DIF_HEADS, len(pairs)),
        in_specs=[
            pl.BlockSpec((tq, hw), lambda b, h, p, qi, ki, last: (b * nq + qi[p], h)),
            pl.BlockSpec((tk, hw), lambda b, h, p, qi, ki, last: (b * nk + ki[p], h)),
            pl.BlockSpec((tk, hw), lambda b, h, p, qi, ki, last: (b * nk + ki[p], h)),
            vec, vec, vec, vec,
            pl.BlockSpec((1, hw), lambda b, h, p, qi, ki, last: (0, 0)),
        ],
        out_specs=pl.BlockSpec((tq, hw), lambda b, h, p, qi, ki, last: (b * nq + qi[p], h)),
        scratch_shapes=[pltpu.VMEM((2, tq, 1), F32), pltpu.VMEM((2, tq, 1), F32), pltpu.VMEM((2, tq, hw), F32)],
    )
    return pl.pallas_call(
        functools.partial(_attn_body, tq=tq, tk=tk, hd=hd),
        grid_spec=grid_spec,
        out_shape=jax.ShapeDtypeStruct((n, d), BF16),
        compiler_params=_cparams(("parallel", "parallel", "arbitrary")),
    )(qi, ki, last, q, k, v, lq1, lk1, lq2, lk2, subln_g)


def diff_layer(h, positions, bsz, seq, w_in, lq1, lk1, lq2, lk2, subln_g, w_out, ln_g, ln_b, alpha):
    d = h.shape[1]
    hd = d // DIF_HEADS // 2
    rope = hd // 4
    assert hd == LANES and rope == 2 * ROPE_HALF
    inv_freq = ROPE_THETA ** (-jnp.arange(0, rope, 2, dtype=F32) / rope)
    freq = jnp.concatenate([inv_freq, inv_freq, jnp.zeros((LANES - rope,), F32)]).reshape(1, LANES)
    cos, sin = rope_tables(positions.reshape(bsz * seq, 1), freq)
    w_in = w_in.astype(BF16)
    rope_mm = lambda w, scale: fused_matmul([h], w, bands=[cos, sin], pre=_pre_cast,
                                            post=functools.partial(_post_rope, scale), out_dtypes=[BF16],
                                            tm=512, tn=1024, tk=2048)[0]
    q = rope_mm(w_in[:, :d], hd ** -0.5)
    k = rope_mm(w_in[:, d:2 * d], 1.0)
    v = proj(h, w_in[:, 2 * d:], BF16)
    v2 = lambda t: t.reshape(1, -1)
    o = diff_attention(q, k, v, v2(lq1), v2(lk1), v2(lq2), v2(lk2), v2(subln_g), bsz, seq)
    return out_proj_ln(o, w_out.astype(BF16), h, ln_g, ln_b, alpha)


def _post_gelu(acc):
    c = math.sqrt(2.0 / math.pi)
    return (0.5 * acc * (1.0 + jnp.tanh(c * (acc + 0.044715 * acc * acc * acc))),)


def _lru_body(u_ref, halo_ref, gate_ref, cw_ref, cb_ref, wa_ref, ba_ref, wx_ref, bx_ref, lam_ref,
              out_ref, h_ref, a_ref, b_ref, *, tq, width):
    s = pl.program_id(1)

    @pl.when(s == 0)
    def _():
        h_ref[...] = jnp.zeros_like(h_ref)

    halo = jnp.where(s == 0, 0.0, halo_ref[...])
    u = _causal_conv4(u_ref[...], halo, cw_ref[...], cb_ref[...])

    nsp = _softplus(-lam_ref[...])
    blk = width // LRU_BLOCKS
    for j in range(LRU_BLOCKS):
        sl = slice(j * blk, (j + 1) * blk)
        uj = u[:, sl]
        ub = uj.astype(BF16)
        r = jax.nn.sigmoid(_dot(ub, wa_ref[j]) + ba_ref[:, sl])
        i = jax.nn.sigmoid(_dot(ub, wx_ref[j]) + bx_ref[:, sl])
        log_a = -LRU_C * r * nsp[:, sl]
        a_ref[:, sl] = jnp.exp(log_a)
        b_ref[:, sl] = jnp.sqrt(1.0 - jnp.exp(2.0 * log_a)) * (i * uj)

    def step(t, h):
        h = a_ref[pl.ds(t, 1), :] * h + b_ref[pl.ds(t, 1), :]
        b_ref[pl.ds(t, 1), :] = h
        return h

    h_ref[0:1, :] = lax.fori_loop(0, tq, step, h_ref[0:1, :], unroll=8)
    out_ref[...] = (gate_ref[...].astype(F32) * b_ref[...]).astype(out_ref.dtype)


def lru_core(u_pre, gate, conv_w, conv_b, w_a, b_a, w_x, b_x, lam, bsz, seq, *, tq=512):
    n, width = u_pre.shape
    tq = min(tq, seq)
    nblk = seq // tq
    per = tq // SUBLANES
    row = lambda b, s: (b * nblk + s, 0)
    full = lambda arr: pl.BlockSpec(arr.shape, lambda b, s: (0,) * arr.ndim)
    return pl.pallas_call(
        functools.partial(_lru_body, tq=tq, width=width),
        grid=(bsz, nblk),
        in_specs=[
            pl.BlockSpec((tq, width), row),
            pl.BlockSpec((SUBLANES, width), lambda b, s: (jnp.maximum((b * nblk + s) * per - 1, 0), 0)),
            pl.BlockSpec((tq, width), row),
            full(conv_w), full(conv_b), full(w_a), full(b_a), full(w_x), full(b_x), full(lam),
        ],
        out_specs=pl.BlockSpec((tq, width), row),
        out_shape=jax.ShapeDtypeStruct((n, width), BF16),
        scratch_shapes=[pltpu.VMEM((SUBLANES, width), F32), pltpu.VMEM((tq, width), F32),
                        pltpu.VMEM((tq, width), F32)],
        compiler_params=_cparams(("arbitrary", "arbitrary")),
    )(u_pre, u_pre, gate, conv_w, conv_b, w_a, b_a, w_x, b_x, lam)


def rglru_layer(h, bsz, seq, w_in, conv_w, conv_b, w_a, b_a, w_x, b_x, lam, w_out, ln_g, ln_b, alpha):
    width = w_in.shape[1] // 2
    w_in = w_in.astype(BF16)
    (gate,) = fused_matmul([h], w_in[:, :width], pre=_pre_cast, post=_post_gelu, out_dtypes=[BF16],
                           tm=512, tn=1024, tk=2048)
    u_pre = proj(h, w_in[:, width:], F32)
    z = lru_core(u_pre, gate, conv_w, conv_b.reshape(1, -1), w_a.astype(BF16), b_a.reshape(1, -1),
                 w_x.astype(BF16), b_x.reshape(1, -1), lam.reshape(1, -1), bsz, seq)
    return out_proj_ln(z, w_out.astype(BF16), h, ln_g, ln_b, alpha)


def kernel(x, positions, rwkv_mu, rwkv_w_in, rwkv_w0, rwkv_w1, rwkv_w2, rwkv_a0, rwkv_a1, rwkv_a2, rwkv_g1, rwkv_g2, rwkv_k_k, rwkv_k_a, rwkv_r_k, rwkv_ln_g, rwkv_ln_b, rwkv_w_out, ssd_w_in, ssd_conv_w, ssd_conv_b, ssd_dt_bias, ssd_a_log, ssd_d, ssd_norm_g, ssd_w_out, dif_w_in, dif_lq1, dif_lk1, dif_lq2, dif_lk2, dif_subln_g, dif_w_out, lru_w_in, lru_conv_w, lru_conv_b, lru_w_a, lru_b_a, lru_w_x, lru_b_x, lru_lam, lru_w_out, ffn_w_in, ffn_w_out, ln_g, ln_b):
    bsz, seq, d = x.shape
    depth = ffn_w_in.shape[0]
    alpha = (2 * depth) ** 0.25
    h = x.reshape(bsz * seq, d)
    for i in range(depth):
        m = i % 4
        g0, b0 = ln_g[i, 0].reshape(1, d), ln_b[i, 0].reshape(1, d)
        g1, b1 = ln_g[i, 1].reshape(1, d), ln_b[i, 1].reshape(1, d)
        if m == 0:
            h = rwkv_layer(h, bsz, seq, rwkv_mu, rwkv_w_in, rwkv_w0, rwkv_w1, rwkv_w2, rwkv_a0, rwkv_a1, rwkv_a2,
                           rwkv_g1, rwkv_g2, rwkv_k_k, rwkv_k_a, rwkv_r_k, rwkv_ln_g, rwkv_ln_b, rwkv_w_out,
                           g0, b0, alpha)
        elif m == 1:
            h = ssd_layer(h, bsz, seq, ssd_w_in, ssd_conv_w, ssd_conv_b, ssd_dt_bias, ssd_a_log, ssd_d,
                          ssd_norm_g, ssd_w_out, g0, b0, alpha)
        elif m == 2:
            h = diff_layer(h, positions, bsz, seq, dif_w_in, dif_lq1, dif_lk1, dif_lq2, dif_lk2, dif_subln_g,
                           dif_w_out, g0, b0, alpha)
        else:
            h = rglru_layer(h, bsz, seq, lru_w_in, lru_conv_w, lru_conv_b, lru_w_a, lru_b_a, lru_w_x, lru_b_x,
                            lru_lam, lru_w_out, g0, b0, alpha)
        h = ffn_ln(h, ffn_w_in[i].astype(BF16), ffn_w_out[i].astype(BF16), g1, b1, alpha)
    return h.reshape(bsz, seq, d)
```

```python
import functools
import math

import numpy as np
import jax
import jax.numpy as jnp
from jax import lax
from jax.experimental import pallas as pl
from jax.experimental.pallas import tpu as pltpu

F32 = jnp.float32
BF16 = jnp.bfloat16
HI = lax.Precision.HIGHEST

LANES = 128
SUBLANES = 8
VMEM_LIMIT_BYTES = 56 * 1024 * 1024

LN_EPS = 1e-5
CHUNK = 64
CHUNK_SHIFT = 6
RW_HEAD = 64
RW_GN_EPS = 64e-5
SSD_HEAD = 64
SSD_GROUPS = 8
SSD_STATE = 128
DIF_HEADS = 8
ROPE_THETA = 500000.0
DIF_LAYER = 2
DIF_LAMBDA_INIT = 0.8 - 0.6 * math.exp(-0.3 * DIF_LAYER)
LRU_BLOCKS = 16
LRU_C = 8.0
NEG_BIG = -1e30


def _cparams(sem):
    return pltpu.CompilerParams(dimension_semantics=sem, vmem_limit_bytes=VMEM_LIMIT_BYTES)


def _dot(a, b):
    return jnp.dot(a, b, preferred_element_type=F32)


def _dot_nt(a, b):
    return lax.dot_general(a, b, (((1,), (1,)), ((), ())), preferred_element_type=F32)


def _dot_tn(a, b):
    return lax.dot_general(a, b, (((0,), (0,)), ((), ())), preferred_element_type=F32)


def _dot_hi(a, b):
    return jnp.dot(a, b, preferred_element_type=F32, precision=HI)


def _dot_01(m01, x):
    hi = x.astype(BF16)
    r1 = x - hi.astype(F32)
    mid = r1.astype(BF16)
    low = (r1 - mid.astype(F32)).astype(BF16)
    return _dot(m01, hi) + _dot(m01, mid) + _dot(m01, low)


def _half_sums(lo_mask, x):
    s_lo = jnp.sum(jnp.where(lo_mask, x, 0.0), axis=-1, keepdims=True)
    s_hi = jnp.sum(jnp.where(lo_mask, 0.0, x), axis=-1, keepdims=True)
    return jnp.where(lo_mask, s_lo, s_hi)


def _layer_norm(t, g, b):
    mu = jnp.mean(t, -1, keepdims=True)
    d = t - mu
    var = jnp.mean(d * d, -1, keepdims=True)
    return d * lax.rsqrt(var + LN_EPS) * g + b


def _neg_softplus_neg(z):
    return jnp.minimum(z, 0.0) - jnp.log(1.0 + jnp.exp(-jnp.abs(z)))


def _softplus(z):
    return jnp.maximum(z, 0.0) + jnp.log(1.0 + jnp.exp(-jnp.abs(z)))


def _silu(z):
    return z * jax.nn.sigmoid(z)


def _shifted(x, halo, d):
    body = pltpu.roll(x, d, axis=0)
    head = pltpu.roll(jnp.concatenate([halo, x[:SUBLANES]], axis=0), d, axis=0)[SUBLANES:]
    return jnp.concatenate([head, body[SUBLANES:]], axis=0)


def _causal_conv4(x, halo, cw, cb):
    out = x * cw[3:4, :] + cb
    for d in (1, 2, 3):
        out = out + _shifted(x, halo, d) * cw[3 - d:4 - d, :]
    return out


def _pair_cols(lo_mask, mat, h0):
    return jnp.where(lo_mask, mat[:, h0:h0 + 1], mat[:, h0 + 1:h0 + 2])


def _mm_body(*refs, n_a, n_extra, n_out, pre, post, nk):
    a = refs[:n_a]
    w = refs[n_a]
    extra = refs[n_a + 1:n_a + 1 + n_extra]
    outs = refs[n_a + 1 + n_extra:n_a + 1 + n_extra + n_out]

    def finish(total):
        res = post(total, *[r[...] for r in extra])
        for o, v in zip(outs, res):
            o[...] = v.astype(o.dtype)

    lhs = pre(*[r[...] for r in a])
    if nk == 1:
        finish(_dot(lhs, w[...]))
    else:
        acc = refs[-1]
        k = pl.program_id(2)

        @pl.when(k == 0)
        def _():
            acc[...] = jnp.zeros_like(acc)

        acc[...] += _dot(lhs, w[...])

        @pl.when(k == nk - 1)
        def _():
            finish(acc[...])


def fused_matmul(a_list, w, *, rows=(), bands=(), cols=(), pre, post, out_dtypes, tm, tn, tk, name):
    m, kdim = a_list[0].shape
    n = w.shape[1]
    tm, tn, tk = min(tm, m), min(tn, n), min(tk, kdim)
    assert m % tm == 0 and n % tn == 0 and kdim % tk == 0
    nk = kdim // tk
    in_specs = [pl.BlockSpec((tm, tk), lambda i, j, k: (i, k)) for _ in a_list]
    in_specs.append(pl.BlockSpec((tk, tn), lambda i, j, k: (k, j)))
    in_specs += [pl.BlockSpec((tm, tn), lambda i, j, k: (i, j)) for _ in rows]
    in_specs += [pl.BlockSpec((tm, b.shape[1]), lambda i, j, k: (i, 0)) for b in bands]
    in_specs += [pl.BlockSpec((1, tn), lambda i, j, k: (0, j)) for _ in cols]
    out_specs = [pl.BlockSpec((tm, tn), lambda i, j, k: (i, j)) for _ in out_dtypes]
    out_shape = [jax.ShapeDtypeStruct((m, n), dt) for dt in out_dtypes]
    body = functools.partial(_mm_body, n_a=len(a_list), n_extra=len(rows) + len(bands) + len(cols),
                             n_out=len(out_dtypes), pre=pre, post=post, nk=nk)
    return pl.pallas_call(
        body,
        grid=(m // tm, n // tn, nk),
        in_specs=in_specs,
        out_specs=out_specs,
        out_shape=out_shape,
        scratch_shapes=[pltpu.VMEM((tm, tn), F32)] if nk > 1 else [],
        compiler_params=_cparams(("parallel", "parallel", "arbitrary")),
        name=name,
    )(*a_list, w, *rows, *bands, *cols)


def _pre_cast(x):
    return x.astype(BF16)


def _post_id(acc):
    return (acc,)


def _post_ln(alpha, acc, h, g, b):
    return (_layer_norm(alpha * h + acc, g, b),)


def proj(h, w, out_dtype, name, *, tn=1024):
    (out,) = fused_matmul([h], w, pre=_pre_cast, post=_post_id, out_dtypes=[out_dtype], tm=512, tn=tn, tk=2048,
                          name=name)
    return out


def out_proj_ln(z, w, h, g, b, alpha):
    n = w.shape[1]
    (out,) = fused_matmul([z], w, rows=[h], cols=[g, b], pre=_pre_cast,
                          post=functools.partial(_post_ln, alpha), out_dtypes=[F32],
                          tm=512, tn=n, tk=1024, name="out_proj_ln")
    return out


def _ffn_body(x_ref, wg_ref, wu_ref, wo_ref, g_ref, b_ref, out_ref, acc_ref, xb_ref, *, nf, alpha):
    f = pl.program_id(1)

    @pl.when(f == 0)
    def _():
        xb_ref[...] = x_ref[...].astype(BF16)
        acc_ref[...] = jnp.zeros_like(acc_ref)

    xb = xb_ref[...]
    gate = _dot(xb, wg_ref[...])
    up = _dot(xb, wu_ref[...])
    act = (_silu(gate) * up).astype(BF16)
    acc_ref[...] += _dot(act, wo_ref[...])

    @pl.when(f == nf - 1)
    def _():
        out_ref[...] = _layer_norm(alpha * x_ref[...] + acc_ref[...], g_ref[...], b_ref[...])


def ffn_ln(x, w_in, w_out, g, b, alpha, *, tm=512, tf=512):
    m, d = x.shape
    dff = w_out.shape[0]
    tm = min(tm, m)
    nf = dff // tf
    assert dff % tf == 0 and m % tm == 0
    return pl.pallas_call(
        functools.partial(_ffn_body, nf=nf, alpha=alpha),
        grid=(m // tm, nf),
        in_specs=[
            pl.BlockSpec((tm, d), lambda i, f: (i, 0)),
            pl.BlockSpec((d, tf), lambda i, f: (0, f)),
            pl.BlockSpec((d, tf), lambda i, f: (0, nf + f)),
            pl.BlockSpec((tf, d), lambda i, f: (f, 0)),
            pl.BlockSpec((1, d), lambda i, f: (0, 0)),
            pl.BlockSpec((1, d), lambda i, f: (0, 0)),
        ],
        out_specs=pl.BlockSpec((tm, d), lambda i, f: (i, 0)),
        out_shape=jax.ShapeDtypeStruct((m, d), F32),
        scratch_shapes=[pltpu.VMEM((tm, d), F32), pltpu.VMEM((tm, d), BF16)],
        compiler_params=_cparams(("parallel", "arbitrary")),
        name="ffn_ln",
    )(x, w_in, w_in, w_out, g, b)


def _rwkv_proj_body(x_ref, halo_ref, mu_ref, w_ref, out_ref, lhs_ref, *, tm, seq):
    i = pl.program_id(0)
    n = pl.program_id(2)

    @pl.when(n == 0)
    def _():
        x = x_ref[...]
        halo = jnp.where((i * tm) % seq == 0, 0.0, halo_ref[...])
        xx = _shifted(x, halo, 1) - x
        lhs_ref[...] = (x + xx * mu_ref[...]).astype(BF16)

    out_ref[...] = _dot(lhs_ref[...], w_ref[...])


def rwkv_proj(x, mu3, w3, seq, *, tm=512, tn=512):
    m, d = x.shape
    n = w3.shape[2]
    tm = min(tm, seq)
    per = tm // SUBLANES
    return pl.pallas_call(
        functools.partial(_rwkv_proj_body, tm=tm, seq=seq),
        grid=(m // tm, 3, n // tn),
        in_specs=[
            pl.BlockSpec((tm, d), lambda i, j, c: (i, 0)),
            pl.BlockSpec((SUBLANES, d), lambda i, j, c: (jnp.maximum(i * per - 1, 0), 0)),
            pl.BlockSpec((None, 1, d), lambda i, j, c: (j, 0, 0)),
            pl.BlockSpec((None, d, tn), lambda i, j, c: (j, 0, c)),
        ],
        out_specs=pl.BlockSpec((None, tm, tn), lambda i, j, c: (j, i, c)),
        out_shape=jax.ShapeDtypeStruct((3, m, n), F32),
        scratch_shapes=[pltpu.VMEM((tm, d), BF16)],
        compiler_params=_cparams(("parallel", "arbitrary", "arbitrary")),
        name="rwkv_proj",
    )(x, x, mu3, w3)


def _rwkv_lora_body(x_ref, halo_ref, mu_ref, w1_ref, w2_ref, a1_ref, a2_ref, g1_ref, g2_ref, w0_ref, a0_ref,
                    wl_ref, a_ref, g_ref, *, tm, seq):
    i = pl.program_id(0)
    x = x_ref[...]
    halo = jnp.where((i * tm) % seq == 0, 0.0, halo_ref[...])
    xx = _shifted(x, halo, 1) - x
    mu = mu_ref[...]
    xw = (x + xx * mu[0:1, :]).astype(BF16)
    xa = (x + xx * mu[1:2, :]).astype(BF16)
    xg = (x + xx * mu[2:3, :]).astype(BF16)
    zw = w0_ref[...] + _dot(jnp.tanh(_dot(xw, w1_ref[...])).astype(BF16), w2_ref[...])
    wl_ref[...] = _neg_softplus_neg(zw) - 0.5
    a_ref[...] = jax.nn.sigmoid(a0_ref[...] + _dot(_dot(xa, a1_ref[...]).astype(BF16), a2_ref[...]))
    g_ref[...] = _dot(jax.nn.sigmoid(_dot(xg, g1_ref[...])).astype(BF16), g2_ref[...])


def rwkv_lora(x, mu3, w1, w2, a1, a2, g1, g2, w0, a0, seq, *, tm=256):
    m, d = x.shape
    tm = min(tm, seq)
    per = tm // SUBLANES
    full = lambda arr: pl.BlockSpec(arr.shape, lambda i: (0,) * arr.ndim)
    row = pl.BlockSpec((tm, d), lambda i: (i, 0))
    return pl.pallas_call(
        functools.partial(_rwkv_lora_body, tm=tm, seq=seq),
        grid=(m // tm,),
        in_specs=[row, pl.BlockSpec((SUBLANES, d), lambda i: (jnp.maximum(i * per - 1, 0), 0)),
                  full(mu3), full(w1), full(w2), full(a1), full(a2), full(g1), full(g2), full(w0), full(a0)],
        out_specs=[row, row, row],
        out_shape=[jax.ShapeDtypeStruct((m, d), F32)] * 3,
        compiler_params=_cparams(("parallel",)),
        name="rwkv_lora",
    )(x, x, mu3, w1, w2, a1, a2, g1, g2, w0, a0)


RW_PAIRS_PER_STEP = 16
RW_LEVELS = 6


def _rwkv_masks():
    n = 2 * CHUNK
    t = np.arange(n)[:, None]
    s = np.arange(n)[None, :]
    strict = (t > s).astype(np.float32)
    incl = (t >= s).astype(np.float32)
    bones = ((t // RW_HEAD) == (s // RW_HEAD)).astype(np.float32)
    tri = np.stack([strict, incl, bones])
    lvl = []
    for lv in range(RW_LEVELS):
        sz = 1 << lv
        lvl.append(((t // (2 * sz) == s // (2 * sz)) & ((t // sz) % 2 == 1) & ((s // sz) % 2 == 0)).astype(np.float32))
    t64 = incl[:CHUNK, :CHUNK]
    return jnp.asarray(tri), jnp.asarray(np.stack(lvl)), jnp.asarray(t64)


def _rwkv_core_body(r_ref, k_ref, v_ref, wl_ref, a_ref, g_ref, kk_ref, ka_ref, rk_ref, lng_ref, lnb_ref,
                    tri_ref, lvl_ref, t64_ref, out_ref, st_ref):
    c = pl.program_id(2)

    @pl.when(c == 0)
    def _():
        st_ref[...] = jnp.zeros_like(st_ref)

    strict, incl = tri_ref[0], tri_ref[1]
    eye = incl - strict
    t64 = t64_ref[...].astype(BF16)
    lo = lax.broadcasted_iota(jnp.int32, (1, LANES), 1) < RW_HEAD
    inv_head = 1.0 / RW_HEAD

    def stack(x):
        return jnp.concatenate([jnp.where(lo, x, 0.0), jnp.where(lo, 0.0, x)], axis=0).astype(BF16)

    pairs = range(RW_PAIRS_PER_STEP)
    sls = [slice(q * LANES, (q + 1) * LANES) for q in pairs]

    kk_sq = []
    for sl in sls:
        kk = k_ref[:, sl] * kk_ref[:, sl]
        kk_sq.append(_half_sums(lo, kk * kk))
    cums = [_dot_01(t64, -jnp.exp(wl_ref[:, sl])) for sl in sls]

    ops = []
    for sl, ss, cum in zip(sls, kk_sq, cums):
        k, a = k_ref[:, sl], a_ref[:, sl]
        kk = k * kk_ref[:, sl] / jnp.maximum(jnp.sqrt(ss), 1e-12)
        k2 = k * (1.0 + (a - 1.0) * ka_ref[:, sl])
        lw = -jnp.exp(wl_ref[:, sl])
        cl = cum[CHUNK - 1:CHUNK, :]
        e_neg = jnp.exp(-cum)
        e_end = jnp.exp(cl - cum)
        kb = kk * a
        ops.append(dict(
            a_s=stack(-kk * jnp.exp(cum - lw)),
            r_s=stack(r_ref[:, sl] * jnp.exp(cum)),
            b_s=stack(kb * e_neg), k_s=stack(k2 * e_neg), v_s=stack(v_ref[:, sl]),
            bh_s=stack(kb * e_end), kh_s=stack(k2 * e_end),
            w_end=jnp.exp(cl), k2=k2))

    a_ab = [strict * _dot_nt(o["a_s"], o["b_s"]) for o in ops]
    a_ak = [(strict * _dot_nt(o["a_s"], o["k_s"])).astype(BF16) for o in ops]
    r_b = [(incl * _dot_nt(o["r_s"], o["b_s"])).astype(BF16) for o in ops]
    r_k = [(incl * _dot_nt(o["r_s"], o["k_s"])).astype(BF16) for o in ops]

    t_inv = [eye + m * lvl_ref[0] for m in a_ab]
    for lv in range(1, RW_LEVELS):
        tb = [t.astype(BF16) for t in t_inv]
        xs = [_dot(b, (m * lvl_ref[lv]).astype(BF16)).astype(BF16) for b, m in zip(tb, a_ab)]
        t_inv = [t + _dot(x, b) for t, x, b in zip(t_inv, xs, tb)]

    s_old = [st_ref[q] for q in pairs]
    s_b = [s.astype(BF16) for s in s_old]
    rhs = [(_dot_nt(o["a_s"], s) + _dot(m, o["v_s"])).astype(BF16) for o, s, m in zip(ops, s_b, a_ak)]
    u_b = [_dot(t.astype(BF16), x).astype(BF16) for t, x in zip(t_inv, rhs)]
    y_s = [_dot_nt(o["r_s"], s) + _dot(mb, u) + _dot(mk, o["v_s"])
           for o, s, mb, u, mk in zip(ops, s_b, r_b, u_b, r_k)]
    for q, (o, s, u) in enumerate(zip(ops, s_old, u_b)):
        st_ref[q] = s * o["w_end"] + _dot_tn(u, o["bh_s"]) + _dot_tn(o["v_s"], o["kh_s"])

    ys = [y[:CHUNK] + y[CHUNK:] for y in y_s]
    mus = [_half_sums(lo, y) * inv_head for y in ys]
    ds = [y - mu for y, mu in zip(ys, mus)]
    var = [_half_sums(lo, d * d) * inv_head for d in ds]
    bonus = [_half_sums(lo, r_ref[:, sl] * o["k2"] * rk_ref[:, sl]) for sl, o in zip(sls, ops)]
    for sl, d, vr, bn in zip(sls, ds, var, bonus):
        yn = d * lax.rsqrt(vr + RW_GN_EPS) * lng_ref[:, sl] + lnb_ref[:, sl]
        out_ref[:, sl] = ((yn + bn * v_ref[:, sl]) * g_ref[:, sl]).astype(out_ref.dtype)


def rwkv_core(rkv, wl, a, g, k_k, k_a, r_k, ln_g, ln_b, bsz, seq):
    _, m, d = rkv.shape
    width = RW_PAIRS_PER_STEP * LANES
    nc = seq // CHUNK
    tri, lvl, t64 = _rwkv_masks()
    row = lambda b, p, c: (b * nc + c, p)
    rkv_spec = lambda j: pl.BlockSpec((None, CHUNK, width), lambda b, p, c: (j, b * nc + c, p))
    act = pl.BlockSpec((CHUNK, width), row)
    par = pl.BlockSpec((1, width), lambda b, p, c: (0, p))
    full = lambda arr: pl.BlockSpec(arr.shape, lambda b, p, c: (0,) * arr.ndim)
    return pl.pallas_call(
        _rwkv_core_body,
        grid=(bsz, d // width, nc),
        in_specs=[rkv_spec(0), rkv_spec(1), rkv_spec(2), act, act, act, par, par, par, par, par,
                  full(tri), full(lvl), full(t64)],
        out_specs=act,
        out_shape=jax.ShapeDtypeStruct((m, d), BF16),
        scratch_shapes=[pltpu.VMEM((RW_PAIRS_PER_STEP, LANES, LANES), F32)],
        compiler_params=_cparams(("parallel", "parallel", "arbitrary")),
        name="rwkv_core",
    )(rkv, rkv, rkv, wl, a, g, k_k, k_a, r_k, ln_g, ln_b, tri, lvl, t64)


def _pad_to(x, axis, size):
    pad = [(0, 0)] * x.ndim
    pad[axis] = (0, size - x.shape[axis])
    return jnp.pad(x, pad)


def rwkv_layer(h, bsz, seq, mu, w_in, w0, w1, w2, a0, a1, a2, g1, g2, k_k, k_a, r_k, gn_g, gn_b, w_out,
               ln_g, ln_b, alpha):
    d = h.shape[1]
    v2 = lambda t: t.reshape(1, d)
    mu_rkv = jnp.stack([mu[0], mu[2], mu[3]]).reshape(3, 1, d)
    mu_lora = jnp.stack([mu[1], mu[4], mu[5]])
    rkv = rwkv_proj(h, mu_rkv, w_in.astype(BF16), seq)
    wl, a, g = rwkv_lora(
        h, mu_lora,
        _pad_to(w1, 1, LANES).astype(BF16), _pad_to(w2, 0, LANES).astype(BF16),
        _pad_to(a1, 1, LANES).astype(BF16), _pad_to(a2, 0, LANES).astype(BF16),
        g1.astype(BF16), g2.astype(BF16), v2(w0), v2(a0), seq)
    z = rwkv_core(rkv, wl, a, g, v2(k_k), v2(k_a), v2(r_k), v2(gn_g), v2(gn_b), bsz, seq)
    return out_proj_ln(z, w_out.astype(BF16), h, ln_g, ln_b, alpha)


def _ssd_body(xbc_ref, halo_ref, z_ref, dtp_ref, cw_ref, cb_ref, dtb_ref, alog_ref, dsk_ref, ng_ref, t64_ref,
              out_ref, st_ref, xc_ref, dt_ref, da_ref, *, tq, d_inner):
    s = pl.program_id(1)

    @pl.when(s == 0)
    def _():
        st_ref[...] = jnp.zeros_like(st_ref)

    halo = jnp.where(s == 0, 0.0, halo_ref[...])
    xc_ref[...] = _silu(_causal_conv4(xbc_ref[...], halo, cw_ref[...], cb_ref[...]))
    dt_all = _softplus(dtp_ref[...] + dtb_ref[...])
    dt_ref[...] = dt_all
    da_ref[...] = dt_all * (-jnp.exp(alog_ref[...]))

    t64 = t64_ref[...]
    t64b = t64.astype(BF16)
    lo = lax.broadcasted_iota(jnp.int32, (1, LANES), 1) < SSD_HEAD
    tril = t64 > 0.5
    dsk = dsk_ref[...]
    gn = SSD_GROUPS * SSD_STATE
    pairs_per_group = d_inner // SSD_GROUPS // LANES

    def chunk(ci, carry):
        r0 = pl.multiple_of(ci * CHUNK, CHUNK)
        rows = pl.ds(r0, CHUNK)
        dt_c = dt_ref[rows, :]
        cs = _dot_01(t64b, da_ref[rows, :])
        cs_t = cs.T
        cs_last = cs[CHUNK - 1:CHUNK, :]
        ecs = jnp.exp(cs)
        to_end = jnp.exp(cs_last - cs)
        e_last = jnp.exp(cs_last)
        for g in range(SSD_GROUPS):
            bg = xc_ref[rows, d_inner + g * SSD_STATE:d_inner + (g + 1) * SSD_STATE]
            cg = xc_ref[rows, d_inner + gn + g * SSD_STATE:d_inner + gn + (g + 1) * SSD_STATE].astype(BF16)
            cb = _dot_nt(cg, bg.astype(BF16))
            bg_t = bg.T.astype(BF16)
            ps = [g * pairs_per_group + q for q in range(pairs_per_group)]
            sls = [slice(p * LANES, (p + 1) * LANES) for p in ps]
            xs = [xc_ref[rows, sl] for sl in sls]
            xdts = [x * _pair_cols(lo, dt_c, 2 * p) for x, p in zip(xs, ps)]

            def decay_mat(hh):
                seg = jnp.minimum(cs[:, hh:hh + 1] - cs_t[hh:hh + 1, :], 0.0)
                return jnp.where(tril, cb * jnp.exp(seg), 0.0).astype(BF16)

            mats = [(decay_mat(2 * p), decay_mat(2 * p + 1)) for p in ps]
            x_lo = [jnp.where(lo, xdt, 0.0).astype(BF16) for xdt in xdts]
            x_hi = [jnp.where(lo, 0.0, xdt).astype(BF16) for xdt in xdts]
            x_end = [(xdt * _pair_cols(lo, to_end, 2 * p)).astype(BF16) for xdt, p in zip(xdts, ps)]
            sts = [st_ref[p] for p in ps]
            intra = [_dot(m[0], a) + _dot(m[1], b) for m, a, b in zip(mats, x_lo, x_hi)]
            inter = [_dot(cg, st.astype(BF16)) for st in sts]
            upd = [_dot(bg_t, xe) for xe in x_end]
            for p, st, u in zip(ps, sts, upd):
                st_ref[p] = st * _pair_cols(lo, e_last, 2 * p) + u
            ys = []
            ssq = jnp.zeros((CHUNK, 1), F32)
            for p, sl, x, ya, yb in zip(ps, sls, xs, intra, inter):
                y = ya + yb * _pair_cols(lo, ecs, 2 * p) + x * _pair_cols(lo, dsk, 2 * p)
                y = y * _silu(z_ref[rows, sl].astype(F32))
                ssq = ssq + jnp.sum(y * y, axis=-1, keepdims=True)
                ys.append(y)
            inv = lax.rsqrt(ssq * (1.0 / (pairs_per_group * LANES)) + 1e-5)
            for sl, y in zip(sls, ys):
                out_ref[rows, sl] = (y * inv * ng_ref[:, sl]).astype(out_ref.dtype)
        return carry

    lax.fori_loop(0, tq // CHUNK, chunk, 0)


def ssd_core(xbc_pre, z, dt_pre, conv_w, conv_b, dt_bias, a_log, d_skip, norm_g, bsz, seq, *, tq=256):
    n, conv_dim = xbc_pre.shape
    d_inner = z.shape[1]
    tq = min(tq, seq)
    nblk = seq // tq
    per = tq // SUBLANES
    npairs = d_inner // LANES
    t64 = _rwkv_masks()[2]
    row = lambda b, s: (b * nblk + s, 0)
    full = lambda arr: pl.BlockSpec(arr.shape, lambda b, s: (0,) * arr.ndim)
    return pl.pallas_call(
        functools.partial(_ssd_body, tq=tq, d_inner=d_inner),
        grid=(bsz, nblk),
        in_specs=[
            pl.BlockSpec((tq, conv_dim), row),
            pl.BlockSpec((SUBLANES, conv_dim), lambda b, s: (jnp.maximum((b * nblk + s) * per - 1, 0), 0)),
            pl.BlockSpec((tq, d_inner), row),
            pl.BlockSpec((tq, LANES), row),
            full(conv_w), full(conv_b), full(dt_bias), full(a_log), full(d_skip), full(norm_g), full(t64),
        ],
        out_specs=pl.BlockSpec((tq, d_inner), row),
        out_shape=jax.ShapeDtypeStruct((n, d_inner), BF16),
        scratch_shapes=[pltpu.VMEM((npairs, SSD_STATE, LANES), F32), pltpu.VMEM((tq, conv_dim), F32),
                        pltpu.VMEM((tq, LANES), F32), pltpu.VMEM((tq, LANES), F32)],
        compiler_params=_cparams(("arbitrary", "arbitrary")),
        name="ssd_core",
    )(xbc_pre, xbc_pre, z, dt_pre, conv_w, conv_b, dt_bias, a_log, d_skip, norm_g, t64)


def ssd_layer(h, bsz, seq, w_in, conv_w, conv_b, dt_bias, a_log, d_skip, norm_g, w_out, ln_g, ln_b, alpha):
    d_inner = w_out.shape[0]
    conv_dim = conv_w.shape[1]
    w_in = w_in.astype(BF16)
    z = proj(h, w_in[:, :d_inner], BF16, "ssd_proj_z")
    xbc_pre = proj(h, w_in[:, d_inner:d_inner + conv_dim], F32, "ssd_proj_xbc")
    dt_pre = proj(h, _pad_to(w_in[:, d_inner + conv_dim:], 1, LANES), F32, "ssd_proj_dt")
    pad_row = lambda t: _pad_to(t.reshape(1, -1), 1, LANES)
    y = ssd_core(xbc_pre, z, dt_pre, conv_w, conv_b.reshape(1, -1), pad_row(dt_bias), pad_row(a_log),
                 pad_row(d_skip), norm_g.reshape(1, -1), bsz, seq)
    return out_proj_ln(y, w_out.astype(BF16), h, ln_g, ln_b, alpha)


ROPE_HALF = 16
ATTN_SUB_ROWS = 256


def _rope_table_body(pos_ref, freq_ref, cos_ref, sin_ref):
    ang = pos_ref[...].astype(F32) * freq_ref[...]
    lane = lax.broadcasted_iota(jnp.int32, (1, LANES), 1)
    sn = jnp.sin(ang)
    cos_ref[...] = jnp.cos(ang)
    sin_ref[...] = jnp.where(lane < ROPE_HALF, -sn, jnp.where(lane < 2 * ROPE_HALF, sn, 0.0))


def rope_tables(pos, freq, *, tm=1024):
    n = pos.shape[0]
    tm = min(tm, n)
    band = pl.BlockSpec((tm, LANES), lambda i: (i, 0))
    return pl.pallas_call(
        _rope_table_body,
        grid=(n // tm,),
        in_specs=[pl.BlockSpec((tm, 1), lambda i: (i, 0)), pl.BlockSpec((1, LANES), lambda i: (0, 0))],
        out_specs=[band, band],
        out_shape=[jax.ShapeDtypeStruct((n, LANES), F32)] * 2,
        compiler_params=_cparams(("parallel",)),
        name="rope_tables",
    )(pos, freq)


def _post_rope(scale, acc, cos, sin):
    lane = lax.broadcasted_iota(jnp.int32, (1, LANES), 1)
    outs = []
    for j in range(acc.shape[1] // LANES):
        t = acc[:, j * LANES:(j + 1) * LANES]
        partner = jnp.where(lane < ROPE_HALF, pltpu.roll(t, LANES - ROPE_HALF, axis=1), pltpu.roll(t, ROPE_HALF, axis=1))
        outs.append((t * cos + partner * sin) * scale)
    return (jnp.concatenate(outs, axis=1),)


def _attn_body(qi_ref, ki_ref, last_ref, q_ref, k_ref, v_ref, lq1_ref, lk1_ref, lq2_ref, lk2_ref, sg_ref,
               out_ref, m_ref, l_ref, acc_ref, *, tq, tk, hd):
    p = pl.program_id(2)
    qi, ki = qi_ref[p], ki_ref[p]

    @pl.when(ki == 0)
    def _():
        m_ref[...] = jnp.full_like(m_ref, NEG_BIG)
        l_ref[...] = jnp.zeros_like(l_ref)
        acc_ref[...] = jnp.zeros_like(acc_ref)

    sub = min(ATTN_SUB_ROWS, tq)
    n_sub = tq // sub

    def qk(r):
        rows = slice(r * sub, (r + 1) * sub)
        return [_dot_nt(q_ref[rows, j * hd:(j + 1) * hd], k_ref[:, j * hd:(j + 1) * hd]) for j in range(2)]

    def update(masked):
        v = v_ref[...]
        if masked:
            k_chunk = jnp.right_shift(ki * tk + lax.broadcasted_iota(jnp.int32, (1, tk), 1), CHUNK_SHIFT)
        scores = qk(0)
        for r in range(n_sub):
            nxt = qk(r + 1) if r + 1 < n_sub else None
            rows = slice(r * sub, (r + 1) * sub)
            if masked:
                row0 = qi * tq + r * sub
                q_chunk = jnp.right_shift(row0 + lax.broadcasted_iota(jnp.int32, (sub, 1), 0), CHUNK_SHIFT)
                visible = k_chunk <= q_chunk
                scores = [jnp.where(visible, s, NEG_BIG) for s in scores]
            for j in range(2):
                m_old = m_ref[j, rows, :]
                m_new = jnp.maximum(m_old, jnp.max(scores[j], axis=-1, keepdims=True))
                pe = jnp.exp(scores[j] - m_new)
                scale = jnp.exp(m_old - m_new)
                l_ref[j, rows, :] = scale * l_ref[j, rows, :] + jnp.sum(pe, axis=-1, keepdims=True)
                acc_ref[j, rows, :] = scale * acc_ref[j, rows, :] + _dot(pe.astype(BF16), v)
                m_ref[j, rows, :] = m_new
            scores = nxt

    needs_mask = (ki + 1) * tk > qi * tq + CHUNK

    @pl.when(needs_mask)
    def _():
        update(True)

    @pl.when(jnp.logical_not(needs_mask))
    def _():
        update(False)

    @pl.when(last_ref[p] == 1)
    def _():
        lam = (jnp.exp(jnp.sum(lq1_ref[...] * lk1_ref[...], axis=-1, keepdims=True))
               - jnp.exp(jnp.sum(lq2_ref[...] * lk2_ref[...], axis=-1, keepdims=True)) + DIF_LAMBDA_INIT)
        o = acc_ref[0] / l_ref[0] - lam * (acc_ref[1] / l_ref[1])
        o = o * lax.rsqrt(jnp.mean(o * o, axis=-1, keepdims=True) + 1e-5) * sg_ref[...]
        out_ref[...] = (o * (1.0 - DIF_LAMBDA_INIT)).astype(out_ref.dtype)


def diff_attention(q, k, v, lq1, lk1, lq2, lk2, subln_g, bsz, seq, *, tq=1024, tk=1024):
    n, d = q.shape
    hw = d // DIF_HEADS
    hd = hw // 2
    tq, tk = min(tq, seq), min(tk, seq)
    assert tq % CHUNK == 0 and tk % CHUNK == 0 and CHUNK == 1 << CHUNK_SHIFT
    nq, nk = seq // tq, seq // tk
    n_kv = lambda a: -(-((a + 1) * tq) // tk)
    pairs = [(a, b) for a in range(nq) for b in range(n_kv(a))]
    qi = jnp.asarray([a for a, _ in pairs], jnp.int32)
    ki = jnp.asarray([b for _, b in pairs], jnp.int32)
    last = jnp.asarray([1 if b == n_kv(a) - 1 else 0 for a, b in pairs], jnp.int32)
    vec = pl.BlockSpec((1, hd), lambda b, h, p, qi, ki, last: (0, 0))
    grid_spec = pltpu.PrefetchScalarGridSpec(
        num_scalar_prefetch=3,
        grid=(bsz, DIF_HEADS, len(pairs)),
        in_specs=[
            pl.BlockSpec((tq, hw), lambda b, h, p, qi, ki, last: (b * nq + qi[p], h)),
            pl.BlockSpec((tk, hw), lambda b, h, p, qi, ki, last: (b * nk + ki[p], h)),
            pl.BlockSpec((tk, hw), lambda b, h, p, qi, ki, last: (b * nk + ki[p], h)),
            vec, vec, vec, vec,
            pl.BlockSpec((1, hw), lambda b, h, p, qi, ki, last: (0, 0)),
        ],
        out_specs=pl.BlockSpec((tq, hw), lambda b, h, p, qi, ki, last: (b * nq + qi[p], h)),
        scratch_shapes=[pltpu.VMEM((2, tq, 1), F32), pltpu.VMEM((2, tq, 1), F32), pltpu.VMEM((2, tq, hw), F32)],
    )
    return pl.pallas_call(
        functools.partial(_attn_body, tq=tq, tk=tk, hd=hd),
        grid_spec=grid_spec,
        out_shape=jax.ShapeDtypeStruct((n, d), BF16),
        compiler_params=_cparams(("parallel", "parallel", "arbitrary")),
        name="diff_attention",
    )(qi, ki, last, q, k, v, lq1, lk1, lq2, lk2, subln_g)


def diff_layer(h, positions, bsz, seq, w_in, lq1, lk1, lq2, lk2, subln_g, w_out, ln_g, ln_b, alpha):
    d = h.shape[1]
    hd = d // DIF_HEADS // 2
    rope = hd // 4
    assert hd == LANES and rope == 2 * ROPE_HALF
    inv_freq = ROPE_THETA ** (-jnp.arange(0, rope, 2, dtype=F32) / rope)
    freq = jnp.concatenate([inv_freq, inv_freq, jnp.zeros((LANES - rope,), F32)]).reshape(1, LANES)
    cos, sin = rope_tables(positions.reshape(bsz * seq, 1), freq)
    w_in = w_in.astype(BF16)
    rope_mm = lambda w, scale, name: fused_matmul([h], w, bands=[cos, sin], pre=_pre_cast,
                                                  post=functools.partial(_post_rope, scale), out_dtypes=[BF16],
                                                  tm=512, tn=1024, tk=2048, name=name)[0]
    q = rope_mm(w_in[:, :d], hd ** -0.5, "dif_proj_q")
    k = rope_mm(w_in[:, d:2 * d], 1.0, "dif_proj_k")
    v = proj(h, w_in[:, 2 * d:], BF16, "dif_proj_v")
    v2 = lambda t: t.reshape(1, -1)
    o = diff_attention(q, k, v, v2(lq1), v2(lk1), v2(lq2), v2(lk2), v2(subln_g), bsz, seq)
    return out_proj_ln(o, w_out.astype(BF16), h, ln_g, ln_b, alpha)


def _post_gelu(acc):
    c = math.sqrt(2.0 / math.pi)
    return (0.5 * acc * (1.0 + jnp.tanh(c * (acc + 0.044715 * acc * acc * acc))),)


def _lru_body(u_ref, halo_ref, gate_ref, cw_ref, cb_ref, wa_ref, ba_ref, wx_ref, bx_ref, lam_ref,
              out_ref, h_ref, a_ref, b_ref, *, tq, width):
    s = pl.program_id(1)

    @pl.when(s == 0)
    def _():
        h_ref[...] = jnp.zeros_like(h_ref)

    halo = jnp.where(s == 0, 0.0, halo_ref[...])
    u = _causal_conv4(u_ref[...], halo, cw_ref[...], cb_ref[...])

    nsp = _softplus(-lam_ref[...])
    blk = width // LRU_BLOCKS
    for j in range(LRU_BLOCKS):
        sl = slice(j * blk, (j + 1) * blk)
        uj = u[:, sl]
        ub = uj.astype(BF16)
        r = jax.nn.sigmoid(_dot(ub, wa_ref[j]) + ba_ref[:, sl])
        i = jax.nn.sigmoid(_dot(ub, wx_ref[j]) + bx_ref[:, sl])
        log_a = -LRU_C * r * nsp[:, sl]
        a_ref[:, sl] = jnp.exp(log_a)
        b_ref[:, sl] = jnp.sqrt(1.0 - jnp.exp(2.0 * log_a)) * (i * uj)

    def step(t, h):
        h = a_ref[pl.ds(t, 1), :] * h + b_ref[pl.ds(t, 1), :]
        b_ref[pl.ds(t, 1), :] = h
        return h

    h_ref[0:1, :] = lax.fori_loop(0, tq, step, h_ref[0:1, :], unroll=8)
    out_ref[...] = (gate_ref[...].astype(F32) * b_ref[...]).astype(out_ref.dtype)


def lru_core(u_pre, gate, conv_w, conv_b, w_a, b_a, w_x, b_x, lam, bsz, seq, *, tq=512):
    n, width = u_pre.shape
    tq = min(tq, seq)
    nblk = seq // tq
    per = tq // SUBLANES
    row = lambda b, s: (b * nblk + s, 0)
    full = lambda arr: pl.BlockSpec(arr.shape, lambda b, s: (0,) * arr.ndim)
    return pl.pallas_call(
        functools.partial(_lru_body, tq=tq, width=width),
        grid=(bsz, nblk),
        in_specs=[
            pl.BlockSpec((tq, width), row),
            pl.BlockSpec((SUBLANES, width), lambda b, s: (jnp.maximum((b * nblk + s) * per - 1, 0), 0)),
            pl.BlockSpec((tq, width), row),
            full(conv_w), full(conv_b), full(w_a), full(b_a), full(w_x), full(b_x), full(lam),
        ],
        out_specs=pl.BlockSpec((tq, width), row),
        out_shape=jax.ShapeDtypeStruct((n, width), BF16),
        scratch_shapes=[pltpu.VMEM((SUBLANES, width), F32), pltpu.VMEM((tq, width), F32),
                        pltpu.VMEM((tq, width), F32)],
        compiler_params=_cparams(("arbitrary", "arbitrary")),
        name="lru_core",
    )(u_pre, u_pre, gate, conv_w, conv_b, w_a, b_a, w_x, b_x, lam)


def rglru_layer(h, bsz, seq, w_in, conv_w, conv_b, w_a, b_a, w_x, b_x, lam, w_out, ln_g, ln_b, alpha):
    width = w_in.shape[1] // 2
    w_in = w_in.astype(BF16)
    (gate,) = fused_matmul([h], w_in[:, :width], pre=_pre_cast, post=_post_gelu, out_dtypes=[BF16],
                           tm=512, tn=1024, tk=2048, name="lru_proj_gate")
    u_pre = proj(h, w_in[:, width:], F32, "lru_proj_u")
    z = lru_core(u_pre, gate, conv_w, conv_b.reshape(1, -1), w_a.astype(BF16), b_a.reshape(1, -1),
                 w_x.astype(BF16), b_x.reshape(1, -1), lam.reshape(1, -1), bsz, seq)
    return out_proj_ln(z, w_out.astype(BF16), h, ln_g, ln_b, alpha)


def kernel(x, positions, rwkv_mu, rwkv_w_in, rwkv_w0, rwkv_w1, rwkv_w2, rwkv_a0, rwkv_a1, rwkv_a2, rwkv_g1, rwkv_g2, rwkv_k_k, rwkv_k_a, rwkv_r_k, rwkv_ln_g, rwkv_ln_b, rwkv_w_out, ssd_w_in, ssd_conv_w, ssd_conv_b, ssd_dt_bias, ssd_a_log, ssd_d, ssd_norm_g, ssd_w_out, dif_w_in, dif_lq1, dif_lk1, dif_lq2, dif_lk2, dif_subln_g, dif_w_out, lru_w_in, lru_conv_w, lru_conv_b, lru_w_a, lru_b_a, lru_w_x, lru_b_x, lru_lam, lru_w_out, ffn_w_in, ffn_w_out, ln_g, ln_b):
    bsz, seq, d = x.shape
    depth = ffn_w_in.shape[0]
    alpha = (2 * depth) ** 0.25
    h = x.reshape(bsz * seq, d)
    for i in range(depth):
        m = i % 4
        g0, b0 = ln_g[i, 0].reshape(1, d), ln_b[i, 0].reshape(1, d)
        g1, b1 = ln_g[i, 1].reshape(1, d), ln_b[i, 1].reshape(1, d)
        if m == 0:
            h = rwkv_layer(h, bsz, seq, rwkv_mu, rwkv_w_in, rwkv_w0, rwkv_w1, rwkv_w2, rwkv_a0, rwkv_a1, rwkv_a2,
                           rwkv_g1, rwkv_g2, rwkv_k_k, rwkv_k_a, rwkv_r_k, rwkv_ln_g, rwkv_ln_b, rwkv_w_out,
                           g0, b0, alpha)
        elif m == 1:
            h = ssd_layer(h, bsz, seq, ssd_w_in, ssd_conv_w, ssd_conv_b, ssd_dt_bias, ssd_a_log, ssd_d,
                          ssd_norm_g, ssd_w_out, g0, b0, alpha)
        elif m == 2:
            h = diff_layer(h, positions, bsz, seq, dif_w_in, dif_lq1, dif_lk1, dif_lq2, dif_lk2, dif_subln_g,
                           dif_w_out, g0, b0, alpha)
        else:
            h = rglru_layer(h, bsz, seq, lru_w_in, lru_conv_w, lru_conv_b, lru_w_a, lru_b_a, lru_w_x, lru_b_x,
                            lru_lam, lru_w_out, g0, b0, alpha)
        h = ffn_ln(h, ffn_w_in[i].astype(BF16), ffn_w_out[i].astype(BF16), g1, b1, alpha)
    return h.reshape(bsz, seq, d)
```

```python
import functools
import math

import numpy as np
import jax
import jax.numpy as jnp
from jax import lax
from jax.experimental import pallas as pl
from jax.experimental.pallas import tpu as pltpu

F32 = jnp.float32
BF16 = jnp.bfloat16
HI = lax.Precision.HIGHEST

LANES = 128
SUBLANES = 8
BF16_ROWS = 16
VMEM_LIMIT_BYTES = 56 * 1024 * 1024

LN_EPS = 1e-5
CHUNK = 64
CHUNK_SHIFT = 6
RW_HEAD = 64
RW_GN_EPS = 64e-5
SSD_HEAD = 64
SSD_GROUPS = 8
SSD_STATE = 128
DIF_HEADS = 8
ROPE_THETA = 500000.0
DIF_LAYER = 2
DIF_LAMBDA_INIT = 0.8 - 0.6 * math.exp(-0.3 * DIF_LAYER)
LRU_BLOCKS = 16
LRU_C = 8.0
NEG_BIG = -1e30


def _cparams(sem):
    return pltpu.CompilerParams(dimension_semantics=sem, vmem_limit_bytes=VMEM_LIMIT_BYTES)


def _dot(a, b):
    return jnp.dot(a, b, preferred_element_type=F32)


def _dot_nt(a, b):
    return lax.dot_general(a, b, (((1,), (1,)), ((), ())), preferred_element_type=F32)


def _dot_tn(a, b):
    return lax.dot_general(a, b, (((0,), (0,)), ((), ())), preferred_element_type=F32)


def _dot_hi(a, b):
    return jnp.dot(a, b, preferred_element_type=F32, precision=HI)


def _dot_01(m01, x):
    hi = x.astype(BF16)
    r1 = x - hi.astype(F32)
    mid = r1.astype(BF16)
    low = (r1 - mid.astype(F32)).astype(BF16)
    return _dot(m01, hi) + _dot(m01, mid) + _dot(m01, low)


def _half_sums(lo_mask, x):
    s_lo = jnp.sum(jnp.where(lo_mask, x, 0.0), axis=-1, keepdims=True)
    s_hi = jnp.sum(jnp.where(lo_mask, 0.0, x), axis=-1, keepdims=True)
    return jnp.where(lo_mask, s_lo, s_hi)


def _layer_norm(t, g, b):
    mu = jnp.mean(t, -1, keepdims=True)
    d = t - mu
    var = jnp.mean(d * d, -1, keepdims=True)
    return d * lax.rsqrt(var + LN_EPS) * g + b


def _neg_softplus_neg(z):
    return jnp.minimum(z, 0.0) - jnp.log(1.0 + jnp.exp(-jnp.abs(z)))


def _softplus(z):
    return jnp.maximum(z, 0.0) + jnp.log(1.0 + jnp.exp(-jnp.abs(z)))


def _silu(z):
    return z * jax.nn.sigmoid(z)


def _shifted(x, halo, d):
    head = pltpu.roll(jnp.concatenate([halo, x[:SUBLANES]], axis=0), d, axis=0)[SUBLANES:]
    if x.shape[0] == SUBLANES:
        return head
    return jnp.concatenate([head, pltpu.roll(x, d, axis=0)[SUBLANES:]], axis=0)


def _causal_conv4(x, halo, cw, cb):
    out = x * cw[3:4, :] + cb
    for d in (1, 2, 3):
        out = out + _shifted(x, halo, d) * cw[3 - d:4 - d, :]
    return out


def _causal_conv4_ref(x_ref, halo, cw, cb):
    n = x_ref.shape[0]
    head = _causal_conv4(x_ref[0:SUBLANES, :], halo, cw, cb)
    body = x_ref[SUBLANES:n, :] * cw[3:4, :] + cb
    for d in (1, 2, 3):
        body = body + x_ref[SUBLANES - d:n - d, :] * cw[3 - d:4 - d, :]
    return jnp.concatenate([head, body], axis=0)


def _mm_body(*refs, n_a, n_extra, n_out, pre, post, nk):
    a = refs[:n_a]
    w = refs[n_a]
    extra = refs[n_a + 1:n_a + 1 + n_extra]
    outs = refs[n_a + 1 + n_extra:n_a + 1 + n_extra + n_out]

    def finish(total):
        res = post(total, *[r[...] for r in extra])
        for o, v in zip(outs, res):
            o[...] = v.astype(o.dtype)

    lhs = pre(*[r[...] for r in a])
    if nk == 1:
        finish(_dot(lhs, w[...]))
    else:
        acc = refs[-1]
        k = pl.program_id(2)

        @pl.when(k == 0)
        def _():
            acc[...] = jnp.zeros_like(acc)

        acc[...] += _dot(lhs, w[...])

        @pl.when(k == nk - 1)
        def _():
            finish(acc[...])


def fused_matmul(a_list, w, *, rows=(), bands=(), cols=(), pre, post, out_dtypes, tm, tn, tk, name):
    m, kdim = a_list[0].shape
    n = w.shape[1]
    tm, tn, tk = min(tm, m), min(tn, n), min(tk, kdim)
    assert m % tm == 0 and n % tn == 0 and kdim % tk == 0
    nk = kdim // tk
    in_specs = [pl.BlockSpec((tm, tk), lambda i, j, k: (i, k)) for _ in a_list]
    in_specs.append(pl.BlockSpec((tk, tn), lambda i, j, k: (k, j)))
    in_specs += [pl.BlockSpec((tm, tn), lambda i, j, k: (i, j)) for _ in rows]
    in_specs += [pl.BlockSpec((tm, b.shape[1]), lambda i, j, k: (i, 0)) for b in bands]
    in_specs += [pl.BlockSpec((1, tn), lambda i, j, k: (0, j)) for _ in cols]
    out_specs = [pl.BlockSpec((tm, tn), lambda i, j, k: (i, j)) for _ in out_dtypes]
    out_shape = [jax.ShapeDtypeStruct((m, n), dt) for dt in out_dtypes]
    body = functools.partial(_mm_body, n_a=len(a_list), n_extra=len(rows) + len(bands) + len(cols),
                             n_out=len(out_dtypes), pre=pre, post=post, nk=nk)
    return pl.pallas_call(
        body,
        grid=(m // tm, n // tn, nk),
        in_specs=in_specs,
        out_specs=out_specs,
        out_shape=out_shape,
        scratch_shapes=[pltpu.VMEM((tm, tn), F32)] if nk > 1 else [],
        compiler_params=_cparams(("parallel", "parallel", "arbitrary")),
        name=name,
    )(*a_list, w, *rows, *bands, *cols)


def _pre_id(x):
    return x


def _post_id(acc):
    return (acc,)


def _post_ln(alpha, acc, h, g, b):
    out = _layer_norm(alpha * h + acc, g, b)
    return (out, out)


PROJ_TM = 1024


def proj(hb, w, out_dtype, name, *, tn=1024):
    (out,) = fused_matmul([hb], w, pre=_pre_id, post=_post_id, out_dtypes=[out_dtype], tm=PROJ_TM, tn=tn, tk=2048,
                          name=name)
    return out


def out_proj_ln(z, w, h, g, b, alpha):
    n = w.shape[1]
    return fused_matmul([z], w, rows=[h], cols=[g, b], pre=_pre_id,
                        post=functools.partial(_post_ln, alpha), out_dtypes=[F32, BF16],
                        tm=512, tn=n, tk=1024, name="out_proj_ln")


def _ffn_body(x_ref, xb_ref, wg_ref, wu_ref, wo_ref, g_ref, b_ref, out_ref, outb_ref, acc_ref, *, nf, alpha):
    f = pl.program_id(1)

    @pl.when(f == 0)
    def _():
        acc_ref[...] = jnp.zeros_like(acc_ref)

    xb = xb_ref[...]
    gate = _dot(xb, wg_ref[...])
    up = _dot(xb, wu_ref[...])
    act = (_silu(gate) * up).astype(BF16)
    acc_ref[...] += _dot(act, wo_ref[...])

    @pl.when(f == nf - 1)
    def _():
        out = _layer_norm(alpha * x_ref[...] + acc_ref[...], g_ref[...], b_ref[...])
        out_ref[...] = out
        outb_ref[...] = out.astype(BF16)


def ffn_ln(x, xb, w_in, w_out, g, b, alpha, *, tm=512, tf=512):
    m, d = x.shape
    dff = w_out.shape[0]
    tm = min(tm, m)
    nf = dff // tf
    assert dff % tf == 0 and m % tm == 0
    row = pl.BlockSpec((tm, d), lambda i, f: (i, 0))
    return pl.pallas_call(
        functools.partial(_ffn_body, nf=nf, alpha=alpha),
        grid=(m // tm, nf),
        in_specs=[
            row, row,
            pl.BlockSpec((d, tf), lambda i, f: (0, f)),
            pl.BlockSpec((d, tf), lambda i, f: (0, nf + f)),
            pl.BlockSpec((tf, d), lambda i, f: (f, 0)),
            pl.BlockSpec((1, d), lambda i, f: (0, 0)),
            pl.BlockSpec((1, d), lambda i, f: (0, 0)),
        ],
        out_specs=[row, row],
        out_shape=[jax.ShapeDtypeStruct((m, d), F32), jax.ShapeDtypeStruct((m, d), BF16)],
        scratch_shapes=[pltpu.VMEM((tm, d), F32)],
        compiler_params=_cparams(("parallel", "arbitrary")),
        name="ffn_ln",
    )(x, xb, w_in, w_in, w_out, g, b)


def _rwkv_proj_body(x_ref, halo_ref, mu_ref, w_ref, out_ref, *, tm, seq):
    i = pl.program_id(0)
    x = x_ref[...]
    halo = jnp.where((i * tm) % seq == 0, 0.0, halo_ref[...])
    xx = _shifted(x, halo, 1) - x
    lhs = (x + xx * mu_ref[...]).astype(BF16)
    out_ref[...] = _dot(lhs, w_ref[...]).astype(out_ref.dtype)


def rwkv_proj(x, mu3, w3, seq, *, tm=512):
    m, d = x.shape
    n = w3.shape[2]
    tm = min(tm, seq)
    per = tm // SUBLANES
    return pl.pallas_call(
        functools.partial(_rwkv_proj_body, tm=tm, seq=seq),
        grid=(m // tm, 3),
        in_specs=[
            pl.BlockSpec((tm, d), lambda i, j: (i, 0)),
            pl.BlockSpec((SUBLANES, d), lambda i, j: (jnp.maximum(i * per - 1, 0), 0)),
            pl.BlockSpec((None, 1, d), lambda i, j: (j, 0, 0)),
            pl.BlockSpec((None, d, n), lambda i, j: (j, 0, 0)),
        ],
        out_specs=pl.BlockSpec((None, tm, n), lambda i, j: (j, i, 0)),
        out_shape=jax.ShapeDtypeStruct((3, m, n), BF16),
        compiler_params=_cparams(("parallel", "arbitrary")),
        name="rwkv_proj",
    )(x, x, mu3, w3)


def _rwkv_lora_body(x_ref, halo_ref, mu_ref, w1_ref, w2_ref, a1_ref, a2_ref, g1_ref, g2_ref, w0_ref, a0_ref,
                    wl_ref, a_ref, g_ref, *, tm, seq):
    i = pl.program_id(0)
    x = x_ref[...]
    halo = jnp.where((i * tm) % seq == 0, 0.0, halo_ref[...])
    xx = _shifted(x, halo, 1) - x
    mu = mu_ref[...]
    xw = (x + xx * mu[0:1, :]).astype(BF16)
    xa = (x + xx * mu[1:2, :]).astype(BF16)
    xg = (x + xx * mu[2:3, :]).astype(BF16)
    zw = w0_ref[...] + _dot(jnp.tanh(_dot(xw, w1_ref[...])).astype(BF16), w2_ref[...])
    wl_ref[...] = _neg_softplus_neg(zw) - 0.5
    a_ref[...] = jax.nn.sigmoid(a0_ref[...] + _dot(_dot(xa, a1_ref[...]).astype(BF16), a2_ref[...]))
    g_ref[...] = _dot(jax.nn.sigmoid(_dot(xg, g1_ref[...])).astype(BF16), g2_ref[...]).astype(g_ref.dtype)


def rwkv_lora(x, mu3, w1, w2, a1, a2, g1, g2, w0, a0, seq, *, tm=256):
    m, d = x.shape
    tm = min(tm, seq)
    per = tm // SUBLANES
    full = lambda arr: pl.BlockSpec(arr.shape, lambda i: (0,) * arr.ndim)
    row = pl.BlockSpec((tm, d), lambda i: (i, 0))
    return pl.pallas_call(
        functools.partial(_rwkv_lora_body, tm=tm, seq=seq),
        grid=(m // tm,),
        in_specs=[row, pl.BlockSpec((SUBLANES, d), lambda i: (jnp.maximum(i * per - 1, 0), 0)),
                  full(mu3), full(w1), full(w2), full(a1), full(a2), full(g1), full(g2), full(w0), full(a0)],
        out_specs=[row, row, row],
        out_shape=[jax.ShapeDtypeStruct((m, d), F32), jax.ShapeDtypeStruct((m, d), F32),
                   jax.ShapeDtypeStruct((m, d), BF16)],
        compiler_params=_cparams(("parallel",)),
        name="rwkv_lora",
    )(x, x, mu3, w1, w2, a1, a2, g1, g2, w0, a0)


RW_PAIRS_PER_STEP = 16
RW_LEVELS = 6


def _rwkv_masks():
    n = 2 * CHUNK
    t = np.arange(n)[:, None]
    s = np.arange(n)[None, :]
    strict = (t > s).astype(np.float32)
    incl = (t >= s).astype(np.float32)
    bones = ((t // RW_HEAD) == (s // RW_HEAD)).astype(np.float32)
    tri = np.stack([strict, incl, bones])
    lvl = []
    for lv in range(RW_LEVELS):
        sz = 1 << lv
        lvl.append(((t // (2 * sz) == s // (2 * sz)) & ((t // sz) % 2 == 1) & ((s // sz) % 2 == 0)).astype(np.float32))
    t64 = incl[:CHUNK, :CHUNK]
    return jnp.asarray(tri), jnp.asarray(np.stack(lvl)), jnp.asarray(t64)


def _rwkv_core_body(r_ref, k_ref, v_ref, wl_ref, a_ref, g_ref, kk_ref, ka_ref, rk_ref, lng_ref, lnb_ref,
                    tri_ref, lvl_ref, t64_ref, out_ref, st_ref):
    c = pl.program_id(2)

    @pl.when(c == 0)
    def _():
        st_ref[...] = jnp.zeros_like(st_ref)

    strict, incl = tri_ref[0], tri_ref[1]
    eye = incl - strict
    t64 = t64_ref[...].astype(BF16)
    lo = lax.broadcasted_iota(jnp.int32, (1, LANES), 1) < RW_HEAD
    inv_head = 1.0 / RW_HEAD

    def stack(x):
        return jnp.concatenate([jnp.where(lo, x, 0.0), jnp.where(lo, 0.0, x)], axis=0).astype(BF16)

    pairs = range(RW_PAIRS_PER_STEP)
    sls = [slice(q * LANES, (q + 1) * LANES) for q in pairs]

    ld = lambda ref, sl: ref[:, sl].astype(F32)
    kk_sq = []
    for sl in sls:
        kk = ld(k_ref, sl) * kk_ref[:, sl]
        kk_sq.append(_half_sums(lo, kk * kk))
    cums = [_dot_01(t64, -jnp.exp(wl_ref[:, sl])) for sl in sls]

    ops = []
    for sl, ss, cum in zip(sls, kk_sq, cums):
        k, a = ld(k_ref, sl), a_ref[:, sl]
        kk = k * kk_ref[:, sl] / jnp.maximum(jnp.sqrt(ss), 1e-12)
        k2 = k * (1.0 + (a - 1.0) * ka_ref[:, sl])
        lw = -jnp.exp(wl_ref[:, sl])
        cl = cum[CHUNK - 1:CHUNK, :]
        e_neg = jnp.exp(-cum)
        e_end = jnp.exp(cl - cum)
        kb = kk * a
        ops.append(dict(
            a_s=stack(-kk * jnp.exp(cum - lw)),
            r_s=stack(ld(r_ref, sl) * jnp.exp(cum)),
            b_s=stack(kb * e_neg), k_s=stack(k2 * e_neg), v_s=stack(ld(v_ref, sl)),
            bh_s=stack(kb * e_end), kh_s=stack(k2 * e_end),
            w_end=jnp.exp(cl), k2=k2))

    a_ab = [strict * _dot_nt(o["a_s"], o["b_s"]) for o in ops]
    a_ak = [(strict * _dot_nt(o["a_s"], o["k_s"])).astype(BF16) for o in ops]
    r_b = [(incl * _dot_nt(o["r_s"], o["b_s"])).astype(BF16) for o in ops]
    r_k = [(incl * _dot_nt(o["r_s"], o["k_s"])).astype(BF16) for o in ops]

    t_inv = [eye + m * lvl_ref[0] for m in a_ab]
    for lv in range(1, RW_LEVELS):
        tb = [t.astype(BF16) for t in t_inv]
        xs = [_dot(b, (m * lvl_ref[lv]).astype(BF16)).astype(BF16) for b, m in zip(tb, a_ab)]
        t_inv = [t + _dot(x, b) for t, x, b in zip(t_inv, xs, tb)]

    s_old = [st_ref[q] for q in pairs]
    s_b = [s.astype(BF16) for s in s_old]
    rhs = [(_dot_nt(o["a_s"], s) + _dot(m, o["v_s"])).astype(BF16) for o, s, m in zip(ops, s_b, a_ak)]
    u_b = [_dot(t.astype(BF16), x).astype(BF16) for t, x in zip(t_inv, rhs)]
    y_s = [_dot_nt(o["r_s"], s) + _dot(mb, u) + _dot(mk, o["v_s"])
           for o, s, mb, u, mk in zip(ops, s_b, r_b, u_b, r_k)]
    for q, (o, s, u) in enumerate(zip(ops, s_old, u_b)):
        st_ref[q] = s * o["w_end"] + _dot_tn(u, o["bh_s"]) + _dot_tn(o["v_s"], o["kh_s"])

    ys = [y[:CHUNK] + y[CHUNK:] for y in y_s]
    mus = [_half_sums(lo, y) * inv_head for y in ys]
    ds = [y - mu for y, mu in zip(ys, mus)]
    var = [_half_sums(lo, d * d) * inv_head for d in ds]
    bonus = [_half_sums(lo, ld(r_ref, sl) * o["k2"] * rk_ref[:, sl]) for sl, o in zip(sls, ops)]
    for sl, d, vr, bn in zip(sls, ds, var, bonus):
        yn = d * lax.rsqrt(vr + RW_GN_EPS) * lng_ref[:, sl] + lnb_ref[:, sl]
        out_ref[:, sl] = ((yn + bn * ld(v_ref, sl)) * ld(g_ref, sl)).astype(out_ref.dtype)


def rwkv_core(rkv, wl, a, g, k_k, k_a, r_k, ln_g, ln_b, bsz, seq):
    _, m, d = rkv.shape
    width = RW_PAIRS_PER_STEP * LANES
    nc = seq // CHUNK
    tri, lvl, t64 = _rwkv_masks()
    row = lambda b, p, c: (b * nc + c, p)
    rkv_spec = lambda j: pl.BlockSpec((None, CHUNK, width), lambda b, p, c: (j, b * nc + c, p))
    act = pl.BlockSpec((CHUNK, width), row)
    par = pl.BlockSpec((1, width), lambda b, p, c: (0, p))
    full = lambda arr: pl.BlockSpec(arr.shape, lambda b, p, c: (0,) * arr.ndim)
    return pl.pallas_call(
        _rwkv_core_body,
        grid=(bsz, d // width, nc),
        in_specs=[rkv_spec(0), rkv_spec(1), rkv_spec(2), act, act, act, par, par, par, par, par,
                  full(tri), full(lvl), full(t64)],
        out_specs=act,
        out_shape=jax.ShapeDtypeStruct((m, d), BF16),
        scratch_shapes=[pltpu.VMEM((RW_PAIRS_PER_STEP, LANES, LANES), F32)],
        compiler_params=_cparams(("parallel", "parallel", "arbitrary")),
        name="rwkv_core",
    )(rkv, rkv, rkv, wl, a, g, k_k, k_a, r_k, ln_g, ln_b, tri, lvl, t64)


def _pad_to(x, axis, size):
    pad = [(0, 0)] * x.ndim
    pad[axis] = (0, size - x.shape[axis])
    return jnp.pad(x, pad)


def rwkv_layer(h, bsz, seq, mu, w_in, w0, w1, w2, a0, a1, a2, g1, g2, k_k, k_a, r_k, gn_g, gn_b, w_out,
               ln_g, ln_b, alpha):
    d = h.shape[1]
    v2 = lambda t: t.reshape(1, d)
    mu_rkv = jnp.stack([mu[0], mu[2], mu[3]]).reshape(3, 1, d)
    mu_lora = jnp.stack([mu[1], mu[4], mu[5]])
    rkv = rwkv_proj(h, mu_rkv, w_in.astype(BF16), seq)
    wl, a, g = rwkv_lora(
        h, mu_lora,
        _pad_to(w1, 1, LANES).astype(BF16), _pad_to(w2, 0, LANES).astype(BF16),
        _pad_to(a1, 1, LANES).astype(BF16), _pad_to(a2, 0, LANES).astype(BF16),
        g1.astype(BF16), g2.astype(BF16), v2(w0), v2(a0), seq)
    z = rwkv_core(rkv, wl, a, g, v2(k_k), v2(k_a), v2(r_k), v2(gn_g), v2(gn_b), bsz, seq)
    return out_proj_ln(z, w_out.astype(BF16), h, ln_g, ln_b, alpha)


def _ssd_consts(heads):
    lanes = heads * SSD_HEAD
    head_of = np.arange(lanes) // SSD_HEAD
    pos_of = np.arange(lanes) % SSD_HEAD
    expand = (np.arange(LANES)[:, None] == head_of[None, :]).astype(np.float32)
    eye = (np.arange(CHUNK)[:, None] == pos_of[None, :]).astype(np.float32)
    tril = (np.arange(CHUNK)[:, None] >= pos_of[None, :]).astype(np.float32)
    return jnp.asarray(expand, BF16), jnp.asarray(eye), jnp.asarray(tril)


def _expand_heads(x, e01):
    hi = x.astype(BF16)
    low = (x - hi.astype(F32)).astype(BF16)
    return _dot(hi, e01) + _dot(low, e01)


def _ssd_body(xbc_ref, halo_ref, z_ref, dtp_ref, cw_ref, cb_ref, dtb_ref, alog_ref, dsk_ref, ng_ref, t64_ref,
              e01_ref, eye_ref, tril_ref, out_ref, st_ref, xc_ref, dt_ref, da_ref, *, tq, d_inner):
    s = pl.program_id(1)

    @pl.when(s == 0)
    def _():
        st_ref[...] = jnp.zeros_like(st_ref)

    halo = jnp.where(s == 0, 0.0, halo_ref[...].astype(F32)[BF16_ROWS - SUBLANES:, :])
    xc_ref[...] = _silu(_causal_conv4(xbc_ref[...].astype(F32), halo, cw_ref[...], cb_ref[...]))
    dt_all = _softplus(dtp_ref[...] + dtb_ref[...])
    dt_ref[...] = dt_all
    da_ref[...] = dt_all * (-jnp.exp(alog_ref[...]))

    t64b = t64_ref[...].astype(BF16)
    lo = lax.broadcasted_iota(jnp.int32, (1, LANES), 1) < SSD_HEAD
    gn = SSD_GROUPS * SSD_STATE
    gw = d_inner // SSD_GROUPS
    pairs_per_group = gw // LANES

    def chunk(ci, carry):
        r0 = pl.multiple_of(ci * CHUNK, CHUNK)
        rows = pl.ds(r0, CHUNK)
        e01 = e01_ref[...]
        cs = _dot_01(t64b, da_ref[rows, :])
        dte = _expand_heads(dt_ref[rows, :], e01)
        cse = _expand_heads(cs, e01)
        cs_row = jnp.sum(eye_ref[...] * cse, axis=0, keepdims=True)
        decay = jnp.exp(jnp.minimum(cse - cs_row, 0.0)) * tril_ref[...]
        cs_last = cse[CHUNK - 1:CHUNK, :]
        ecs = jnp.exp(cse)
        to_end = jnp.exp(cs_last - cse)
        e_last = jnp.exp(cs_last)
        for g in range(SSD_GROUPS):
            gsl = slice(g * gw, (g + 1) * gw)
            bg = xc_ref[rows, d_inner + g * SSD_STATE:d_inner + (g + 1) * SSD_STATE]
            cg = xc_ref[rows, d_inner + gn + g * SSD_STATE:d_inner + gn + (g + 1) * SSD_STATE].astype(BF16)
            bgb = bg.astype(BF16)
            cb_rep = _dot_nt(cg, jnp.concatenate([bgb] * (gw // SSD_HEAD), axis=0))
            mat = (cb_rep * decay[:, gsl]).astype(BF16)
            bg_t = bg.T.astype(BF16)
            ps = [g * pairs_per_group + q for q in range(pairs_per_group)]
            sls = [slice(p * LANES, (p + 1) * LANES) for p in ps]
            xs = [xc_ref[rows, sl] for sl in sls]
            xdts = [x * dte[:, sl] for x, sl in zip(xs, sls)]
            x_st = [jnp.concatenate([jnp.where(lo, xdt, 0.0), jnp.where(lo, 0.0, xdt)], axis=0).astype(BF16)
                    for xdt in xdts]
            x_end = [(xdt * to_end[:, sl]).astype(BF16) for xdt, sl in zip(xdts, sls)]
            sts = [st_ref[p] for p in ps]
            intra = [_dot(mat[:, q * LANES:(q + 1) * LANES], xs_) for q, xs_ in enumerate(x_st)]
            inter = [_dot(cg, st.astype(BF16)) for st in sts]
            upd = [_dot(bg_t, xe) for xe in x_end]
            for p, sl, st, u in zip(ps, sls, sts, upd):
                st_ref[p] = st * e_last[:, sl] + u
            ys = []
            ssq = jnp.zeros((CHUNK, 1), F32)
            for sl, x, ya, yb in zip(sls, xs, intra, inter):
                y = ya + yb * ecs[:, sl] + x * dsk_ref[:, sl]
                y = y * _silu(z_ref[rows, sl].astype(F32))
                ssq = ssq + jnp.sum(y * y, axis=-1, keepdims=True)
                ys.append(y)
            inv = lax.rsqrt(ssq * (1.0 / gw) + 1e-5)
            for sl, y in zip(sls, ys):
                out_ref[rows, sl] = (y * inv * ng_ref[:, sl]).astype(out_ref.dtype)
        return carry

    lax.fori_loop(0, tq // CHUNK, chunk, 0)


def ssd_core(xbc_pre, z, dt_pre, conv_w, conv_b, dt_bias, a_log, d_skip, norm_g, bsz, seq, *, tq=256):
    n, conv_dim = xbc_pre.shape
    d_inner = z.shape[1]
    tq = min(tq, seq)
    nblk = seq // tq
    per = tq // BF16_ROWS
    npairs = d_inner // LANES
    t64 = _rwkv_masks()[2]
    e01, eye, tril = _ssd_consts(d_inner // SSD_HEAD)
    row = lambda b, s: (b * nblk + s, 0)
    full = lambda arr: pl.BlockSpec(arr.shape, lambda b, s: (0,) * arr.ndim)
    return pl.pallas_call(
        functools.partial(_ssd_body, tq=tq, d_inner=d_inner),
        grid=(bsz, nblk),
        in_specs=[
            pl.BlockSpec((tq, conv_dim), row),
            pl.BlockSpec((BF16_ROWS, conv_dim), lambda b, s: (jnp.maximum((b * nblk + s) * per - 1, 0), 0)),
            pl.BlockSpec((tq, d_inner), row),
            pl.BlockSpec((tq, LANES), row),
            full(conv_w), full(conv_b), full(dt_bias), full(a_log), full(d_skip), full(norm_g), full(t64),
            full(e01), full(eye), full(tril),
        ],
        out_specs=pl.BlockSpec((tq, d_inner), row),
        out_shape=jax.ShapeDtypeStruct((n, d_inner), BF16),
        scratch_shapes=[pltpu.VMEM((npairs, SSD_STATE, LANES), F32), pltpu.VMEM((tq, conv_dim), F32),
                        pltpu.VMEM((tq, LANES), F32), pltpu.VMEM((tq, LANES), F32)],
        compiler_params=_cparams(("arbitrary", "arbitrary")),
        name="ssd_core",
    )(xbc_pre, xbc_pre, z, dt_pre, conv_w, conv_b, dt_bias, a_log, d_skip, norm_g, t64, e01, eye, tril)


def ssd_layer(h, hb, bsz, seq, w_in, conv_w, conv_b, dt_bias, a_log, d_skip, norm_g, w_out, ln_g, ln_b, alpha):
    d_inner = w_out.shape[0]
    conv_dim = conv_w.shape[1]
    w_in = w_in.astype(BF16)
    z = proj(hb, w_in[:, :d_inner], BF16, "ssd_proj_z")
    xbc_pre = proj(hb, w_in[:, d_inner:d_inner + conv_dim], BF16, "ssd_proj_xbc")
    dt_pre = proj(hb, _pad_to(w_in[:, d_inner + conv_dim:], 1, LANES), F32, "ssd_proj_dt")
    pad_row = lambda t: _pad_to(t.reshape(1, -1), 1, LANES)
    d_skip_lanes = jnp.repeat(d_skip, SSD_HEAD).reshape(1, -1)
    y = ssd_core(xbc_pre, z, dt_pre, conv_w, conv_b.reshape(1, -1), pad_row(dt_bias), pad_row(a_log),
                 d_skip_lanes, norm_g.reshape(1, -1), bsz, seq)
    return out_proj_ln(y, w_out.astype(BF16), h, ln_g, ln_b, alpha)


ROPE_HALF = 16
ATTN_SUB_ROWS = 256


def _rope_table_body(pos_ref, freq_ref, cos_ref, sin_ref):
    ang = pos_ref[...].astype(F32) * freq_ref[...]
    lane = lax.broadcasted_iota(jnp.int32, (1, LANES), 1)
    sn = jnp.sin(ang)
    cos_ref[...] = jnp.cos(ang)
    sin_ref[...] = jnp.where(lane < ROPE_HALF, -sn, jnp.where(lane < 2 * ROPE_HALF, sn, 0.0))


def rope_tables(pos, freq, *, tm=1024):
    n = pos.shape[0]
    tm = min(tm, n)
    band = pl.BlockSpec((tm, LANES), lambda i: (i, 0))
    return pl.pallas_call(
        _rope_table_body,
        grid=(n // tm,),
        in_specs=[pl.BlockSpec((tm, 1), lambda i: (i, 0)), pl.BlockSpec((1, LANES), lambda i: (0, 0))],
        out_specs=[band, band],
        out_shape=[jax.ShapeDtypeStruct((n, LANES), F32)] * 2,
        compiler_params=_cparams(("parallel",)),
        name="rope_tables",
    )(pos, freq)


def _post_rope(scale, acc, cos, sin):
    lane = lax.broadcasted_iota(jnp.int32, (1, LANES), 1)
    outs = []
    for j in range(acc.shape[1] // LANES):
        t = acc[:, j * LANES:(j + 1) * LANES]
        partner = jnp.where(lane < ROPE_HALF, pltpu.roll(t, LANES - ROPE_HALF, axis=1), pltpu.roll(t, ROPE_HALF, axis=1))
        outs.append((t * cos + partner * sin) * scale)
    return (jnp.concatenate(outs, axis=1),)


def _attn_body(qi_ref, ki_ref, last_ref, q_ref, k_ref, v_ref, lq1_ref, lk1_ref, lq2_ref, lk2_ref, sg_ref,
               out_ref, m_ref, l_ref, acc_ref, *, tq, tk, hd):
    p = pl.program_id(2)
    qi, ki = qi_ref[p], ki_ref[p]

    @pl.when(ki == 0)
    def _():
        m_ref[...] = jnp.full_like(m_ref, NEG_BIG)
        l_ref[...] = jnp.zeros_like(l_ref)
        acc_ref[...] = jnp.zeros_like(acc_ref)

    sub = min(ATTN_SUB_ROWS, tq)
    n_sub = tq // sub

    def qk(r):
        rows = slice(r * sub, (r + 1) * sub)
        return [_dot_nt(q_ref[rows, j * hd:(j + 1) * hd], k_ref[:, j * hd:(j + 1) * hd]) for j in range(2)]

    def update(masked):
        v = v_ref[...]
        if masked:
            k_chunk = jnp.right_shift(ki * tk + lax.broadcasted_iota(jnp.int32, (1, tk), 1), CHUNK_SHIFT)
        scores = qk(0)
        for r in range(n_sub):
            nxt = qk(r + 1) if r + 1 < n_sub else None
            rows = slice(r * sub, (r + 1) * sub)
            if masked:
                row0 = qi * tq + r * sub
                q_chunk = jnp.right_shift(row0 + lax.broadcasted_iota(jnp.int32, (sub, 1), 0), CHUNK_SHIFT)
                visible = k_chunk <= q_chunk
                scores = [jnp.where(visible, s, NEG_BIG) for s in scores]
            for j in range(2):
                m_old = m_ref[j, rows, :]
                m_new = jnp.maximum(m_old, jnp.max(scores[j], axis=-1, keepdims=True))
                pe = jnp.exp(scores[j] - m_new)
                scale = jnp.exp(m_old - m_new)
                l_ref[j, rows, :] = scale * l_ref[j, rows, :] + jnp.sum(pe, axis=-1, keepdims=True)
                acc_ref[j, rows, :] = scale * acc_ref[j, rows, :] + _dot(pe.astype(BF16), v)
                m_ref[j, rows, :] = m_new
            scores = nxt

    needs_mask = (ki + 1) * tk > qi * tq + CHUNK

    @pl.when(needs_mask)
    def _():
        update(True)

    @pl.when(jnp.logical_not(needs_mask))
    def _():
        update(False)

    @pl.when(last_ref[p] == 1)
    def _():
        lam = (jnp.exp(jnp.sum(lq1_ref[...] * lk1_ref[...], axis=-1, keepdims=True))
               - jnp.exp(jnp.sum(lq2_ref[...] * lk2_ref[...], axis=-1, keepdims=True)) + DIF_LAMBDA_INIT)
        o = acc_ref[0] / l_ref[0] - lam * (acc_ref[1] / l_ref[1])
        o = o * lax.rsqrt(jnp.mean(o * o, axis=-1, keepdims=True) + 1e-5) * sg_ref[...]
        out_ref[...] = (o * (1.0 - DIF_LAMBDA_INIT)).astype(out_ref.dtype)


def diff_attention(q, k, v, lq1, lk1, lq2, lk2, subln_g, bsz, seq, *, tq=1024, tk=1024):
    n, d = q.shape
    hw = d // DIF_HEADS
    hd = hw // 2
    tq, tk = min(tq, seq), min(tk, seq)
    assert tq % CHUNK == 0 and tk % CHUNK == 0 and CHUNK == 1 << CHUNK_SHIFT
    nq, nk = seq // tq, seq // tk
    n_kv = lambda a: -(-((a + 1) * tq) // tk)
    pairs = [(a, b) for a in range(nq) for b in range(n_kv(a))]
    qi = jnp.asarray([a for a, _ in pairs], jnp.int32)
    ki = jnp.asarray([b for _, b in pairs], jnp.int32)
    last = jnp.asarray([1 if b == n_kv(a) - 1 else 0 for a, b in pairs], jnp.int32)
    vec = pl.BlockSpec((1, hd), lambda b, h, p, qi, ki, last: (0, 0))
    grid_spec = pltpu.PrefetchScalarGridSpec(
        num_scalar_prefetch=3,
        grid=(bsz, DIF_HEADS, len(pairs)),
        in_specs=[
            pl.BlockSpec((tq, hw), lambda b, h, p, qi, ki, last: (b * nq + qi[p], h)),
            pl.BlockSpec((tk, hw), lambda b, h, p, qi, ki, last: (b * nk + ki[p], h)),
            pl.BlockSpec((tk, hw), lambda b, h, p, qi, ki, last: (b * nk + ki[p], h)),
            vec, vec, vec, vec,
            pl.BlockSpec((1, hw), lambda b, h, p, qi, ki, last: (0, 0)),
        ],
        out_specs=pl.BlockSpec((tq, hw), lambda b, h, p, qi, ki, last: (b * nq + qi[p], h)),
        scratch_shapes=[pltpu.VMEM((2, tq, 1), F32), pltpu.VMEM((2, tq, 1), F32), pltpu.VMEM((2, tq, hw), F32)],
    )
    return pl.pallas_call(
        functools.partial(_attn_body, tq=tq, tk=tk, hd=hd),
        grid_spec=grid_spec,
        out_shape=jax.ShapeDtypeStruct((n, d), BF16),
        compiler_params=_cparams(("parallel", "parallel", "arbitrary")),
        name="diff_attention",
    )(qi, ki, last, q, k, v, lq1, lk1, lq2, lk2, subln_g)


def diff_layer(h, hb, positions, bsz, seq, w_in, lq1, lk1, lq2, lk2, subln_g, w_out, ln_g, ln_b, alpha):
    d = h.shape[1]
    hd = d // DIF_HEADS // 2
    rope = hd // 4
    assert hd == LANES and rope == 2 * ROPE_HALF
    inv_freq = ROPE_THETA ** (-jnp.arange(0, rope, 2, dtype=F32) / rope)
    freq = jnp.concatenate([inv_freq, inv_freq, jnp.zeros((LANES - rope,), F32)]).reshape(1, LANES)
    cos, sin = rope_tables(positions.reshape(bsz * seq, 1), freq)
    w_in = w_in.astype(BF16)
    rope_mm = lambda w, scale, name: fused_matmul([hb], w, bands=[cos, sin], pre=_pre_id,
                                                  post=functools.partial(_post_rope, scale), out_dtypes=[BF16],
                                                  tm=PROJ_TM, tn=1024, tk=2048, name=name)[0]
    q = rope_mm(w_in[:, :d], hd ** -0.5, "dif_proj_q")
    k = rope_mm(w_in[:, d:2 * d], 1.0, "dif_proj_k")
    v = proj(hb, w_in[:, 2 * d:], BF16, "dif_proj_v")
    v2 = lambda t: t.reshape(1, -1)
    o = diff_attention(q, k, v, v2(lq1), v2(lk1), v2(lq2), v2(lk2), v2(subln_g), bsz, seq)
    return out_proj_ln(o, w_out.astype(BF16), h, ln_g, ln_b, alpha)


def _post_gelu(acc):
    c = math.sqrt(2.0 / math.pi)
    return (0.5 * acc * (1.0 + jnp.tanh(c * (acc + 0.044715 * acc * acc * acc))),)


def _lru_body(u_ref, halo_ref, gate_ref, cw_ref, cb_ref, wa_ref, ba_ref, wx_ref, bx_ref, lam_ref,
              out_ref, h_ref, a_ref, b_ref, *, tq, width):
    s = pl.program_id(1)

    @pl.when(s == 0)
    def _():
        h_ref[...] = jnp.zeros_like(h_ref)

    halo = jnp.where(s == 0, 0.0, halo_ref[...])
    u = _causal_conv4_ref(u_ref, halo, cw_ref[...], cb_ref[...])

    nsp = _softplus(-lam_ref[...])
    blk = width // LRU_BLOCKS
    for j in range(LRU_BLOCKS):
        sl = slice(j * blk, (j + 1) * blk)
        uj = u[:, sl]
        ub = uj.astype(BF16)
        r = jax.nn.sigmoid(_dot(ub, wa_ref[j]) + ba_ref[:, sl])
        i = jax.nn.sigmoid(_dot(ub, wx_ref[j]) + bx_ref[:, sl])
        log_a = -LRU_C * r * nsp[:, sl]
        a_ref[:, sl] = jnp.exp(log_a)
        b_ref[:, sl] = jnp.sqrt(1.0 - jnp.exp(2.0 * log_a)) * (i * uj)

    def step(t, h):
        h = a_ref[pl.ds(t, 1), :] * h + b_ref[pl.ds(t, 1), :]
        b_ref[pl.ds(t, 1), :] = h
        return h

    h_ref[0:1, :] = lax.fori_loop(0, tq, step, h_ref[0:1, :], unroll=8)
    out_ref[...] = (gate_ref[...].astype(F32) * b_ref[...]).astype(out_ref.dtype)


def lru_core(u_pre, gate, conv_w, conv_b, w_a, b_a, w_x, b_x, lam, bsz, seq, *, tq=512):
    n, width = u_pre.shape
    tq = min(tq, seq)
    nblk = seq // tq
    per = tq // SUBLANES
    row = lambda b, s: (b * nblk + s, 0)
    full = lambda arr: pl.BlockSpec(arr.shape, lambda b, s: (0,) * arr.ndim)
    return pl.pallas_call(
        functools.partial(_lru_body, tq=tq, width=width),
        grid=(bsz, nblk),
        in_specs=[
            pl.BlockSpec((tq, width), row),
            pl.BlockSpec((SUBLANES, width), lambda b, s: (jnp.maximum((b * nblk + s) * per - 1, 0), 0)),
            pl.BlockSpec((tq, width), row),
            full(conv_w), full(conv_b), full(w_a), full(b_a), full(w_x), full(b_x), full(lam),
        ],
        out_specs=pl.BlockSpec((tq, width), row),
        out_shape=jax.ShapeDtypeStruct((n, width), BF16),
        scratch_shapes=[pltpu.VMEM((SUBLANES, width), F32), pltpu.VMEM((tq, width), F32),
                        pltpu.VMEM((tq, width), F32)],
        compiler_params=_cparams(("arbitrary", "arbitrary")),
        name="lru_core",
    )(u_pre, u_pre, gate, conv_w, conv_b, w_a, b_a, w_x, b_x, lam)


def rglru_layer(h, hb, bsz, seq, w_in, conv_w, conv_b, w_a, b_a, w_x, b_x, lam, w_out, ln_g, ln_b, alpha):
    width = w_in.shape[1] // 2
    w_in = w_in.astype(BF16)
    (gate,) = fused_matmul([hb], w_in[:, :width], pre=_pre_id, post=_post_gelu, out_dtypes=[BF16],
                           tm=PROJ_TM, tn=1024, tk=2048, name="lru_proj_gate")
    u_pre = proj(hb, w_in[:, width:], F32, "lru_proj_u")
    z = lru_core(u_pre, gate, conv_w, conv_b.reshape(1, -1), w_a.astype(BF16), b_a.reshape(1, -1),
                 w_x.astype(BF16), b_x.reshape(1, -1), lam.reshape(1, -1), bsz, seq)
    return out_proj_ln(z, w_out.astype(BF16), h, ln_g, ln_b, alpha)


def kernel(x, positions, rwkv_mu, rwkv_w_in, rwkv_w0, rwkv_w1, rwkv_w2, rwkv_a0, rwkv_a1, rwkv_a2, rwkv_g1, rwkv_g2, rwkv_k_k, rwkv_k_a, rwkv_r_k, rwkv_ln_g, rwkv_ln_b, rwkv_w_out, ssd_w_in, ssd_conv_w, ssd_conv_b, ssd_dt_bias, ssd_a_log, ssd_d, ssd_norm_g, ssd_w_out, dif_w_in, dif_lq1, dif_lk1, dif_lq2, dif_lk2, dif_subln_g, dif_w_out, lru_w_in, lru_conv_w, lru_conv_b, lru_w_a, lru_b_a, lru_w_x, lru_b_x, lru_lam, lru_w_out, ffn_w_in, ffn_w_out, ln_g, ln_b):
    bsz, seq, d = x.shape
    depth = ffn_w_in.shape[0]
    alpha = (2 * depth) ** 0.25
    h = x.reshape(bsz * seq, d)
    hb = None
    for i in range(depth):
        m = i % 4
        g0, b0 = ln_g[i, 0].reshape(1, d), ln_b[i, 0].reshape(1, d)
        g1, b1 = ln_g[i, 1].reshape(1, d), ln_b[i, 1].reshape(1, d)
        if m == 0:
            h, hb = rwkv_layer(h, bsz, seq, rwkv_mu, rwkv_w_in, rwkv_w0, rwkv_w1, rwkv_w2, rwkv_a0, rwkv_a1,
                               rwkv_a2, rwkv_g1, rwkv_g2, rwkv_k_k, rwkv_k_a, rwkv_r_k, rwkv_ln_g, rwkv_ln_b,
                               rwkv_w_out, g0, b0, alpha)
        elif m == 1:
            h, hb = ssd_layer(h, hb, bsz, seq, ssd_w_in, ssd_conv_w, ssd_conv_b, ssd_dt_bias, ssd_a_log, ssd_d,
                              ssd_norm_g, ssd_w_out, g0, b0, alpha)
        elif m == 2:
            h, hb = diff_layer(h, hb, positions, bsz, seq, dif_w_in, dif_lq1, dif_lk1, dif_lq2, dif_lk2,
                               dif_subln_g, dif_w_out, g0, b0, alpha)
        else:
            h, hb = rglru_layer(h, hb, bsz, seq, lru_w_in, lru_conv_w, lru_conv_b, lru_w_a, lru_b_a, lru_w_x,
                                lru_b_x, lru_lam, lru_w_out, g0, b0, alpha)
        h, hb = ffn_ln(h, hb, ffn_w_in[i].astype(BF16), ffn_w_out[i].astype(BF16), g1, b1, alpha)
    return h.reshape(bsz, seq, d)
```

```python
import functools
import math

import numpy as np
import jax
import jax.numpy as jnp
from jax import lax
from jax.experimental import pallas as pl
from jax.experimental.pallas import tpu as pltpu

F32 = jnp.float32
BF16 = jnp.bfloat16
HI = lax.Precision.HIGHEST

LANES = 128
SUBLANES = 8
BF16_ROWS = 16
VMEM_LIMIT_BYTES = 56 * 1024 * 1024

LN_EPS = 1e-5
CHUNK = 64
CHUNK_SHIFT = 6
RW_HEAD = 64
RW_GN_EPS = 64e-5
SSD_HEAD = 64
SSD_GROUPS = 8
SSD_STATE = 128
DIF_HEADS = 8
ROPE_THETA = 500000.0
DIF_LAYER = 2
DIF_LAMBDA_INIT = 0.8 - 0.6 * math.exp(-0.3 * DIF_LAYER)
LRU_BLOCKS = 16
LRU_C = 8.0
NEG_BIG = -1e30


def _cparams(sem):
    return pltpu.CompilerParams(dimension_semantics=sem, vmem_limit_bytes=VMEM_LIMIT_BYTES)


def _dot(a, b):
    return jnp.dot(a, b, preferred_element_type=F32)


def _dot_nt(a, b):
    return lax.dot_general(a, b, (((1,), (1,)), ((), ())), preferred_element_type=F32)


def _dot_tn(a, b):
    return lax.dot_general(a, b, (((0,), (0,)), ((), ())), preferred_element_type=F32)


def _dot_hi(a, b):
    return jnp.dot(a, b, preferred_element_type=F32, precision=HI)


def _dot_01(m01, x):
    hi = x.astype(BF16)
    r1 = x - hi.astype(F32)
    mid = r1.astype(BF16)
    low = (r1 - mid.astype(F32)).astype(BF16)
    return _dot(m01, hi) + _dot(m01, mid) + _dot(m01, low)


def _half_sums(lo_mask, x):
    s_lo = jnp.sum(jnp.where(lo_mask, x, 0.0), axis=-1, keepdims=True)
    s_hi = jnp.sum(jnp.where(lo_mask, 0.0, x), axis=-1, keepdims=True)
    return jnp.where(lo_mask, s_lo, s_hi)


def _layer_norm(t, g, b):
    mu = jnp.mean(t, -1, keepdims=True)
    d = t - mu
    var = jnp.mean(d * d, -1, keepdims=True)
    return d * lax.rsqrt(var + LN_EPS) * g + b


def _neg_softplus_neg(z):
    return jnp.minimum(z, 0.0) - jnp.log(1.0 + jnp.exp(-jnp.abs(z)))


def _softplus(z):
    return jnp.maximum(z, 0.0) + jnp.log(1.0 + jnp.exp(-jnp.abs(z)))


def _silu(z):
    return z * jax.nn.sigmoid(z)


def _shifted(x, halo, d):
    head = pltpu.roll(jnp.concatenate([halo, x[:SUBLANES]], axis=0), d, axis=0)[SUBLANES:]
    if x.shape[0] == SUBLANES:
        return head
    return jnp.concatenate([head, pltpu.roll(x, d, axis=0)[SUBLANES:]], axis=0)


def _causal_conv4(x, halo, cw, cb):
    out = x * cw[3:4, :] + cb
    for d in (1, 2, 3):
        out = out + _shifted(x, halo, d) * cw[3 - d:4 - d, :]
    return out


def _causal_conv4_ref(x_ref, halo, cw, cb):
    n = x_ref.shape[0]
    head = _causal_conv4(x_ref[0:SUBLANES, :], halo, cw, cb)
    body = x_ref[SUBLANES:n, :] * cw[3:4, :] + cb
    for d in (1, 2, 3):
        body = body + x_ref[SUBLANES - d:n - d, :] * cw[3 - d:4 - d, :]
    return jnp.concatenate([head, body], axis=0)


def _mm_body(*refs, n_a, n_extra, n_out, pre, post, nk):
    a = refs[:n_a]
    w = refs[n_a]
    extra = refs[n_a + 1:n_a + 1 + n_extra]
    outs = refs[n_a + 1 + n_extra:n_a + 1 + n_extra + n_out]

    def finish(total):
        res = post(total, *[r[...] for r in extra])
        for o, v in zip(outs, res):
            o[...] = v.astype(o.dtype)

    lhs = pre(*[r[...] for r in a])
    if nk == 1:
        finish(_dot(lhs, w[...]))
    else:
        acc = refs[-1]
        k = pl.program_id(2)

        @pl.when(k == 0)
        def _():
            acc[...] = jnp.zeros_like(acc)

        acc[...] += _dot(lhs, w[...])

        @pl.when(k == nk - 1)
        def _():
            finish(acc[...])


def fused_matmul(a_list, w, *, rows=(), bands=(), cols=(), pre, post, out_dtypes, tm, tn, tk, name):
    m, kdim = a_list[0].shape
    n = w.shape[1]
    tm, tn, tk = min(tm, m), min(tn, n), min(tk, kdim)
    assert m % tm == 0 and n % tn == 0 and kdim % tk == 0
    nk = kdim // tk
    in_specs = [pl.BlockSpec((tm, tk), lambda i, j, k: (i, k)) for _ in a_list]
    in_specs.append(pl.BlockSpec((tk, tn), lambda i, j, k: (k, j)))
    in_specs += [pl.BlockSpec((tm, tn), lambda i, j, k: (i, j)) for _ in rows]
    in_specs += [pl.BlockSpec((tm, b.shape[1]), lambda i, j, k: (i, 0)) for b in bands]
    in_specs += [pl.BlockSpec((1, tn), lambda i, j, k: (0, j)) for _ in cols]
    out_specs = [pl.BlockSpec((tm, tn), lambda i, j, k: (i, j)) for _ in out_dtypes]
    out_shape = [jax.ShapeDtypeStruct((m, n), dt) for dt in out_dtypes]
    body = functools.partial(_mm_body, n_a=len(a_list), n_extra=len(rows) + len(bands) + len(cols),
                             n_out=len(out_dtypes), pre=pre, post=post, nk=nk)
    return pl.pallas_call(
        body,
        grid=(m // tm, n // tn, nk),
        in_specs=in_specs,
        out_specs=out_specs,
        out_shape=out_shape,
        scratch_shapes=[pltpu.VMEM((tm, tn), F32)] if nk > 1 else [],
        compiler_params=_cparams(("parallel", "parallel", "arbitrary")),
        name=name,
    )(*a_list, w, *rows, *bands, *cols)


def _pre_id(x):
    return x


def _post_id(acc):
    return (acc,)


def _post_ln(alpha, acc, h, g, b):
    out = _layer_norm(alpha * h + acc, g, b)
    return (out, out)


PROJ_TM = 1024


def proj(hb, w, out_dtype, name, *, tn=1024):
    (out,) = fused_matmul([hb], w, pre=_pre_id, post=_post_id, out_dtypes=[out_dtype], tm=PROJ_TM, tn=tn, tk=2048,
                          name=name)
    return out


OUT_PROJ_SUB = 128


def _out_proj_body(z_ref, w_ref, h_ref, g_ref, b_ref, out_ref, outb_ref, *scratch, nk, alpha, tm):
    def finish():
        for r in range(tm // OUT_PROJ_SUB):
            rows = slice(r * OUT_PROJ_SUB, (r + 1) * OUT_PROJ_SUB)
            t = _dot(z_ref[rows, :], w_ref[...])
            if nk > 1:
                t = t + scratch[0][rows, :]
            o = _layer_norm(alpha * h_ref[rows, :] + t, g_ref[...], b_ref[...])
            out_ref[rows, :] = o
            outb_ref[rows, :] = o.astype(BF16)

    if nk == 1:
        finish()
    else:
        acc = scratch[0]
        k = pl.program_id(1)

        @pl.when(k == 0)
        def _():
            acc[...] = _dot(z_ref[...], w_ref[...])

        @pl.when(jnp.logical_and(k > 0, k < nk - 1))
        def _():
            acc[...] += _dot(z_ref[...], w_ref[...])

        pl.when(k == nk - 1)(finish)


def out_proj_ln(z, w, h, g, b, alpha, *, tm=512, tk=2048):
    m, kdim = z.shape
    n = w.shape[1]
    tm = min(tm, m)
    nk = kdim // tk
    assert kdim % tk == 0 and m % tm == 0 and tm % OUT_PROJ_SUB == 0
    row = pl.BlockSpec((tm, n), lambda i, k: (i, 0))
    vec = pl.BlockSpec((1, n), lambda i, k: (0, 0))
    return pl.pallas_call(
        functools.partial(_out_proj_body, nk=nk, alpha=alpha, tm=tm),
        grid=(m // tm, nk),
        in_specs=[pl.BlockSpec((tm, tk), lambda i, k: (i, k)), pl.BlockSpec((tk, n), lambda i, k: (k, 0)),
                  row, vec, vec],
        out_specs=[row, row],
        out_shape=[jax.ShapeDtypeStruct((m, n), F32), jax.ShapeDtypeStruct((m, n), BF16)],
        scratch_shapes=[pltpu.VMEM((tm, n), F32)] if nk > 1 else [],
        compiler_params=_cparams(("parallel", "arbitrary")),
        name="out_proj_ln",
    )(z, w, h, g, b)


def _ffn_body(x_ref, xb_ref, wg_ref, wu_ref, wo_ref, g_ref, b_ref, out_ref, outb_ref, acc_ref, *, nf, alpha):
    f = pl.program_id(1)

    @pl.when(f == 0)
    def _():
        acc_ref[...] = jnp.zeros_like(acc_ref)

    xb = xb_ref[...]
    gate = _dot(xb, wg_ref[...])
    up = _dot(xb, wu_ref[...])
    act = (_silu(gate) * up).astype(BF16)
    acc_ref[...] += _dot(act, wo_ref[...])

    @pl.when(f == nf - 1)
    def _():
        out = _layer_norm(alpha * x_ref[...] + acc_ref[...], g_ref[...], b_ref[...])
        out_ref[...] = out
        outb_ref[...] = out.astype(BF16)


def ffn_ln(x, xb, w_in, w_out, g, b, alpha, *, tm=512, tf=512):
    m, d = x.shape
    dff = w_out.shape[0]
    tm = min(tm, m)
    nf = dff // tf
    assert dff % tf == 0 and m % tm == 0
    row = pl.BlockSpec((tm, d), lambda i, f: (i, 0))
    return pl.pallas_call(
        functools.partial(_ffn_body, nf=nf, alpha=alpha),
        grid=(m // tm, nf),
        in_specs=[
            row, row,
            pl.BlockSpec((d, tf), lambda i, f: (0, f)),
            pl.BlockSpec((d, tf), lambda i, f: (0, nf + f)),
            pl.BlockSpec((tf, d), lambda i, f: (f, 0)),
            pl.BlockSpec((1, d), lambda i, f: (0, 0)),
            pl.BlockSpec((1, d), lambda i, f: (0, 0)),
        ],
        out_specs=[row, row],
        out_shape=[jax.ShapeDtypeStruct((m, d), F32), jax.ShapeDtypeStruct((m, d), BF16)],
        scratch_shapes=[pltpu.VMEM((tm, d), F32)],
        compiler_params=_cparams(("parallel", "arbitrary")),
        name="ffn_ln",
    )(x, xb, w_in, w_in, w_out, g, b)


def _rwkv_proj_body(x_ref, halo_ref, mu_ref, w_ref, out_ref, *, tm, seq):
    i = pl.program_id(0)
    x = x_ref[...]
    halo = jnp.where((i * tm) % seq == 0, 0.0, halo_ref[...])
    xx = _shifted(x, halo, 1) - x
    lhs = (x + xx * mu_ref[...]).astype(BF16)
    out_ref[...] = _dot(lhs, w_ref[...]).astype(out_ref.dtype)


def rwkv_proj(x, mu3, w3, seq, *, tm=512):
    m, d = x.shape
    n = w3.shape[2]
    tm = min(tm, seq)
    per = tm // SUBLANES
    return pl.pallas_call(
        functools.partial(_rwkv_proj_body, tm=tm, seq=seq),
        grid=(m // tm, 3),
        in_specs=[
            pl.BlockSpec((tm, d), lambda i, j: (i, 0)),
            pl.BlockSpec((SUBLANES, d), lambda i, j: (jnp.maximum(i * per - 1, 0), 0)),
            pl.BlockSpec((None, 1, d), lambda i, j: (j, 0, 0)),
            pl.BlockSpec((None, d, n), lambda i, j: (j, 0, 0)),
        ],
        out_specs=pl.BlockSpec((None, tm, n), lambda i, j: (j, i, 0)),
        out_shape=jax.ShapeDtypeStruct((3, m, n), BF16),
        compiler_params=_cparams(("parallel", "arbitrary")),
        name="rwkv_proj",
    )(x, x, mu3, w3)


def _rwkv_lora_body(x_ref, halo_ref, mu_ref, w1_ref, w2_ref, a1_ref, a2_ref, g1_ref, g2_ref, w0_ref, a0_ref,
                    wl_ref, a_ref, g_ref, *, tm, seq):
    i = pl.program_id(0)
    x = x_ref[...]
    halo = jnp.where((i * tm) % seq == 0, 0.0, halo_ref[...])
    xx = _shifted(x, halo, 1) - x
    mu = mu_ref[...]
    xw = (x + xx * mu[0:1, :]).astype(BF16)
    xa = (x + xx * mu[1:2, :]).astype(BF16)
    xg = (x + xx * mu[2:3, :]).astype(BF16)
    zw = w0_ref[...] + _dot(jnp.tanh(_dot(xw, w1_ref[...])).astype(BF16), w2_ref[...])
    wl_ref[...] = _neg_softplus_neg(zw) - 0.5
    a_ref[...] = jax.nn.sigmoid(a0_ref[...] + _dot(_dot(xa, a1_ref[...]).astype(BF16), a2_ref[...]))
    g_ref[...] = _dot(jax.nn.sigmoid(_dot(xg, g1_ref[...])).astype(BF16), g2_ref[...]).astype(g_ref.dtype)


def rwkv_lora(x, mu3, w1, w2, a1, a2, g1, g2, w0, a0, seq, *, tm=256):
    m, d = x.shape
    tm = min(tm, seq)
    per = tm // SUBLANES
    full = lambda arr: pl.BlockSpec(arr.shape, lambda i: (0,) * arr.ndim)
    row = pl.BlockSpec((tm, d), lambda i: (i, 0))
    return pl.pallas_call(
        functools.partial(_rwkv_lora_body, tm=tm, seq=seq),
        grid=(m // tm,),
        in_specs=[row, pl.BlockSpec((SUBLANES, d), lambda i: (jnp.maximum(i * per - 1, 0), 0)),
                  full(mu3), full(w1), full(w2), full(a1), full(a2), full(g1), full(g2), full(w0), full(a0)],
        out_specs=[row, row, row],
        out_shape=[jax.ShapeDtypeStruct((m, d), F32), jax.ShapeDtypeStruct((m, d), F32),
                   jax.ShapeDtypeStruct((m, d), BF16)],
        compiler_params=_cparams(("parallel",)),
        name="rwkv_lora",
    )(x, x, mu3, w1, w2, a1, a2, g1, g2, w0, a0)


RW_PAIRS_PER_STEP = 16
RW_LEVELS = 6


def _rwkv_masks():
    n = 2 * CHUNK
    t = np.arange(n)[:, None]
    s = np.arange(n)[None, :]
    strict = (t > s).astype(np.float32)
    incl = (t >= s).astype(np.float32)
    bones = ((t // RW_HEAD) == (s // RW_HEAD)).astype(np.float32)
    tri = np.stack([strict, incl, bones])
    lvl = []
    for lv in range(RW_LEVELS):
        sz = 1 << lv
        lvl.append(((t // (2 * sz) == s // (2 * sz)) & ((t // sz) % 2 == 1) & ((s // sz) % 2 == 0)).astype(np.float32))
    t64 = incl[:CHUNK, :CHUNK]
    return jnp.asarray(tri), jnp.asarray(np.stack(lvl)), jnp.asarray(t64)


def _rwkv_core_body(r_ref, k_ref, v_ref, wl_ref, a_ref, g_ref, kk_ref, ka_ref, rk_ref, lng_ref, lnb_ref,
                    tri_ref, lvl_ref, t64_ref, out_ref, st_ref):
    c = pl.program_id(2)

    @pl.when(c == 0)
    def _():
        st_ref[...] = jnp.zeros_like(st_ref)

    strict, incl = tri_ref[0], tri_ref[1]
    eye = incl - strict
    t64 = t64_ref[...].astype(BF16)
    lo = lax.broadcasted_iota(jnp.int32, (1, LANES), 1) < RW_HEAD
    inv_head = 1.0 / RW_HEAD

    def stack(x):
        return jnp.concatenate([jnp.where(lo, x, 0.0), jnp.where(lo, 0.0, x)], axis=0).astype(BF16)

    pairs = range(RW_PAIRS_PER_STEP)
    sls = [slice(q * LANES, (q + 1) * LANES) for q in pairs]

    ld = lambda ref, sl: ref[:, sl].astype(F32)
    kk_sq = []
    for sl in sls:
        kk = ld(k_ref, sl) * kk_ref[:, sl]
        kk_sq.append(_half_sums(lo, kk * kk))
    cums = [_dot_01(t64, -jnp.exp(wl_ref[:, sl])) for sl in sls]

    ops = []
    for sl, ss, cum in zip(sls, kk_sq, cums):
        k, a = ld(k_ref, sl), a_ref[:, sl]
        kk = k * kk_ref[:, sl] / jnp.maximum(jnp.sqrt(ss), 1e-12)
        k2 = k * (1.0 + (a - 1.0) * ka_ref[:, sl])
        lw = -jnp.exp(wl_ref[:, sl])
        cl = cum[CHUNK - 1:CHUNK, :]
        e_neg = jnp.exp(-cum)
        e_end = jnp.exp(cl - cum)
        kb = kk * a
        ops.append(dict(
            a_s=stack(-kk * jnp.exp(cum - lw)),
            r_s=stack(ld(r_ref, sl) * jnp.exp(cum)),
            b_s=stack(kb * e_neg), k_s=stack(k2 * e_neg), v_s=stack(ld(v_ref, sl)),
            bh_s=stack(kb * e_end), kh_s=stack(k2 * e_end),
            w_end=jnp.exp(cl), k2=k2))

    n2 = 2 * CHUNK
    quads = [(2 * i, 2 * i + 1) for i in range(RW_PAIRS_PER_STEP // 2)]
    zero = jnp.zeros((n2, n2), BF16)
    rows2 = lambda top, bot: jnp.concatenate([top, bot], axis=0)
    side = lambda xa, xb: jnp.concatenate([xa, xb], axis=1)
    bdiag = lambda xa, xb: rows2(side(xa, zero), side(zero, xb))
    half = lambda x, j: x[:, j * n2:(j + 1) * n2]
    col = lambda key, q: side(ops[q[0]][key], ops[q[1]][key])

    gmask = rows2(side(strict, strict), side(incl, incl))
    gram = [gmask * _dot_nt(rows2(o["a_s"], o["r_s"]), rows2(o["b_s"], o["k_s"])) for o in ops]
    a_ab = [g[:n2, :n2] for g in gram]
    a_ak = [g[:n2, n2:].astype(BF16) for g in gram]
    r_b = [g[n2:, :n2].astype(BF16) for g in gram]
    r_k = [g[n2:, n2:].astype(BF16) for g in gram]

    t_inv = [eye + m * lvl_ref[0] for m in a_ab]
    for lv in range(1, RW_LEVELS):
        tb = [t.astype(BF16) for t in t_inv]
        off = [(m * lvl_ref[lv]).astype(BF16) for m in a_ab]
        xs = [_dot(side(tb[a], tb[b]), bdiag(off[a], off[b])).astype(BF16) for a, b in quads]
        upd = [_dot(x, bdiag(tb[a], tb[b])) for x, (a, b) in zip(xs, quads)]
        t_inv = [t + half(upd[p // 2], p % 2) for p, t in enumerate(t_inv)]
    tb = [t.astype(BF16) for t in t_inv]

    s_old = [st_ref[q] for q in pairs]
    s_b = [s.astype(BF16) for s in s_old]
    p12 = [_dot_nt(rows2(col("a_s", q), col("r_s", q)), bdiag(s_b[q[0]], s_b[q[1]]))
           + _dot(rows2(side(a_ak[q[0]], a_ak[q[1]]), side(r_k[q[0]], r_k[q[1]])), bdiag(ops[q[0]]["v_s"], ops[q[1]]["v_s"]))
           for q in quads]
    rhs = [p[:n2].astype(BF16) for p in p12]
    u_b = [_dot(side(tb[a], tb[b]), bdiag(half(x, 0), half(x, 1))).astype(BF16) for x, (a, b) in zip(rhs, quads)]
    y_q = [p[n2:] + _dot(side(r_b[a], r_b[b]), bdiag(half(u, 0), half(u, 1))) for p, u, (a, b) in zip(p12, u_b, quads)]
    inc = [_dot_tn(rows2(u, col("v_s", q)), rows2(col("bh_s", q), col("kh_s", q))) for u, q in zip(u_b, quads)]
    for z, (a, b) in zip(inc, quads):
        st_ref[a] = s_old[a] * ops[a]["w_end"] + z[:n2, :n2]
        st_ref[b] = s_old[b] * ops[b]["w_end"] + z[n2:, n2:]

    y_s = [half(y_q[p // 2], p % 2) for p in pairs]
    ys = [y[:CHUNK] + y[CHUNK:] for y in y_s]
    mus = [_half_sums(lo, y) * inv_head for y in ys]
    ds = [y - mu for y, mu in zip(ys, mus)]
    var = [_half_sums(lo, d * d) * inv_head for d in ds]
    bonus = [_half_sums(lo, ld(r_ref, sl) * o["k2"] * rk_ref[:, sl]) for sl, o in zip(sls, ops)]
    for sl, d, vr, bn in zip(sls, ds, var, bonus):
        yn = d * lax.rsqrt(vr + RW_GN_EPS) * lng_ref[:, sl] + lnb_ref[:, sl]
        out_ref[:, sl] = ((yn + bn * ld(v_ref, sl)) * ld(g_ref, sl)).astype(out_ref.dtype)


def rwkv_core(rkv, wl, a, g, k_k, k_a, r_k, ln_g, ln_b, bsz, seq):
    _, m, d = rkv.shape
    width = RW_PAIRS_PER_STEP * LANES
    nc = seq // CHUNK
    tri, lvl, t64 = _rwkv_masks()
    row = lambda b, p, c: (b * nc + c, p)
    rkv_spec = lambda j: pl.BlockSpec((None, CHUNK, width), lambda b, p, c: (j, b * nc + c, p))
    act = pl.BlockSpec((CHUNK, width), row)
    par = pl.BlockSpec((1, width), lambda b, p, c: (0, p))
    full = lambda arr: pl.BlockSpec(arr.shape, lambda b, p, c: (0,) * arr.ndim)
    return pl.pallas_call(
        _rwkv_core_body,
        grid=(bsz, d // width, nc),
        in_specs=[rkv_spec(0), rkv_spec(1), rkv_spec(2), act, act, act, par, par, par, par, par,
                  full(tri), full(lvl), full(t64)],
        out_specs=act,
        out_shape=jax.ShapeDtypeStruct((m, d), BF16),
        scratch_shapes=[pltpu.VMEM((RW_PAIRS_PER_STEP, LANES, LANES), F32)],
        compiler_params=_cparams(("parallel", "parallel", "arbitrary")),
        name="rwkv_core",
    )(rkv, rkv, rkv, wl, a, g, k_k, k_a, r_k, ln_g, ln_b, tri, lvl, t64)


def _pad_to(x, axis, size):
    pad = [(0, 0)] * x.ndim
    pad[axis] = (0, size - x.shape[axis])
    return jnp.pad(x, pad)


def rwkv_layer(h, bsz, seq, mu, w_in, w0, w1, w2, a0, a1, a2, g1, g2, k_k, k_a, r_k, gn_g, gn_b, w_out,
               ln_g, ln_b, alpha):
    d = h.shape[1]
    v2 = lambda t: t.reshape(1, d)
    mu_rkv = jnp.stack([mu[0], mu[2], mu[3]]).reshape(3, 1, d)
    mu_lora = jnp.stack([mu[1], mu[4], mu[5]])
    rkv = rwkv_proj(h, mu_rkv, w_in.astype(BF16), seq)
    wl, a, g = rwkv_lora(
        h, mu_lora,
        _pad_to(w1, 1, LANES).astype(BF16), _pad_to(w2, 0, LANES).astype(BF16),
        _pad_to(a1, 1, LANES).astype(BF16), _pad_to(a2, 0, LANES).astype(BF16),
        g1.astype(BF16), g2.astype(BF16), v2(w0), v2(a0), seq)
    z = rwkv_core(rkv, wl, a, g, v2(k_k), v2(k_a), v2(r_k), v2(gn_g), v2(gn_b), bsz, seq)
    return out_proj_ln(z, w_out.astype(BF16), h, ln_g, ln_b, alpha)


def _ssd_consts(heads):
    lanes = heads * SSD_HEAD
    head_of = np.arange(lanes) // SSD_HEAD
    pos_of = np.arange(lanes) % SSD_HEAD
    expand = (np.arange(LANES)[:, None] == head_of[None, :]).astype(np.float32)
    eye = (np.arange(CHUNK)[:, None] == pos_of[None, :]).astype(np.float32)
    tril = (np.arange(CHUNK)[:, None] >= pos_of[None, :]).astype(np.float32)
    return jnp.asarray(expand, BF16), jnp.asarray(eye), jnp.asarray(tril)


def _expand_heads(x, e01):
    hi = x.astype(BF16)
    low = (x - hi.astype(F32)).astype(BF16)
    return _dot(hi, e01) + _dot(low, e01)


def _ssd_body(xbc_ref, halo_ref, z_ref, dtp_ref, cw_ref, cb_ref, dtb_ref, alog_ref, dsk_ref, ng_ref, t64_ref,
              e01_ref, eye_ref, tril_ref, out_ref, st_ref, xc_ref, dt_ref, da_ref, *, tq, d_inner):
    s = pl.program_id(1)

    @pl.when(s == 0)
    def _():
        st_ref[...] = jnp.zeros_like(st_ref)

    halo = jnp.where(s == 0, 0.0, halo_ref[...].astype(F32)[BF16_ROWS - SUBLANES:, :])
    xc_ref[...] = _silu(_causal_conv4(xbc_ref[...].astype(F32), halo, cw_ref[...], cb_ref[...]))
    dt_all = _softplus(dtp_ref[...] + dtb_ref[...])
    dt_ref[...] = dt_all
    da_ref[...] = dt_all * (-jnp.exp(alog_ref[...]))

    t64b = t64_ref[...].astype(BF16)
    lo = lax.broadcasted_iota(jnp.int32, (1, LANES), 1) < SSD_HEAD
    gn = SSD_GROUPS * SSD_STATE
    gw = d_inner // SSD_GROUPS
    pairs_per_group = gw // LANES

    def chunk(ci, carry):
        r0 = pl.multiple_of(ci * CHUNK, CHUNK)
        rows = pl.ds(r0, CHUNK)
        e01 = e01_ref[...]
        cs = _dot_01(t64b, da_ref[rows, :])
        dte = _expand_heads(dt_ref[rows, :], e01)
        cse = _expand_heads(cs, e01)
        cs_row = jnp.sum(eye_ref[...] * cse, axis=0, keepdims=True)
        decay = jnp.exp(jnp.minimum(cse - cs_row, 0.0)) * tril_ref[...]
        cs_last = cse[CHUNK - 1:CHUNK, :]
        ecs = jnp.exp(cse)
        to_end = jnp.exp(cs_last - cse)
        e_last = jnp.exp(cs_last)
        for g in range(SSD_GROUPS):
            gsl = slice(g * gw, (g + 1) * gw)
            bg = xc_ref[rows, d_inner + g * SSD_STATE:d_inner + (g + 1) * SSD_STATE]
            cg = xc_ref[rows, d_inner + gn + g * SSD_STATE:d_inner + gn + (g + 1) * SSD_STATE].astype(BF16)
            bgb = bg.astype(BF16)
            cb_rep = _dot_nt(cg, jnp.concatenate([bgb] * (gw // SSD_HEAD), axis=0))
            mat = (cb_rep * decay[:, gsl]).astype(BF16)
            bg_t = bg.T.astype(BF16)
            ps = [g * pairs_per_group + q for q in range(pairs_per_group)]
            sls = [slice(p * LANES, (p + 1) * LANES) for p in ps]
            xs = [xc_ref[rows, sl] for sl in sls]
            xdts = [x * dte[:, sl] for x, sl in zip(xs, sls)]
            x_st = [jnp.concatenate([jnp.where(lo, xdt, 0.0), jnp.where(lo, 0.0, xdt)], axis=0).astype(BF16)
                    for xdt in xdts]
            x_end = [(xdt * to_end[:, sl]).astype(BF16) for xdt, sl in zip(xdts, sls)]
            sts = [st_ref[p] for p in ps]
            intra = [_dot(mat[:, q * LANES:(q + 1) * LANES], xs_) for q, xs_ in enumerate(x_st)]
            inter = [_dot(cg, st.astype(BF16)) for st in sts]
            upd = [_dot(bg_t, xe) for xe in x_end]
            for p, sl, st, u in zip(ps, sls, sts, upd):
                st_ref[p] = st * e_last[:, sl] + u
            ys = []
            ssq = jnp.zeros((CHUNK, 1), F32)
            for sl, x, ya, yb in zip(sls, xs, intra, inter):
                y = ya + yb * ecs[:, sl] + x * dsk_ref[:, sl]
                y = y * _silu(z_ref[rows, sl].astype(F32))
                ssq = ssq + jnp.sum(y * y, axis=-1, keepdims=True)
                ys.append(y)
            inv = lax.rsqrt(ssq * (1.0 / gw) + 1e-5)
            for sl, y in zip(sls, ys):
                out_ref[rows, sl] = (y * inv * ng_ref[:, sl]).astype(out_ref.dtype)
        return carry

    lax.fori_loop(0, tq // CHUNK, chunk, 0)


def ssd_core(xbc_pre, z, dt_pre, conv_w, conv_b, dt_bias, a_log, d_skip, norm_g, bsz, seq, *, tq=256):
    n, conv_dim = xbc_pre.shape
    d_inner = z.shape[1]
    tq = min(tq, seq)
    nblk = seq // tq
    per = tq // BF16_ROWS
    npairs = d_inner // LANES
    t64 = _rwkv_masks()[2]
    e01, eye, tril = _ssd_consts(d_inner // SSD_HEAD)
    row = lambda b, s: (b * nblk + s, 0)
    full = lambda arr: pl.BlockSpec(arr.shape, lambda b, s: (0,) * arr.ndim)
    return pl.pallas_call(
        functools.partial(_ssd_body, tq=tq, d_inner=d_inner),
        grid=(bsz, nblk),
        in_specs=[
            pl.BlockSpec((tq, conv_dim), row),
            pl.BlockSpec((BF16_ROWS, conv_dim), lambda b, s: (jnp.maximum((b * nblk + s) * per - 1, 0), 0)),
            pl.BlockSpec((tq, d_inner), row),
            pl.BlockSpec((tq, LANES), row),
            full(conv_w), full(conv_b), full(dt_bias), full(a_log), full(d_skip), full(norm_g), full(t64),
            full(e01), full(eye), full(tril),
        ],
        out_specs=pl.BlockSpec((tq, d_inner), row),
        out_shape=jax.ShapeDtypeStruct((n, d_inner), BF16),
        scratch_shapes=[pltpu.VMEM((npairs, SSD_STATE, LANES), F32), pltpu.VMEM((tq, conv_dim), F32),
                        pltpu.VMEM((tq, LANES), F32), pltpu.VMEM((tq, LANES), F32)],
        compiler_params=_cparams(("arbitrary", "arbitrary")),
        name="ssd_core",
    )(xbc_pre, xbc_pre, z, dt_pre, conv_w, conv_b, dt_bias, a_log, d_skip, norm_g, t64, e01, eye, tril)


def ssd_layer(h, hb, bsz, seq, w_in, conv_w, conv_b, dt_bias, a_log, d_skip, norm_g, w_out, ln_g, ln_b, alpha):
    d_inner = w_out.shape[0]
    conv_dim = conv_w.shape[1]
    w_in = w_in.astype(BF16)
    z = proj(hb, w_in[:, :d_inner], BF16, "ssd_proj_z")
    xbc_pre = proj(hb, w_in[:, d_inner:d_inner + conv_dim], BF16, "ssd_proj_xbc")
    dt_pre = proj(hb, _pad_to(w_in[:, d_inner + conv_dim:], 1, LANES), F32, "ssd_proj_dt")
    pad_row = lambda t: _pad_to(t.reshape(1, -1), 1, LANES)
    d_skip_lanes = jnp.repeat(d_skip, SSD_HEAD).reshape(1, -1)
    y = ssd_core(xbc_pre, z, dt_pre, conv_w, conv_b.reshape(1, -1), pad_row(dt_bias), pad_row(a_log),
                 d_skip_lanes, norm_g.reshape(1, -1), bsz, seq)
    return out_proj_ln(y, w_out.astype(BF16), h, ln_g, ln_b, alpha)


ROPE_HALF = 16
ATTN_SUB_ROWS = 256


def _rope_table_body(pos_ref, freq_ref, cos_ref, sin_ref):
    ang = pos_ref[...].astype(F32) * freq_ref[...]
    lane = lax.broadcasted_iota(jnp.int32, (1, LANES), 1)
    sn = jnp.sin(ang)
    cos_ref[...] = jnp.cos(ang)
    sin_ref[...] = jnp.where(lane < ROPE_HALF, -sn, jnp.where(lane < 2 * ROPE_HALF, sn, 0.0))


def rope_tables(pos, freq, *, tm=1024):
    n = pos.shape[0]
    tm = min(tm, n)
    band = pl.BlockSpec((tm, LANES), lambda i: (i, 0))
    return pl.pallas_call(
        _rope_table_body,
        grid=(n // tm,),
        in_specs=[pl.BlockSpec((tm, 1), lambda i: (i, 0)), pl.BlockSpec((1, LANES), lambda i: (0, 0))],
        out_specs=[band, band],
        out_shape=[jax.ShapeDtypeStruct((n, LANES), F32)] * 2,
        compiler_params=_cparams(("parallel",)),
        name="rope_tables",
    )(pos, freq)


def _post_rope(scale, acc, cos, sin):
    lane = lax.broadcasted_iota(jnp.int32, (1, LANES), 1)
    outs = []
    for j in range(acc.shape[1] // LANES):
        t = acc[:, j * LANES:(j + 1) * LANES]
        partner = jnp.where(lane < ROPE_HALF, pltpu.roll(t, LANES - ROPE_HALF, axis=1), pltpu.roll(t, ROPE_HALF, axis=1))
        outs.append((t * cos + partner * sin) * scale)
    return (jnp.concatenate(outs, axis=1),)


def _attn_body(qi_ref, ki_ref, last_ref, q_ref, k_ref, v_ref, lq1_ref, lk1_ref, lq2_ref, lk2_ref, sg_ref,
               out_ref, m_ref, l_ref, acc_ref, *, tq, tk, hd):
    p = pl.program_id(2)
    qi, ki = qi_ref[p], ki_ref[p]

    @pl.when(ki == 0)
    def _():
        m_ref[...] = jnp.full_like(m_ref, NEG_BIG)
        l_ref[...] = jnp.zeros_like(l_ref)
        acc_ref[...] = jnp.zeros_like(acc_ref)

    sub = min(ATTN_SUB_ROWS, tq)
    n_sub = tq // sub

    def qk(r):
        rows = slice(r * sub, (r + 1) * sub)
        return [_dot_nt(q_ref[rows, j * hd:(j + 1) * hd], k_ref[:, j * hd:(j + 1) * hd]) for j in range(2)]

    def update(masked):
        v = v_ref[...]
        if masked:
            k_chunk = jnp.right_shift(ki * tk + lax.broadcasted_iota(jnp.int32, (1, tk), 1), CHUNK_SHIFT)
        scores = qk(0)
        for r in range(n_sub):
            nxt = qk(r + 1) if r + 1 < n_sub else None
            rows = slice(r * sub, (r + 1) * sub)
            if masked:
                row0 = qi * tq + r * sub
                q_chunk = jnp.right_shift(row0 + lax.broadcasted_iota(jnp.int32, (sub, 1), 0), CHUNK_SHIFT)
                visible = k_chunk <= q_chunk
                scores = [jnp.where(visible, s, NEG_BIG) for s in scores]
            for j in range(2):
                m_old = m_ref[j, rows, :]
                m_new = jnp.maximum(m_old, jnp.max(scores[j], axis=-1, keepdims=True))
                pe = jnp.exp2(scores[j] - m_new)
                scale = jnp.exp2(m_old - m_new)
                l_ref[j, rows, :] = scale * l_ref[j, rows, :] + jnp.sum(pe, axis=-1, keepdims=True)
                acc_ref[j, rows, :] = scale * acc_ref[j, rows, :] + _dot(pe.astype(BF16), v)
                m_ref[j, rows, :] = m_new
            scores = nxt

    needs_mask = (ki + 1) * tk > qi * tq + CHUNK

    @pl.when(needs_mask)
    def _():
        update(True)

    @pl.when(jnp.logical_not(needs_mask))
    def _():
        update(False)

    @pl.when(last_ref[p] == 1)
    def _():
        lam = (jnp.exp(jnp.sum(lq1_ref[...] * lk1_ref[...], axis=-1, keepdims=True))
               - jnp.exp(jnp.sum(lq2_ref[...] * lk2_ref[...], axis=-1, keepdims=True)) + DIF_LAMBDA_INIT)
        o = acc_ref[0] / l_ref[0] - lam * (acc_ref[1] / l_ref[1])
        o = o * lax.rsqrt(jnp.mean(o * o, axis=-1, keepdims=True) + 1e-5) * sg_ref[...]
        out_ref[...] = (o * (1.0 - DIF_LAMBDA_INIT)).astype(out_ref.dtype)


def diff_attention(q, k, v, lq1, lk1, lq2, lk2, subln_g, bsz, seq, *, tq=1024, tk=1024):
    n, d = q.shape
    hw = d // DIF_HEADS
    hd = hw // 2
    tq, tk = min(tq, seq), min(tk, seq)
    assert tq % CHUNK == 0 and tk % CHUNK == 0 and CHUNK == 1 << CHUNK_SHIFT
    nq, nk = seq // tq, seq // tk
    n_kv = lambda a: -(-((a + 1) * tq) // tk)
    pairs = [(a, b) for a in range(nq) for b in range(n_kv(a))]
    qi = jnp.asarray([a for a, _ in pairs], jnp.int32)
    ki = jnp.asarray([b for _, b in pairs], jnp.int32)
    last = jnp.asarray([1 if b == n_kv(a) - 1 else 0 for a, b in pairs], jnp.int32)
    vec = pl.BlockSpec((1, hd), lambda b, h, p, qi, ki, last: (0, 0))
    grid_spec = pltpu.PrefetchScalarGridSpec(
        num_scalar_prefetch=3,
        grid=(bsz, DIF_HEADS, len(pairs)),
        in_specs=[
            pl.BlockSpec((tq, hw), lambda b, h, p, qi, ki, last: (b * nq + qi[p], h)),
            pl.BlockSpec((tk, hw), lambda b, h, p, qi, ki, last: (b * nk + ki[p], h)),
            pl.BlockSpec((tk, hw), lambda b, h, p, qi, ki, last: (b * nk + ki[p], h)),
            vec, vec, vec, vec,
            pl.BlockSpec((1, hw), lambda b, h, p, qi, ki, last: (0, 0)),
        ],
        out_specs=pl.BlockSpec((tq, hw), lambda b, h, p, qi, ki, last: (b * nq + qi[p], h)),
        scratch_shapes=[pltpu.VMEM((2, tq, 1), F32), pltpu.VMEM((2, tq, 1), F32), pltpu.VMEM((2, tq, hw), F32)],
    )
    return pl.pallas_call(
        functools.partial(_attn_body, tq=tq, tk=tk, hd=hd),
        grid_spec=grid_spec,
        out_shape=jax.ShapeDtypeStruct((n, d), BF16),
        compiler_params=_cparams(("parallel", "parallel", "arbitrary")),
        name="diff_attention",
    )(qi, ki, last, q, k, v, lq1, lk1, lq2, lk2, subln_g)


def diff_layer(h, hb, positions, bsz, seq, w_in, lq1, lk1, lq2, lk2, subln_g, w_out, ln_g, ln_b, alpha):
    d = h.shape[1]
    hd = d // DIF_HEADS // 2
    rope = hd // 4
    assert hd == LANES and rope == 2 * ROPE_HALF
    inv_freq = ROPE_THETA ** (-jnp.arange(0, rope, 2, dtype=F32) / rope)
    freq = jnp.concatenate([inv_freq, inv_freq, jnp.zeros((LANES - rope,), F32)]).reshape(1, LANES)
    cos, sin = rope_tables(positions.reshape(bsz * seq, 1), freq)
    w_in = w_in.astype(BF16)
    rope_mm = lambda w, scale, name: fused_matmul([hb], w, bands=[cos, sin], pre=_pre_id,
                                                  post=functools.partial(_post_rope, scale), out_dtypes=[BF16],
                                                  tm=PROJ_TM, tn=1024, tk=2048, name=name)[0]
    q = rope_mm(w_in[:, :d], hd ** -0.5 * math.log2(math.e), "dif_proj_q")
    k = rope_mm(w_in[:, d:2 * d], 1.0, "dif_proj_k")
    v = proj(hb, w_in[:, 2 * d:], BF16, "dif_proj_v")
    v2 = lambda t: t.reshape(1, -1)
    o = diff_attention(q, k, v, v2(lq1), v2(lk1), v2(lq2), v2(lk2), v2(subln_g), bsz, seq)
    return out_proj_ln(o, w_out.astype(BF16), h, ln_g, ln_b, alpha)


def _post_gelu(acc):
    c = math.sqrt(2.0 / math.pi)
    return (0.5 * acc * (1.0 + jnp.tanh(c * (acc + 0.044715 * acc * acc * acc))),)


def _lru_body(u_ref, halo_ref, gate_ref, cw_ref, cb_ref, wa_ref, ba_ref, wx_ref, bx_ref, lam_ref,
              out_ref, h_ref, a_ref, b_ref, *, tq, width):
    s = pl.program_id(1)

    @pl.when(s == 0)
    def _():
        h_ref[...] = jnp.zeros_like(h_ref)

    halo = jnp.where(s == 0, 0.0, halo_ref[...])
    u = _causal_conv4_ref(u_ref, halo, cw_ref[...], cb_ref[...])

    nsp = _softplus(-lam_ref[...])
    blk = width // LRU_BLOCKS
    for j in range(LRU_BLOCKS):
        sl = slice(j * blk, (j + 1) * blk)
        uj = u[:, sl]
        ub = uj.astype(BF16)
        r = jax.nn.sigmoid(_dot(ub, wa_ref[j]) + ba_ref[:, sl])
        i = jax.nn.sigmoid(_dot(ub, wx_ref[j]) + bx_ref[:, sl])
        log_a = -LRU_C * r * nsp[:, sl]
        a_ref[:, sl] = jnp.exp(log_a)
        b_ref[:, sl] = jnp.sqrt(1.0 - jnp.exp(2.0 * log_a)) * (i * uj)

    def step(t, h):
        h = a_ref[pl.ds(t, 1), :] * h + b_ref[pl.ds(t, 1), :]
        b_ref[pl.ds(t, 1), :] = h
        return h

    h_ref[0:1, :] = lax.fori_loop(0, tq, step, h_ref[0:1, :], unroll=8)
    out_ref[...] = (gate_ref[...].astype(F32) * b_ref[...]).astype(out_ref.dtype)


def lru_core(u_pre, gate, conv_w, conv_b, w_a, b_a, w_x, b_x, lam, bsz, seq, *, tq=512):
    n, width = u_pre.shape
    tq = min(tq, seq)
    nblk = seq // tq
    per = tq // SUBLANES
    row = lambda b, s: (b * nblk + s, 0)
    full = lambda arr: pl.BlockSpec(arr.shape, lambda b, s: (0,) * arr.ndim)
    return pl.pallas_call(
        functools.partial(_lru_body, tq=tq, width=width),
        grid=(bsz, nblk),
        in_specs=[
            pl.BlockSpec((tq, width), row),
            pl.BlockSpec((SUBLANES, width), lambda b, s: (jnp.maximum((b * nblk + s) * per - 1, 0), 0)),
            pl.BlockSpec((tq, width), row),
            full(conv_w), full(conv_b), full(w_a), full(b_a), full(w_x), full(b_x), full(lam),
        ],
        out_specs=pl.BlockSpec((tq, width), row),
        out_shape=jax.ShapeDtypeStruct((n, width), BF16),
        scratch_shapes=[pltpu.VMEM((SUBLANES, width), F32), pltpu.VMEM((tq, width), F32),
                        pltpu.VMEM((tq, width), F32)],
        compiler_params=_cparams(("arbitrary", "arbitrary")),
        name="lru_core",
    )(u_pre, u_pre, gate, conv_w, conv_b, w_a, b_a, w_x, b_x, lam)


def rglru_layer(h, hb, bsz, seq, w_in, conv_w, conv_b, w_a, b_a, w_x, b_x, lam, w_out, ln_g, ln_b, alpha):
    width = w_in.shape[1] // 2
    w_in = w_in.astype(BF16)
    (gate,) = fused_matmul([hb], w_in[:, :width], pre=_pre_id, post=_post_gelu, out_dtypes=[BF16],
                           tm=PROJ_TM, tn=1024, tk=2048, name="lru_proj_gate")
    u_pre = proj(hb, w_in[:, width:], F32, "lru_proj_u")
    z = lru_core(u_pre, gate, conv_w, conv_b.reshape(1, -1), w_a.astype(BF16), b_a.reshape(1, -1),
                 w_x.astype(BF16), b_x.reshape(1, -1), lam.reshape(1, -1), bsz, seq)
    return out_proj_ln(z, w_out.astype(BF16), h, ln_g, ln_b, alpha)


def kernel(x, positions, rwkv_mu, rwkv_w_in, rwkv_w0, rwkv_w1, rwkv_w2, rwkv_a0, rwkv_a1, rwkv_a2, rwkv_g1, rwkv_g2, rwkv_k_k, rwkv_k_a, rwkv_r_k, rwkv_ln_g, rwkv_ln_b, rwkv_w_out, ssd_w_in, ssd_conv_w, ssd_conv_b, ssd_dt_bias, ssd_a_log, ssd_d, ssd_norm_g, ssd_w_out, dif_w_in, dif_lq1, dif_lk1, dif_lq2, dif_lk2, dif_subln_g, dif_w_out, lru_w_in, lru_conv_w, lru_conv_b, lru_w_a, lru_b_a, lru_w_x, lru_b_x, lru_lam, lru_w_out, ffn_w_in, ffn_w_out, ln_g, ln_b):
    bsz, seq, d = x.shape
    depth = ffn_w_in.shape[0]
    alpha = (2 * depth) ** 0.25
    h = x.reshape(bsz * seq, d)
    hb = None
    for i in range(depth):
        m = i % 4
        g0, b0 = ln_g[i, 0].reshape(1, d), ln_b[i, 0].reshape(1, d)
        g1, b1 = ln_g[i, 1].reshape(1, d), ln_b[i, 1].reshape(1, d)
        if m == 0:
            h, hb = rwkv_layer(h, bsz, seq, rwkv_mu, rwkv_w_in, rwkv_w0, rwkv_w1, rwkv_w2, rwkv_a0, rwkv_a1,
                               rwkv_a2, rwkv_g1, rwkv_g2, rwkv_k_k, rwkv_k_a, rwkv_r_k, rwkv_ln_g, rwkv_ln_b,
                               rwkv_w_out, g0, b0, alpha)
        elif m == 1:
            h, hb = ssd_layer(h, hb, bsz, seq, ssd_w_in, ssd_conv_w, ssd_conv_b, ssd_dt_bias, ssd_a_log, ssd_d,
                              ssd_norm_g, ssd_w_out, g0, b0, alpha)
        elif m == 2:
            h, hb = diff_layer(h, hb, positions, bsz, seq, dif_w_in, dif_lq1, dif_lk1, dif_lq2, dif_lk2,
                               dif_subln_g, dif_w_out, g0, b0, alpha)
        else:
            h, hb = rglru_layer(h, hb, bsz, seq, lru_w_in, lru_conv_w, lru_conv_b, lru_w_a, lru_b_a, lru_w_x,
                                lru_b_x, lru_lam, lru_w_out, g0, b0, alpha)
        h, hb = ffn_ln(h, hb, ffn_w_in[i].astype(BF16), ffn_w_out[i].astype(BF16), g1, b1, alpha)
    return h.reshape(bsz, seq, d)
```

```python
import functools
import math

import numpy as np
import jax
import jax.numpy as jnp
from jax import lax
from jax.experimental import pallas as pl
from jax.experimental.pallas import tpu as pltpu

F32 = jnp.float32
BF16 = jnp.bfloat16
HI = lax.Precision.HIGHEST

LANES = 128
SUBLANES = 8
BF16_ROWS = 16
VMEM_LIMIT_BYTES = 56 * 1024 * 1024

LN_EPS = 1e-5
CHUNK = 64
CHUNK_SHIFT = 6
RW_HEAD = 64
RW_GN_EPS = 64e-5
SSD_HEAD = 64
SSD_GROUPS = 8
SSD_STATE = 128
DIF_HEADS = 8
ROPE_THETA = 500000.0
DIF_LAYER = 2
DIF_LAMBDA_INIT = 0.8 - 0.6 * math.exp(-0.3 * DIF_LAYER)
LRU_BLOCKS = 16
LRU_C = 8.0
NEG_BIG = -1e30


def _cparams(sem):
    return pltpu.CompilerParams(dimension_semantics=sem, vmem_limit_bytes=VMEM_LIMIT_BYTES)


def _dot(a, b):
    return jnp.dot(a, b, preferred_element_type=F32)


def _dot_nt(a, b):
    return lax.dot_general(a, b, (((1,), (1,)), ((), ())), preferred_element_type=F32)


def _dot_tn(a, b):
    return lax.dot_general(a, b, (((0,), (0,)), ((), ())), preferred_element_type=F32)


def _dot_hi(a, b):
    return jnp.dot(a, b, preferred_element_type=F32, precision=HI)


def _dot_01(m01, x):
    hi = x.astype(BF16)
    r1 = x - hi.astype(F32)
    mid = r1.astype(BF16)
    low = (r1 - mid.astype(F32)).astype(BF16)
    return _dot(m01, hi) + _dot(m01, mid) + _dot(m01, low)


def _half_sums(lo_mask, x):
    s_lo = jnp.sum(jnp.where(lo_mask, x, 0.0), axis=-1, keepdims=True)
    s_hi = jnp.sum(jnp.where(lo_mask, 0.0, x), axis=-1, keepdims=True)
    return jnp.where(lo_mask, s_lo, s_hi)


def _layer_norm(t, g, b):
    mu = jnp.mean(t, -1, keepdims=True)
    d = t - mu
    var = jnp.mean(d * d, -1, keepdims=True)
    return d * lax.rsqrt(var + LN_EPS) * g + b


def _neg_softplus_neg(z):
    return jnp.minimum(z, 0.0) - jnp.log(1.0 + jnp.exp(-jnp.abs(z)))


def _softplus(z):
    return jnp.maximum(z, 0.0) + jnp.log(1.0 + jnp.exp(-jnp.abs(z)))


def _silu(z):
    return z * jax.nn.sigmoid(z)


def _shifted(x, halo, d):
    head = pltpu.roll(jnp.concatenate([halo, x[:SUBLANES]], axis=0), d, axis=0)[SUBLANES:]
    if x.shape[0] == SUBLANES:
        return head
    return jnp.concatenate([head, pltpu.roll(x, d, axis=0)[SUBLANES:]], axis=0)


def _causal_conv4(x, halo, cw, cb):
    out = x * cw[3:4, :] + cb
    for d in (1, 2, 3):
        out = out + _shifted(x, halo, d) * cw[3 - d:4 - d, :]
    return out


def _causal_conv4_ref(x_ref, halo, cw, cb):
    n = x_ref.shape[0]
    head = _causal_conv4(x_ref[0:SUBLANES, :], halo, cw, cb)
    body = x_ref[SUBLANES:n, :] * cw[3:4, :] + cb
    for d in (1, 2, 3):
        body = body + x_ref[SUBLANES - d:n - d, :] * cw[3 - d:4 - d, :]
    return jnp.concatenate([head, body], axis=0)


def _mm_body(*refs, n_a, n_extra, n_out, pre, post, nk):
    a = refs[:n_a]
    w = refs[n_a]
    extra = refs[n_a + 1:n_a + 1 + n_extra]
    outs = refs[n_a + 1 + n_extra:n_a + 1 + n_extra + n_out]

    def finish(total):
        res = post(total, *[r[...] for r in extra])
        for o, v in zip(outs, res):
            o[...] = v.astype(o.dtype)

    lhs = pre(*[r[...] for r in a])
    if nk == 1:
        finish(_dot(lhs, w[...]))
    else:
        acc = refs[-1]
        k = pl.program_id(2)

        @pl.when(k == 0)
        def _():
            acc[...] = jnp.zeros_like(acc)

        acc[...] += _dot(lhs, w[...])

        @pl.when(k == nk - 1)
        def _():
            finish(acc[...])


def fused_matmul(a_list, w, *, rows=(), bands=(), cols=(), pre, post, out_dtypes, tm, tn, tk, name):
    m, kdim = a_list[0].shape
    n = w.shape[1]
    tm, tn, tk = min(tm, m), min(tn, n), min(tk, kdim)
    assert m % tm == 0 and n % tn == 0 and kdim % tk == 0
    nk = kdim // tk
    in_specs = [pl.BlockSpec((tm, tk), lambda i, j, k: (i, k)) for _ in a_list]
    in_specs.append(pl.BlockSpec((None, tk, tn), lambda i, j, k: (j, k, 0)))
    in_specs += [pl.BlockSpec((tm, tn), lambda i, j, k: (i, j)) for _ in rows]
    in_specs += [pl.BlockSpec((tm, b.shape[1]), lambda i, j, k: (i, 0)) for b in bands]
    in_specs += [pl.BlockSpec((1, tn), lambda i, j, k: (0, j)) for _ in cols]
    out_specs = [pl.BlockSpec((tm, tn), lambda i, j, k: (i, j)) for _ in out_dtypes]
    out_shape = [jax.ShapeDtypeStruct((m, n), dt) for dt in out_dtypes]
    body = functools.partial(_mm_body, n_a=len(a_list), n_extra=len(rows) + len(bands) + len(cols),
                             n_out=len(out_dtypes), pre=pre, post=post, nk=nk)
    return pl.pallas_call(
        body,
        grid=(m // tm, n // tn, nk),
        in_specs=in_specs,
        out_specs=out_specs,
        out_shape=out_shape,
        scratch_shapes=[pltpu.VMEM((tm, tn), F32)] if nk > 1 else [],
        compiler_params=_cparams(("parallel", "parallel", "arbitrary")),
        name=name,
    )(*a_list, _tile_cols(w, tn), *rows, *bands, *cols)


def _pre_id(x):
    return x


def _post_id(acc):
    return (acc,)


def _post_ln(alpha, acc, h, g, b):
    out = _layer_norm(alpha * h + acc, g, b)
    return (out, out)


PROJ_TM = 1024


def proj(hb, w, out_dtype, name, *, tn=1024):
    (out,) = fused_matmul([hb], w, pre=_pre_id, post=_post_id, out_dtypes=[out_dtype], tm=PROJ_TM, tn=tn, tk=2048,
                          name=name)
    return out


OUT_PROJ_SUB = 128


def _out_proj_body(z_ref, w_ref, h_ref, g_ref, b_ref, out_ref, outb_ref, *scratch, nk, alpha, tm):
    def finish():
        for r in range(tm // OUT_PROJ_SUB):
            rows = slice(r * OUT_PROJ_SUB, (r + 1) * OUT_PROJ_SUB)
            t = _dot(z_ref[rows, :], w_ref[...])
            if nk > 1:
                t = t + scratch[0][rows, :]
            o = _layer_norm(alpha * h_ref[rows, :] + t, g_ref[...], b_ref[...])
            out_ref[rows, :] = o
            outb_ref[rows, :] = o.astype(BF16)

    if nk == 1:
        finish()
    else:
        acc = scratch[0]
        k = pl.program_id(1)

        @pl.when(k == 0)
        def _():
            acc[...] = _dot(z_ref[...], w_ref[...])

        @pl.when(jnp.logical_and(k > 0, k < nk - 1))
        def _():
            acc[...] += _dot(z_ref[...], w_ref[...])

        pl.when(k == nk - 1)(finish)


def out_proj_ln(z, w, h, g, b, alpha, *, tm=512, tk=2048):
    m, kdim = z.shape
    n = w.shape[1]
    tm = min(tm, m)
    nk = kdim // tk
    assert kdim % tk == 0 and m % tm == 0 and tm % OUT_PROJ_SUB == 0
    row = pl.BlockSpec((tm, n), lambda i, k: (i, 0))
    vec = pl.BlockSpec((1, n), lambda i, k: (0, 0))
    return pl.pallas_call(
        functools.partial(_out_proj_body, nk=nk, alpha=alpha, tm=tm),
        grid=(m // tm, nk),
        in_specs=[pl.BlockSpec((tm, tk), lambda i, k: (i, k)), pl.BlockSpec((tk, n), lambda i, k: (k, 0)),
                  row, vec, vec],
        out_specs=[row, row],
        out_shape=[jax.ShapeDtypeStruct((m, n), F32), jax.ShapeDtypeStruct((m, n), BF16)],
        scratch_shapes=[pltpu.VMEM((tm, n), F32)] if nk > 1 else [],
        compiler_params=_cparams(("parallel", "arbitrary")),
        name="out_proj_ln",
    )(z, w, h, g, b)


def _ffn_body(x_ref, xb_ref, wg_ref, wu_ref, wo_ref, g_ref, b_ref, out_ref, outb_ref, acc_ref, *, nf, alpha):
    f = pl.program_id(1)

    @pl.when(f == 0)
    def _():
        acc_ref[...] = jnp.zeros_like(acc_ref)

    xb = xb_ref[...]
    gate = _dot(xb, wg_ref[...])
    up = _dot(xb, wu_ref[...])
    act = (_silu(gate) * up).astype(BF16)
    acc_ref[...] += _dot(act, wo_ref[...])

    @pl.when(f == nf - 1)
    def _():
        out = _layer_norm(alpha * x_ref[...] + acc_ref[...], g_ref[...], b_ref[...])
        out_ref[...] = out
        outb_ref[...] = out.astype(BF16)


FFN_TF = 512


def _tile_cols(w, tn):
    k, n = w.shape
    return w.reshape(k, n // tn, tn).transpose(1, 0, 2)


def ffn_ln(x, xb, w_in_tiles, w_out, g, b, alpha, *, tm=512):
    m, d = x.shape
    dff = w_out.shape[0]
    tf = w_in_tiles.shape[2]
    tm = min(tm, m)
    nf = dff // tf
    assert dff % tf == 0 and m % tm == 0 and w_in_tiles.shape[0] == 2 * nf
    row = pl.BlockSpec((tm, d), lambda i, f: (i, 0))
    return pl.pallas_call(
        functools.partial(_ffn_body, nf=nf, alpha=alpha),
        grid=(m // tm, nf),
        in_specs=[
            row, row,
            pl.BlockSpec((None, d, tf), lambda i, f: (f, 0, 0)),
            pl.BlockSpec((None, d, tf), lambda i, f: (nf + f, 0, 0)),
            pl.BlockSpec((tf, d), lambda i, f: (f, 0)),
            pl.BlockSpec((1, d), lambda i, f: (0, 0)),
            pl.BlockSpec((1, d), lambda i, f: (0, 0)),
        ],
        out_specs=[row, row],
        out_shape=[jax.ShapeDtypeStruct((m, d), F32), jax.ShapeDtypeStruct((m, d), BF16)],
        scratch_shapes=[pltpu.VMEM((tm, d), F32)],
        compiler_params=_cparams(("parallel", "arbitrary")),
        name="ffn_ln",
    )(x, xb, w_in_tiles, w_in_tiles, w_out, g, b)


def _rwkv_proj_body(x_ref, halo_ref, mu_ref, w_ref, out_ref, *, tm, seq):
    i = pl.program_id(0)
    x = x_ref[...]
    halo = jnp.where((i * tm) % seq == 0, 0.0, halo_ref[...])
    xx = _shifted(x, halo, 1) - x
    lhs = (x + xx * mu_ref[...]).astype(BF16)
    out_ref[...] = _dot(lhs, w_ref[...]).astype(out_ref.dtype)


def rwkv_proj(x, mu3, w3, seq, *, tm=512):
    m, d = x.shape
    n = w3.shape[2]
    tm = min(tm, seq)
    per = tm // SUBLANES
    return pl.pallas_call(
        functools.partial(_rwkv_proj_body, tm=tm, seq=seq),
        grid=(m // tm, 3),
        in_specs=[
            pl.BlockSpec((tm, d), lambda i, j: (i, 0)),
            pl.BlockSpec((SUBLANES, d), lambda i, j: (jnp.maximum(i * per - 1, 0), 0)),
            pl.BlockSpec((None, 1, d), lambda i, j: (j, 0, 0)),
            pl.BlockSpec((None, d, n), lambda i, j: (j, 0, 0)),
        ],
        out_specs=pl.BlockSpec((None, tm, n), lambda i, j: (j, i, 0)),
        out_shape=jax.ShapeDtypeStruct((3, m, n), BF16),
        compiler_params=_cparams(("parallel", "arbitrary")),
        name="rwkv_proj",
    )(x, x, mu3, w3)


def _rwkv_lora_body(x_ref, halo_ref, mu_ref, w1_ref, w2_ref, a1_ref, a2_ref, g1_ref, g2_ref, w0_ref, a0_ref,
                    wl_ref, a_ref, g_ref, *, tm, seq):
    i = pl.program_id(0)
    x = x_ref[...]
    halo = jnp.where((i * tm) % seq == 0, 0.0, halo_ref[...])
    xx = _shifted(x, halo, 1) - x
    mu = mu_ref[...]
    xw = (x + xx * mu[0:1, :]).astype(BF16)
    xa = (x + xx * mu[1:2, :]).astype(BF16)
    xg = (x + xx * mu[2:3, :]).astype(BF16)
    zw = w0_ref[...] + _dot(jnp.tanh(_dot(xw, w1_ref[...])).astype(BF16), w2_ref[...])
    wl_ref[...] = _neg_softplus_neg(zw) - 0.5
    a_ref[...] = jax.nn.sigmoid(a0_ref[...] + _dot(_dot(xa, a1_ref[...]).astype(BF16), a2_ref[...]))
    g_ref[...] = _dot(jax.nn.sigmoid(_dot(xg, g1_ref[...])).astype(BF16), g2_ref[...]).astype(g_ref.dtype)


def rwkv_lora(x, mu3, w1, w2, a1, a2, g1, g2, w0, a0, seq, *, tm=256):
    m, d = x.shape
    tm = min(tm, seq)
    per = tm // SUBLANES
    full = lambda arr: pl.BlockSpec(arr.shape, lambda i: (0,) * arr.ndim)
    row = pl.BlockSpec((tm, d), lambda i: (i, 0))
    return pl.pallas_call(
        functools.partial(_rwkv_lora_body, tm=tm, seq=seq),
        grid=(m // tm,),
        in_specs=[row, pl.BlockSpec((SUBLANES, d), lambda i: (jnp.maximum(i * per - 1, 0), 0)),
                  full(mu3), full(w1), full(w2), full(a1), full(a2), full(g1), full(g2), full(w0), full(a0)],
        out_specs=[row, row, row],
        out_shape=[jax.ShapeDtypeStruct((m, d), F32), jax.ShapeDtypeStruct((m, d), F32),
                   jax.ShapeDtypeStruct((m, d), BF16)],
        compiler_params=_cparams(("parallel",)),
        name="rwkv_lora",
    )(x, x, mu3, w1, w2, a1, a2, g1, g2, w0, a0)


RW_PAIRS_PER_STEP = 16
RW_LEVELS = 6


def _rwkv_masks():
    n = 2 * CHUNK
    t = np.arange(n)[:, None]
    s = np.arange(n)[None, :]
    strict = (t > s).astype(np.float32)
    incl = (t >= s).astype(np.float32)
    bones = ((t // RW_HEAD) == (s // RW_HEAD)).astype(np.float32)
    tri = np.stack([strict, incl, bones])
    lvl = []
    for lv in range(RW_LEVELS):
        sz = 1 << lv
        lvl.append(((t // (2 * sz) == s // (2 * sz)) & ((t // sz) % 2 == 1) & ((s // sz) % 2 == 0)).astype(np.float32))
    t64 = incl[:CHUNK, :CHUNK]
    return jnp.asarray(tri), jnp.asarray(np.stack(lvl)), jnp.asarray(t64)


def _rwkv_core_body(r_ref, k_ref, v_ref, wl_ref, a_ref, g_ref, kk_ref, ka_ref, rk_ref, lng_ref, lnb_ref,
                    tri_ref, lvl_ref, t64_ref, out_ref, st_ref):
    c = pl.program_id(2)

    @pl.when(c == 0)
    def _():
        st_ref[...] = jnp.zeros_like(st_ref)

    strict, incl = tri_ref[0], tri_ref[1]
    eye = incl - strict
    t64 = t64_ref[...].astype(BF16)
    lo = lax.broadcasted_iota(jnp.int32, (1, LANES), 1) < RW_HEAD
    inv_head = 1.0 / RW_HEAD

    def stack(x):
        return jnp.concatenate([jnp.where(lo, x, 0.0), jnp.where(lo, 0.0, x)], axis=0).astype(BF16)

    pairs = range(RW_PAIRS_PER_STEP)
    sls = [slice(q * LANES, (q + 1) * LANES) for q in pairs]

    ld = lambda ref, sl: ref[:, sl].astype(F32)
    kk_sq = []
    for sl in sls:
        kk = ld(k_ref, sl) * kk_ref[:, sl]
        kk_sq.append(_half_sums(lo, kk * kk))
    cums = [_dot_01(t64, -jnp.exp(wl_ref[:, sl])) for sl in sls]

    ops = []
    for sl, ss, cum in zip(sls, kk_sq, cums):
        k, a = ld(k_ref, sl), a_ref[:, sl]
        kk = k * kk_ref[:, sl] / jnp.maximum(jnp.sqrt(ss), 1e-12)
        k2 = k * (1.0 + (a - 1.0) * ka_ref[:, sl])
        lw = -jnp.exp(wl_ref[:, sl])
        cl = cum[CHUNK - 1:CHUNK, :]
        e_neg = jnp.exp(-cum)
        e_end = jnp.exp(cl - cum)
        kb = kk * a
        ops.append(dict(
            a_s=stack(-kk * jnp.exp(cum - lw)),
            r_s=stack(ld(r_ref, sl) * jnp.exp(cum)),
            b_s=stack(kb * e_neg), k_s=stack(k2 * e_neg), v_s=stack(ld(v_ref, sl)),
            bh_s=stack(kb * e_end), kh_s=stack(k2 * e_end),
            w_end=jnp.exp(cl), k2=k2))

    n2 = 2 * CHUNK
    quads = [(2 * i, 2 * i + 1) for i in range(RW_PAIRS_PER_STEP // 2)]
    zero = jnp.zeros((n2, n2), BF16)
    rows2 = lambda top, bot: jnp.concatenate([top, bot], axis=0)
    side = lambda xa, xb: jnp.concatenate([xa, xb], axis=1)
    bdiag = lambda xa, xb: rows2(side(xa, zero), side(zero, xb))
    half = lambda x, j: x[:, j * n2:(j + 1) * n2]
    col = lambda key, q: side(ops[q[0]][key], ops[q[1]][key])

    gmask = rows2(side(strict, strict), side(incl, incl))
    gram = [gmask * _dot_nt(rows2(o["a_s"], o["r_s"]), rows2(o["b_s"], o["k_s"])) for o in ops]
    a_ab = [g[:n2, :n2] for g in gram]
    a_ak = [g[:n2, n2:].astype(BF16) for g in gram]
    r_b = [g[n2:, :n2].astype(BF16) for g in gram]
    r_k = [g[n2:, n2:].astype(BF16) for g in gram]

    t_inv = [eye + m * lvl_ref[0] for m in a_ab]
    for lv in range(1, RW_LEVELS):
        tb = [t.astype(BF16) for t in t_inv]
        off = [(m * lvl_ref[lv]).astype(BF16) for m in a_ab]
        xs = [_dot(side(tb[a], tb[b]), bdiag(off[a], off[b])).astype(BF16) for a, b in quads]
        upd = [_dot(x, bdiag(tb[a], tb[b])) for x, (a, b) in zip(xs, quads)]
        t_inv = [t + half(upd[p // 2], p % 2) for p, t in enumerate(t_inv)]
    tb = [t.astype(BF16) for t in t_inv]

    s_old = [st_ref[q] for q in pairs]
    s_b = [s.astype(BF16) for s in s_old]
    p12 = [_dot_nt(rows2(col("a_s", q), col("r_s", q)), bdiag(s_b[q[0]], s_b[q[1]]))
           + _dot(rows2(side(a_ak[q[0]], a_ak[q[1]]), side(r_k[q[0]], r_k[q[1]])), bdiag(ops[q[0]]["v_s"], ops[q[1]]["v_s"]))
           for q in quads]
    rhs = [p[:n2].astype(BF16) for p in p12]
    u_b = [_dot(side(tb[a], tb[b]), bdiag(half(x, 0), half(x, 1))).astype(BF16) for x, (a, b) in zip(rhs, quads)]
    y_q = [p[n2:] + _dot(side(r_b[a], r_b[b]), bdiag(half(u, 0), half(u, 1))) for p, u, (a, b) in zip(p12, u_b, quads)]
    inc = [_dot_tn(rows2(u, col("v_s", q)), rows2(col("bh_s", q), col("kh_s", q))) for u, q in zip(u_b, quads)]
    for z, (a, b) in zip(inc, quads):
        st_ref[a] = s_old[a] * ops[a]["w_end"] + z[:n2, :n2]
        st_ref[b] = s_old[b] * ops[b]["w_end"] + z[n2:, n2:]

    y_s = [half(y_q[p // 2], p % 2) for p in pairs]
    ys = [y[:CHUNK] + y[CHUNK:] for y in y_s]
    mus = [_half_sums(lo, y) * inv_head for y in ys]
    ds = [y - mu for y, mu in zip(ys, mus)]
    var = [_half_sums(lo, d * d) * inv_head for d in ds]
    bonus = [_half_sums(lo, ld(r_ref, sl) * o["k2"] * rk_ref[:, sl]) for sl, o in zip(sls, ops)]
    for sl, d, vr, bn in zip(sls, ds, var, bonus):
        yn = d * lax.rsqrt(vr + RW_GN_EPS) * lng_ref[:, sl] + lnb_ref[:, sl]
        out_ref[:, sl] = ((yn + bn * ld(v_ref, sl)) * ld(g_ref, sl)).astype(out_ref.dtype)


def rwkv_core(rkv, wl, a, g, k_k, k_a, r_k, ln_g, ln_b, bsz, seq):
    _, m, d = rkv.shape
    width = RW_PAIRS_PER_STEP * LANES
    nc = seq // CHUNK
    tri, lvl, t64 = _rwkv_masks()
    row = lambda b, p, c: (b * nc + c, p)
    rkv_spec = lambda j: pl.BlockSpec((None, CHUNK, width), lambda b, p, c: (j, b * nc + c, p))
    act = pl.BlockSpec((CHUNK, width), row)
    par = pl.BlockSpec((1, width), lambda b, p, c: (0, p))
    full = lambda arr: pl.BlockSpec(arr.shape, lambda b, p, c: (0,) * arr.ndim)
    return pl.pallas_call(
        _rwkv_core_body,
        grid=(bsz, d // width, nc),
        in_specs=[rkv_spec(0), rkv_spec(1), rkv_spec(2), act, act, act, par, par, par, par, par,
                  full(tri), full(lvl), full(t64)],
        out_specs=act,
        out_shape=jax.ShapeDtypeStruct((m, d), BF16),
        scratch_shapes=[pltpu.VMEM((RW_PAIRS_PER_STEP, LANES, LANES), F32)],
        compiler_params=_cparams(("parallel", "parallel", "arbitrary")),
        name="rwkv_core",
    )(rkv, rkv, rkv, wl, a, g, k_k, k_a, r_k, ln_g, ln_b, tri, lvl, t64)


def _pad_to(x, axis, size):
    pad = [(0, 0)] * x.ndim
    pad[axis] = (0, size - x.shape[axis])
    return jnp.pad(x, pad)


def rwkv_layer(h, bsz, seq, mu, w_in, w0, w1, w2, a0, a1, a2, g1, g2, k_k, k_a, r_k, gn_g, gn_b, w_out,
               ln_g, ln_b, alpha):
    d = h.shape[1]
    v2 = lambda t: t.reshape(1, d)
    mu_rkv = jnp.stack([mu[0], mu[2], mu[3]]).reshape(3, 1, d)
    mu_lora = jnp.stack([mu[1], mu[4], mu[5]])
    rkv = rwkv_proj(h, mu_rkv, w_in.astype(BF16), seq)
    wl, a, g = rwkv_lora(
        h, mu_lora,
        _pad_to(w1, 1, LANES).astype(BF16), _pad_to(w2, 0, LANES).astype(BF16),
        _pad_to(a1, 1, LANES).astype(BF16), _pad_to(a2, 0, LANES).astype(BF16),
        g1.astype(BF16), g2.astype(BF16), v2(w0), v2(a0), seq)
    z = rwkv_core(rkv, wl, a, g, v2(k_k), v2(k_a), v2(r_k), v2(gn_g), v2(gn_b), bsz, seq)
    return out_proj_ln(z, w_out.astype(BF16), h, ln_g, ln_b, alpha)


def _ssd_consts(heads):
    lanes = heads * SSD_HEAD
    head_of = np.arange(lanes) // SSD_HEAD
    pos_of = np.arange(lanes) % SSD_HEAD
    expand = (np.arange(LANES)[:, None] == head_of[None, :]).astype(np.float32)
    eye = (np.arange(CHUNK)[:, None] == pos_of[None, :]).astype(np.float32)
    tril = (np.arange(CHUNK)[:, None] >= pos_of[None, :]).astype(np.float32)
    return jnp.asarray(expand, BF16), jnp.asarray(eye), jnp.asarray(tril)


def _expand_heads(x, e01):
    hi = x.astype(BF16)
    low = (x - hi.astype(F32)).astype(BF16)
    return _dot(hi, e01) + _dot(low, e01)


def _ssd_body(xbc_ref, halo_ref, z_ref, dtp_ref, cw_ref, cb_ref, dtb_ref, alog_ref, dsk_ref, ng_ref, t64_ref,
              e01_ref, eye_ref, tril_ref, out_ref, st_ref, xc_ref, dt_ref, da_ref, *, tq, d_inner):
    s = pl.program_id(1)

    @pl.when(s == 0)
    def _():
        st_ref[...] = jnp.zeros_like(st_ref)

    halo = jnp.where(s == 0, 0.0, halo_ref[...].astype(F32)[BF16_ROWS - SUBLANES:, :])
    xc_ref[...] = _silu(_causal_conv4(xbc_ref[...].astype(F32), halo, cw_ref[...], cb_ref[...]))
    dt_all = _softplus(dtp_ref[...] + dtb_ref[...])
    dt_ref[...] = dt_all
    da_ref[...] = dt_all * (-jnp.exp(alog_ref[...]))

    t64b = t64_ref[...].astype(BF16)
    lo = lax.broadcasted_iota(jnp.int32, (1, LANES), 1) < SSD_HEAD
    gn = SSD_GROUPS * SSD_STATE
    gw = d_inner // SSD_GROUPS
    pairs_per_group = gw // LANES

    def chunk(ci, carry):
        r0 = pl.multiple_of(ci * CHUNK, CHUNK)
        rows = pl.ds(r0, CHUNK)
        e01 = e01_ref[...]
        cs = _dot_01(t64b, da_ref[rows, :])
        dte = _expand_heads(dt_ref[rows, :], e01)
        cse = _expand_heads(cs, e01)
        cs_row = jnp.sum(eye_ref[...] * cse, axis=0, keepdims=True)
        decay = jnp.exp(jnp.minimum(cse - cs_row, 0.0)) * tril_ref[...]
        cs_last = cse[CHUNK - 1:CHUNK, :]
        ecs = jnp.exp(cse)
        to_end = jnp.exp(cs_last - cse)
        e_last = jnp.exp(cs_last)
        for g in range(SSD_GROUPS):
            gsl = slice(g * gw, (g + 1) * gw)
            bg = xc_ref[rows, d_inner + g * SSD_STATE:d_inner + (g + 1) * SSD_STATE]
            cg = xc_ref[rows, d_inner + gn + g * SSD_STATE:d_inner + gn + (g + 1) * SSD_STATE].astype(BF16)
            bgb = bg.astype(BF16)
            cb_rep = _dot_nt(cg, jnp.concatenate([bgb] * (gw // SSD_HEAD), axis=0))
            mat = (cb_rep * decay[:, gsl]).astype(BF16)
            bg_t = bg.T.astype(BF16)
            ps = [g * pairs_per_group + q for q in range(pairs_per_group)]
            sls = [slice(p * LANES, (p + 1) * LANES) for p in ps]
            xs = [xc_ref[rows, sl] for sl in sls]
            xdts = [x * dte[:, sl] for x, sl in zip(xs, sls)]
            x_st = [jnp.concatenate([jnp.where(lo, xdt, 0.0), jnp.where(lo, 0.0, xdt)], axis=0).astype(BF16)
                    for xdt in xdts]
            x_end = [(xdt * to_end[:, sl]).astype(BF16) for xdt, sl in zip(xdts, sls)]
            sts = [st_ref[p] for p in ps]
            intra = [_dot(mat[:, q * LANES:(q + 1) * LANES], xs_) for q, xs_ in enumerate(x_st)]
            inter = [_dot(cg, st.astype(BF16)) for st in sts]
            upd = [_dot(bg_t, xe) for xe in x_end]
            for p, sl, st, u in zip(ps, sls, sts, upd):
                st_ref[p] = st * e_last[:, sl] + u
            ys = []
            ssq = jnp.zeros((CHUNK, 1), F32)
            for sl, x, ya, yb in zip(sls, xs, intra, inter):
                y = ya + yb * ecs[:, sl] + x * dsk_ref[:, sl]
                y = y * _silu(z_ref[rows, sl].astype(F32))
                ssq = ssq + jnp.sum(y * y, axis=-1, keepdims=True)
                ys.append(y)
            inv = lax.rsqrt(ssq * (1.0 / gw) + 1e-5)
            for sl, y in zip(sls, ys):
                out_ref[rows, sl] = (y * inv * ng_ref[:, sl]).astype(out_ref.dtype)
        return carry

    lax.fori_loop(0, tq // CHUNK, chunk, 0)


def ssd_core(xbc_pre, z, dt_pre, conv_w, conv_b, dt_bias, a_log, d_skip, norm_g, bsz, seq, *, tq=256):
    n, conv_dim = xbc_pre.shape
    d_inner = z.shape[1]
    tq = min(tq, seq)
    nblk = seq // tq
    per = tq // BF16_ROWS
    npairs = d_inner // LANES
    t64 = _rwkv_masks()[2]
    e01, eye, tril = _ssd_consts(d_inner // SSD_HEAD)
    row = lambda b, s: (b * nblk + s, 0)
    full = lambda arr: pl.BlockSpec(arr.shape, lambda b, s: (0,) * arr.ndim)
    return pl.pallas_call(
        functools.partial(_ssd_body, tq=tq, d_inner=d_inner),
        grid=(bsz, nblk),
        in_specs=[
            pl.BlockSpec((tq, conv_dim), row),
            pl.BlockSpec((BF16_ROWS, conv_dim), lambda b, s: (jnp.maximum((b * nblk + s) * per - 1, 0), 0)),
            pl.BlockSpec((tq, d_inner), row),
            pl.BlockSpec((tq, LANES), row),
            full(conv_w), full(conv_b), full(dt_bias), full(a_log), full(d_skip), full(norm_g), full(t64),
            full(e01), full(eye), full(tril),
        ],
        out_specs=pl.BlockSpec((tq, d_inner), row),
        out_shape=jax.ShapeDtypeStruct((n, d_inner), BF16),
        scratch_shapes=[pltpu.VMEM((npairs, SSD_STATE, LANES), F32), pltpu.VMEM((tq, conv_dim), F32),
                        pltpu.VMEM((tq, LANES), F32), pltpu.VMEM((tq, LANES), F32)],
        compiler_params=_cparams(("arbitrary", "arbitrary")),
        name="ssd_core",
    )(xbc_pre, xbc_pre, z, dt_pre, conv_w, conv_b, dt_bias, a_log, d_skip, norm_g, t64, e01, eye, tril)


def ssd_layer(h, hb, bsz, seq, w_in, conv_w, conv_b, dt_bias, a_log, d_skip, norm_g, w_out, ln_g, ln_b, alpha):
    d_inner = w_out.shape[0]
    conv_dim = conv_w.shape[1]
    w_in = w_in.astype(BF16)
    z = proj(hb, w_in[:, :d_inner], BF16, "ssd_proj_z")
    xbc_pre = proj(hb, w_in[:, d_inner:d_inner + conv_dim], BF16, "ssd_proj_xbc")
    dt_pre = proj(hb, _pad_to(w_in[:, d_inner + conv_dim:], 1, LANES), F32, "ssd_proj_dt")
    pad_row = lambda t: _pad_to(t.reshape(1, -1), 1, LANES)
    d_skip_lanes = jnp.repeat(d_skip, SSD_HEAD).reshape(1, -1)
    y = ssd_core(xbc_pre, z, dt_pre, conv_w, conv_b.reshape(1, -1), pad_row(dt_bias), pad_row(a_log),
                 d_skip_lanes, norm_g.reshape(1, -1), bsz, seq)
    return out_proj_ln(y, w_out.astype(BF16), h, ln_g, ln_b, alpha)


ROPE_HALF = 16
ATTN_SUB_ROWS = 256


def _rope_table_body(pos_ref, freq_ref, cos_ref, sin_ref):
    ang = pos_ref[...].astype(F32) * freq_ref[...]
    lane = lax.broadcasted_iota(jnp.int32, (1, LANES), 1)
    sn = jnp.sin(ang)
    cos_ref[...] = jnp.cos(ang)
    sin_ref[...] = jnp.where(lane < ROPE_HALF, -sn, jnp.where(lane < 2 * ROPE_HALF, sn, 0.0))


def rope_tables(pos, freq, *, tm=1024):
    n = pos.shape[0]
    tm = min(tm, n)
    band = pl.BlockSpec((tm, LANES), lambda i: (i, 0))
    return pl.pallas_call(
        _rope_table_body,
        grid=(n // tm,),
        in_specs=[pl.BlockSpec((tm, 1), lambda i: (i, 0)), pl.BlockSpec((1, LANES), lambda i: (0, 0))],
        out_specs=[band, band],
        out_shape=[jax.ShapeDtypeStruct((n, LANES), F32)] * 2,
        compiler_params=_cparams(("parallel",)),
        name="rope_tables",
    )(pos, freq)


def _post_rope(scale, acc, cos, sin):
    lane = lax.broadcasted_iota(jnp.int32, (1, LANES), 1)
    outs = []
    for j in range(acc.shape[1] // LANES):
        t = acc[:, j * LANES:(j + 1) * LANES]
        partner = jnp.where(lane < ROPE_HALF, pltpu.roll(t, LANES - ROPE_HALF, axis=1), pltpu.roll(t, ROPE_HALF, axis=1))
        outs.append((t * cos + partner * sin) * scale)
    return (jnp.concatenate(outs, axis=1),)


def _attn_body(qi_ref, ki_ref, last_ref, q_ref, k_ref, v_ref, lq1_ref, lk1_ref, lq2_ref, lk2_ref, sg_ref,
               out_ref, m_ref, l_ref, acc_ref, *, tq, tk, hd):
    p = pl.program_id(2)
    qi, ki = qi_ref[p], ki_ref[p]

    @pl.when(ki == 0)
    def _():
        m_ref[...] = jnp.full_like(m_ref, NEG_BIG)
        l_ref[...] = jnp.zeros_like(l_ref)
        acc_ref[...] = jnp.zeros_like(acc_ref)

    sub = min(ATTN_SUB_ROWS, tq)
    n_sub = tq // sub

    def qk(r):
        rows = slice(r * sub, (r + 1) * sub)
        return [_dot_nt(q_ref[rows, j * hd:(j + 1) * hd], k_ref[:, j * hd:(j + 1) * hd]) for j in range(2)]

    def update(masked):
        v = v_ref[...]
        if masked:
            k_chunk = jnp.right_shift(ki * tk + lax.broadcasted_iota(jnp.int32, (1, tk), 1), CHUNK_SHIFT)
        scores = qk(0)
        for r in range(n_sub):
            nxt = qk(r + 1) if r + 1 < n_sub else None
            rows = slice(r * sub, (r + 1) * sub)
            if masked:
                row0 = qi * tq + r * sub
                q_chunk = jnp.right_shift(row0 + lax.broadcasted_iota(jnp.int32, (sub, 1), 0), CHUNK_SHIFT)
                visible = k_chunk <= q_chunk
                scores = [jnp.where(visible, s, NEG_BIG) for s in scores]
            for j in range(2):
                m_old = m_ref[j, rows, :]
                m_new = jnp.maximum(m_old, jnp.max(scores[j], axis=-1, keepdims=True))
                pe = jnp.exp2(scores[j] - m_new)
                scale = jnp.exp2(m_old - m_new)
                l_ref[j, rows, :] = scale * l_ref[j, rows, :] + jnp.sum(pe, axis=-1, keepdims=True)
                acc_ref[j, rows, :] = scale * acc_ref[j, rows, :] + _dot(pe.astype(BF16), v)
                m_ref[j, rows, :] = m_new
            scores = nxt

    needs_mask = (ki + 1) * tk > qi * tq + CHUNK

    @pl.when(needs_mask)
    def _():
        update(True)

    @pl.when(jnp.logical_not(needs_mask))
    def _():
        update(False)

    @pl.when(last_ref[p] == 1)
    def _():
        lam = (jnp.exp(jnp.sum(lq1_ref[...] * lk1_ref[...], axis=-1, keepdims=True))
               - jnp.exp(jnp.sum(lq2_ref[...] * lk2_ref[...], axis=-1, keepdims=True)) + DIF_LAMBDA_INIT)
        o = acc_ref[0] / l_ref[0] - lam * (acc_ref[1] / l_ref[1])
        o = o * lax.rsqrt(jnp.mean(o * o, axis=-1, keepdims=True) + 1e-5) * sg_ref[...]
        out_ref[...] = (o * (1.0 - DIF_LAMBDA_INIT)).astype(out_ref.dtype)


def diff_attention(q, k, v, lq1, lk1, lq2, lk2, subln_g, bsz, seq, *, tq=1024, tk=1024):
    n, d = q.shape
    hw = d // DIF_HEADS
    hd = hw // 2
    tq, tk = min(tq, seq), min(tk, seq)
    assert tq % CHUNK == 0 and tk % CHUNK == 0 and CHUNK == 1 << CHUNK_SHIFT
    nq, nk = seq // tq, seq // tk
    n_kv = lambda a: -(-((a + 1) * tq) // tk)
    pairs = [(a, b) for a in range(nq) for b in range(n_kv(a))]
    qi = jnp.asarray([a for a, _ in pairs], jnp.int32)
    ki = jnp.asarray([b for _, b in pairs], jnp.int32)
    last = jnp.asarray([1 if b == n_kv(a) - 1 else 0 for a, b in pairs], jnp.int32)
    vec = pl.BlockSpec((1, hd), lambda b, h, p, qi, ki, last: (0, 0))
    grid_spec = pltpu.PrefetchScalarGridSpec(
        num_scalar_prefetch=3,
        grid=(bsz, DIF_HEADS, len(pairs)),
        in_specs=[
            pl.BlockSpec((tq, hw), lambda b, h, p, qi, ki, last: (b * nq + qi[p], h)),
            pl.BlockSpec((tk, hw), lambda b, h, p, qi, ki, last: (b * nk + ki[p], h)),
            pl.BlockSpec((tk, hw), lambda b, h, p, qi, ki, last: (b * nk + ki[p], h)),
            vec, vec, vec, vec,
            pl.BlockSpec((1, hw), lambda b, h, p, qi, ki, last: (0, 0)),
        ],
        out_specs=pl.BlockSpec((tq, hw), lambda b, h, p, qi, ki, last: (b * nq + qi[p], h)),
        scratch_shapes=[pltpu.VMEM((2, tq, 1), F32), pltpu.VMEM((2, tq, 1), F32), pltpu.VMEM((2, tq, hw), F32)],
    )
    return pl.pallas_call(
        functools.partial(_attn_body, tq=tq, tk=tk, hd=hd),
        grid_spec=grid_spec,
        out_shape=jax.ShapeDtypeStruct((n, d), BF16),
        compiler_params=_cparams(("parallel", "parallel", "arbitrary")),
        name="diff_attention",
    )(qi, ki, last, q, k, v, lq1, lk1, lq2, lk2, subln_g)


def diff_layer(h, hb, positions, bsz, seq, w_in, lq1, lk1, lq2, lk2, subln_g, w_out, ln_g, ln_b, alpha):
    d = h.shape[1]
    hd = d // DIF_HEADS // 2
    rope = hd // 4
    assert hd == LANES and rope == 2 * ROPE_HALF
    inv_freq = ROPE_THETA ** (-jnp.arange(0, rope, 2, dtype=F32) / rope)
    freq = jnp.concatenate([inv_freq, inv_freq, jnp.zeros((LANES - rope,), F32)]).reshape(1, LANES)
    cos, sin = rope_tables(positions.reshape(bsz * seq, 1), freq)
    w_in = w_in.astype(BF16)
    rope_mm = lambda w, scale, name: fused_matmul([hb], w, bands=[cos, sin], pre=_pre_id,
                                                  post=functools.partial(_post_rope, scale), out_dtypes=[BF16],
                                                  tm=PROJ_TM, tn=1024, tk=2048, name=name)[0]
    q = rope_mm(w_in[:, :d], hd ** -0.5 * math.log2(math.e), "dif_proj_q")
    k = rope_mm(w_in[:, d:2 * d], 1.0, "dif_proj_k")
    v = proj(hb, w_in[:, 2 * d:], BF16, "dif_proj_v")
    v2 = lambda t: t.reshape(1, -1)
    o = diff_attention(q, k, v, v2(lq1), v2(lk1), v2(lq2), v2(lk2), v2(subln_g), bsz, seq)
    return out_proj_ln(o, w_out.astype(BF16), h, ln_g, ln_b, alpha)


def _post_gelu(acc):
    c = math.sqrt(2.0 / math.pi)
    return (0.5 * acc * (1.0 + jnp.tanh(c * (acc + 0.044715 * acc * acc * acc))),)


def _lru_body(u_ref, halo_ref, gate_ref, cw_ref, cb_ref, wa_ref, ba_ref, wx_ref, bx_ref, lam_ref,
              out_ref, h_ref, a_ref, b_ref, *, tq, width):
    s = pl.program_id(1)

    @pl.when(s == 0)
    def _():
        h_ref[...] = jnp.zeros_like(h_ref)

    halo = jnp.where(s == 0, 0.0, halo_ref[...])
    u = _causal_conv4_ref(u_ref, halo, cw_ref[...], cb_ref[...])

    nsp = _softplus(-lam_ref[...])
    blk = width // LRU_BLOCKS
    for j in range(LRU_BLOCKS):
        sl = slice(j * blk, (j + 1) * blk)
        uj = u[:, sl]
        ub = uj.astype(BF16)
        r = jax.nn.sigmoid(_dot(ub, wa_ref[j]) + ba_ref[:, sl])
        i = jax.nn.sigmoid(_dot(ub, wx_ref[j]) + bx_ref[:, sl])
        log_a = -LRU_C * r * nsp[:, sl]
        a_ref[:, sl] = jnp.exp(log_a)
        b_ref[:, sl] = jnp.sqrt(1.0 - jnp.exp(2.0 * log_a)) * (i * uj)

    def step(t, h):
        h = a_ref[pl.ds(t, 1), :] * h + b_ref[pl.ds(t, 1), :]
        b_ref[pl.ds(t, 1), :] = h
        return h

    h_ref[0:1, :] = lax.fori_loop(0, tq, step, h_ref[0:1, :], unroll=8)
    out_ref[...] = (gate_ref[...].astype(F32) * b_ref[...]).astype(out_ref.dtype)


def lru_core(u_pre, gate, conv_w, conv_b, w_a, b_a, w_x, b_x, lam, bsz, seq, *, tq=512):
    n, width = u_pre.shape
    tq = min(tq, seq)
    nblk = seq // tq
    per = tq // SUBLANES
    row = lambda b, s: (b * nblk + s, 0)
    full = lambda arr: pl.BlockSpec(arr.shape, lambda b, s: (0,) * arr.ndim)
    return pl.pallas_call(
        functools.partial(_lru_body, tq=tq, width=width),
        grid=(bsz, nblk),
        in_specs=[
            pl.BlockSpec((tq, width), row),
            pl.BlockSpec((SUBLANES, width), lambda b, s: (jnp.maximum((b * nblk + s) * per - 1, 0), 0)),
            pl.BlockSpec((tq, width), row),
            full(conv_w), full(conv_b), full(w_a), full(b_a), full(w_x), full(b_x), full(lam),
        ],
        out_specs=pl.BlockSpec((tq, width), row),
        out_shape=jax.ShapeDtypeStruct((n, width), BF16),
        scratch_shapes=[pltpu.VMEM((SUBLANES, width), F32), pltpu.VMEM((tq, width), F32),
                        pltpu.VMEM((tq, width), F32)],
        compiler_params=_cparams(("arbitrary", "arbitrary")),
        name="lru_core",
    )(u_pre, u_pre, gate, conv_w, conv_b, w_a, b_a, w_x, b_x, lam)


def rglru_layer(h, hb, bsz, seq, w_in, conv_w, conv_b, w_a, b_a, w_x, b_x, lam, w_out, ln_g, ln_b, alpha):
    width = w_in.shape[1] // 2
    w_in = w_in.astype(BF16)
    (gate,) = fused_matmul([hb], w_in[:, :width], pre=_pre_id, post=_post_gelu, out_dtypes=[BF16],
                           tm=PROJ_TM, tn=1024, tk=2048, name="lru_proj_gate")
    u_pre = proj(hb, w_in[:, width:], F32, "lru_proj_u")
    z = lru_core(u_pre, gate, conv_w, conv_b.reshape(1, -1), w_a.astype(BF16), b_a.reshape(1, -1),
                 w_x.astype(BF16), b_x.reshape(1, -1), lam.reshape(1, -1), bsz, seq)
    return out_proj_ln(z, w_out.astype(BF16), h, ln_g, ln_b, alpha)


def kernel(x, positions, rwkv_mu, rwkv_w_in, rwkv_w0, rwkv_w1, rwkv_w2, rwkv_a0, rwkv_a1, rwkv_a2, rwkv_g1, rwkv_g2, rwkv_k_k, rwkv_k_a, rwkv_r_k, rwkv_ln_g, rwkv_ln_b, rwkv_w_out, ssd_w_in, ssd_conv_w, ssd_conv_b, ssd_dt_bias, ssd_a_log, ssd_d, ssd_norm_g, ssd_w_out, dif_w_in, dif_lq1, dif_lk1, dif_lq2, dif_lk2, dif_subln_g, dif_w_out, lru_w_in, lru_conv_w, lru_conv_b, lru_w_a, lru_b_a, lru_w_x, lru_b_x, lru_lam, lru_w_out, ffn_w_in, ffn_w_out, ln_g, ln_b):
    bsz, seq, d = x.shape
    depth = ffn_w_in.shape[0]
    alpha = (2 * depth) ** 0.25
    h = x.reshape(bsz * seq, d)
    hb = None
    for i in range(depth):
        m = i % 4
        g0, b0 = ln_g[i, 0].reshape(1, d), ln_b[i, 0].reshape(1, d)
        g1, b1 = ln_g[i, 1].reshape(1, d), ln_b[i, 1].reshape(1, d)
        if m == 0:
            h, hb = rwkv_layer(h, bsz, seq, rwkv_mu, rwkv_w_in, rwkv_w0, rwkv_w1, rwkv_w2, rwkv_a0, rwkv_a1,
                               rwkv_a2, rwkv_g1, rwkv_g2, rwkv_k_k, rwkv_k_a, rwkv_r_k, rwkv_ln_g, rwkv_ln_b,
                               rwkv_w_out, g0, b0, alpha)
        elif m == 1:
            h, hb = ssd_layer(h, hb, bsz, seq, ssd_w_in, ssd_conv_w, ssd_conv_b, ssd_dt_bias, ssd_a_log, ssd_d,
                              ssd_norm_g, ssd_w_out, g0, b0, alpha)
        elif m == 2:
            h, hb = diff_layer(h, hb, positions, bsz, seq, dif_w_in, dif_lq1, dif_lk1, dif_lq2, dif_lk2,
                               dif_subln_g, dif_w_out, g0, b0, alpha)
        else:
            h, hb = rglru_layer(h, hb, bsz, seq, lru_w_in, lru_conv_w, lru_conv_b, lru_w_a, lru_b_a, lru_w_x,
                                lru_b_x, lru_lam, lru_w_out, g0, b0, alpha)
        h, hb = ffn_ln(h, hb, _tile_cols(ffn_w_in[i].astype(BF16), FFN_TF), ffn_w_out[i].astype(BF16), g1, b1,
                       alpha)
    return h.reshape(bsz, seq, d)
```

```python
import functools
import math

import numpy as np
import jax
import jax.numpy as jnp
from jax import lax
from jax.experimental import pallas as pl
from jax.experimental.pallas import tpu as pltpu

F32 = jnp.float32
BF16 = jnp.bfloat16

LANES = 128
SUBLANES = 8
BF16_ROWS = 16
VMEM_LIMIT_BYTES = 56 * 1024 * 1024

PROJ_TM = 1024
PROJ_TN = 1024
FFN_TM = 512
FFN_TF = 512
OUT_PROJ_TM = 512
OUT_PROJ_TK = 2048
OUT_PROJ_SUB = 128
RWKV_PROJ_TM = 512
RWKV_LORA_TM = 256
RW_PAIRS_PER_STEP = 16
SSD_TQ = 256
LRU_TQ = 512
ROPE_TM = 1024
ATTN_TQ = 1024
ATTN_TK = 1024
ATTN_SUB_ROWS = 256

LN_EPS = 1e-5
CHUNK = 64
CHUNK_SHIFT = 6
RW_HEAD = 64
RW_GN_EPS = 64e-5
SSD_HEAD = 64
SSD_GROUPS = 8
SSD_STATE = 128
DIF_HEADS = 8
ROPE_THETA = 500000.0
DIF_LAYER = 2
DIF_LAMBDA_INIT = 0.8 - 0.6 * math.exp(-0.3 * DIF_LAYER)
LRU_BLOCKS = 16
LRU_C = 8.0
NEG_BIG = -1e30


def _cparams(sem):
    return pltpu.CompilerParams(dimension_semantics=sem, vmem_limit_bytes=VMEM_LIMIT_BYTES)


def _dot(a, b):
    return jnp.dot(a, b, preferred_element_type=F32)


def _dot_nt(a, b):
    return lax.dot_general(a, b, (((1,), (1,)), ((), ())), preferred_element_type=F32)


def _dot_tn(a, b):
    return lax.dot_general(a, b, (((0,), (0,)), ((), ())), preferred_element_type=F32)


def _dot_01(m01, x):
    hi = x.astype(BF16)
    r1 = x - hi.astype(F32)
    mid = r1.astype(BF16)
    low = (r1 - mid.astype(F32)).astype(BF16)
    return _dot(m01, hi) + _dot(m01, mid) + _dot(m01, low)


def _half_sums(lo_mask, x):
    s_lo = jnp.sum(jnp.where(lo_mask, x, 0.0), axis=-1, keepdims=True)
    s_hi = jnp.sum(jnp.where(lo_mask, 0.0, x), axis=-1, keepdims=True)
    return jnp.where(lo_mask, s_lo, s_hi)


def _layer_norm(t, g, b):
    mu = jnp.mean(t, -1, keepdims=True)
    d = t - mu
    var = jnp.mean(d * d, -1, keepdims=True)
    return d * lax.rsqrt(var + LN_EPS) * g + b


def _neg_softplus_neg(z):
    return jnp.minimum(z, 0.0) - jnp.log(1.0 + jnp.exp(-jnp.abs(z)))


def _softplus(z):
    return jnp.maximum(z, 0.0) + jnp.log(1.0 + jnp.exp(-jnp.abs(z)))


def _silu(z):
    return z * jax.nn.sigmoid(z)


def _shifted(x, halo, d):
    head = pltpu.roll(jnp.concatenate([halo, x[:SUBLANES]], axis=0), d, axis=0)[SUBLANES:]
    if x.shape[0] == SUBLANES:
        return head
    return jnp.concatenate([head, pltpu.roll(x, d, axis=0)[SUBLANES:]], axis=0)


def _causal_conv4(x, halo, cw, cb):
    out = x * cw[3:4, :] + cb
    for d in (1, 2, 3):
        out = out + _shifted(x, halo, d) * cw[3 - d:4 - d, :]
    return out


def _proj_body(*refs, n_bands, post):
    a, w = refs[0], refs[1]
    bands = refs[2:2 + n_bands]
    out = refs[2 + n_bands]
    out[...] = post(_dot(a[...], w[...]), *[r[...] for r in bands]).astype(out.dtype)


def proj(hb, w, out_dtype, name, *, post=lambda acc: acc, bands=()):
    m, kdim = hb.shape
    n = w.shape[1]
    tm, tn = min(PROJ_TM, m), min(PROJ_TN, n)
    assert m % tm == 0 and n % tn == 0
    in_specs = [pl.BlockSpec((tm, kdim), lambda i, j: (i, 0)), pl.BlockSpec((kdim, tn), lambda i, j: (0, j))]
    in_specs += [pl.BlockSpec((tm, b.shape[1]), lambda i, j: (i, 0)) for b in bands]
    return pl.pallas_call(
        functools.partial(_proj_body, n_bands=len(bands), post=post),
        grid=(m // tm, n // tn),
        in_specs=in_specs,
        out_specs=pl.BlockSpec((tm, tn), lambda i, j: (i, j)),
        out_shape=jax.ShapeDtypeStruct((m, n), out_dtype),
        compiler_params=_cparams(("parallel", "parallel")),
        name=name,
    )(hb, w, *bands)


def _out_proj_body(z_ref, w_ref, h_ref, g_ref, b_ref, out_ref, outb_ref, *scratch, nk, alpha, tm):
    def finish():
        for r in range(tm // OUT_PROJ_SUB):
            rows = slice(r * OUT_PROJ_SUB, (r + 1) * OUT_PROJ_SUB)
            t = _dot(z_ref[rows, :], w_ref[...])
            if nk > 1:
                t = t + scratch[0][rows, :]
            o = _layer_norm(alpha * h_ref[rows, :] + t, g_ref[...], b_ref[...])
            out_ref[rows, :] = o
            outb_ref[rows, :] = o.astype(BF16)

    if nk == 1:
        finish()
    else:
        acc = scratch[0]
        k = pl.program_id(1)

        @pl.when(k == 0)
        def _():
            acc[...] = _dot(z_ref[...], w_ref[...])

        @pl.when(jnp.logical_and(k > 0, k < nk - 1))
        def _():
            acc[...] += _dot(z_ref[...], w_ref[...])

        pl.when(k == nk - 1)(finish)


def out_proj_ln(z, w, h, g, b, alpha, *, tm=OUT_PROJ_TM, tk=OUT_PROJ_TK):
    m, kdim = z.shape
    n = w.shape[1]
    tm = min(tm, m)
    nk = kdim // tk
    assert kdim % tk == 0 and m % tm == 0 and tm % OUT_PROJ_SUB == 0
    row = pl.BlockSpec((tm, n), lambda i, k: (i, 0))
    vec = pl.BlockSpec((1, n), lambda i, k: (0, 0))
    return pl.pallas_call(
        functools.partial(_out_proj_body, nk=nk, alpha=alpha, tm=tm),
        grid=(m // tm, nk),
        in_specs=[pl.BlockSpec((tm, tk), lambda i, k: (i, k)), pl.BlockSpec((tk, n), lambda i, k: (k, 0)),
                  row, vec, vec],
        out_specs=[row, row],
        out_shape=[jax.ShapeDtypeStruct((m, n), F32), jax.ShapeDtypeStruct((m, n), BF16)],
        scratch_shapes=[pltpu.VMEM((tm, n), F32)] if nk > 1 else [],
        compiler_params=_cparams(("parallel", "arbitrary")),
        name="out_proj_ln",
    )(z, w, h, g, b)


def _ffn_body(x_ref, xb_ref, wg_ref, wu_ref, wo_ref, g_ref, b_ref, out_ref, outb_ref, acc_ref, *, nf, alpha):
    f = pl.program_id(1)

    @pl.when(f == 0)
    def _():
        acc_ref[...] = jnp.zeros_like(acc_ref)

    xb = xb_ref[...]
    gate = _dot(xb, wg_ref[...])
    up = _dot(xb, wu_ref[...])
    act = (_silu(gate) * up).astype(BF16)
    acc_ref[...] += _dot(act, wo_ref[...])

    @pl.when(f == nf - 1)
    def _():
        out = _layer_norm(alpha * x_ref[...] + acc_ref[...], g_ref[...], b_ref[...])
        out_ref[...] = out
        outb_ref[...] = out.astype(BF16)


def ffn_ln(x, xb, w_in, w_out, g, b, alpha, *, tm=FFN_TM, tf=FFN_TF):
    m, d = x.shape
    dff = w_out.shape[0]
    tm = min(tm, m)
    nf = dff // tf
    assert dff % tf == 0 and m % tm == 0
    row = pl.BlockSpec((tm, d), lambda i, f: (i, 0))
    return pl.pallas_call(
        functools.partial(_ffn_body, nf=nf, alpha=alpha),
        grid=(m // tm, nf),
        in_specs=[
            row, row,
            pl.BlockSpec((d, tf), lambda i, f: (0, f)),
            pl.BlockSpec((d, tf), lambda i, f: (0, nf + f)),
            pl.BlockSpec((tf, d), lambda i, f: (f, 0)),
            pl.BlockSpec((1, d), lambda i, f: (0, 0)),
            pl.BlockSpec((1, d), lambda i, f: (0, 0)),
        ],
        out_specs=[row, row],
        out_shape=[jax.ShapeDtypeStruct((m, d), F32), jax.ShapeDtypeStruct((m, d), BF16)],
        scratch_shapes=[pltpu.VMEM((tm, d), F32)],
        compiler_params=_cparams(("parallel", "arbitrary")),
        name="ffn_ln",
    )(x, xb, w_in, w_in, w_out, g, b)


def _rwkv_proj_body(x_ref, halo_ref, mu_ref, w_ref, out_ref, *, tm, seq):
    i = pl.program_id(0)
    x = x_ref[...]
    halo = jnp.where((i * tm) % seq == 0, 0.0, halo_ref[...])
    xx = _shifted(x, halo, 1) - x
    lhs = (x + xx * mu_ref[...]).astype(BF16)
    out_ref[...] = _dot(lhs, w_ref[...]).astype(out_ref.dtype)


def rwkv_proj(x, mu3, w3, seq, *, tm=RWKV_PROJ_TM):
    m, d = x.shape
    n = w3.shape[2]
    tm = min(tm, seq)
    per = tm // SUBLANES
    return pl.pallas_call(
        functools.partial(_rwkv_proj_body, tm=tm, seq=seq),
        grid=(m // tm, 3),
        in_specs=[
            pl.BlockSpec((tm, d), lambda i, j: (i, 0)),
            pl.BlockSpec((SUBLANES, d), lambda i, j: (jnp.maximum(i * per - 1, 0), 0)),
            pl.BlockSpec((None, 1, d), lambda i, j: (j, 0, 0)),
            pl.BlockSpec((None, d, n), lambda i, j: (j, 0, 0)),
        ],
        out_specs=pl.BlockSpec((None, tm, n), lambda i, j: (j, i, 0)),
        out_shape=jax.ShapeDtypeStruct((3, m, n), BF16),
        compiler_params=_cparams(("parallel", "arbitrary")),
        name="rwkv_proj",
    )(x, x, mu3, w3)


def _rwkv_lora_body(x_ref, halo_ref, mu_ref, w1_ref, w2_ref, a1_ref, a2_ref, g1_ref, g2_ref, w0_ref, a0_ref,
                    wl_ref, a_ref, g_ref, *, tm, seq):
    i = pl.program_id(0)
    x = x_ref[...]
    halo = jnp.where((i * tm) % seq == 0, 0.0, halo_ref[...])
    xx = _shifted(x, halo, 1) - x
    mu = mu_ref[...]
    xw = (x + xx * mu[0:1, :]).astype(BF16)
    xa = (x + xx * mu[1:2, :]).astype(BF16)
    xg = (x + xx * mu[2:3, :]).astype(BF16)
    zw = w0_ref[...] + _dot(jnp.tanh(_dot(xw, w1_ref[...])).astype(BF16), w2_ref[...])
    wl_ref[...] = _neg_softplus_neg(zw) - 0.5
    a_ref[...] = jax.nn.sigmoid(a0_ref[...] + _dot(_dot(xa, a1_ref[...]).astype(BF16), a2_ref[...]))
    g_ref[...] = _dot(jax.nn.sigmoid(_dot(xg, g1_ref[...])).astype(BF16), g2_ref[...]).astype(g_ref.dtype)


def rwkv_lora(x, mu3, w1, w2, a1, a2, g1, g2, w0, a0, seq, *, tm=RWKV_LORA_TM):
    m, d = x.shape
    tm = min(tm, seq)
    per = tm // SUBLANES
    full = lambda arr: pl.BlockSpec(arr.shape, lambda i: (0,) * arr.ndim)
    row = pl.BlockSpec((tm, d), lambda i: (i, 0))
    return pl.pallas_call(
        functools.partial(_rwkv_lora_body, tm=tm, seq=seq),
        grid=(m // tm,),
        in_specs=[row, pl.BlockSpec((SUBLANES, d), lambda i: (jnp.maximum(i * per - 1, 0), 0)),
                  full(mu3), full(w1), full(w2), full(a1), full(a2), full(g1), full(g2), full(w0), full(a0)],
        out_specs=[row, row, row],
        out_shape=[jax.ShapeDtypeStruct((m, d), F32), jax.ShapeDtypeStruct((m, d), F32),
                   jax.ShapeDtypeStruct((m, d), BF16)],
        compiler_params=_cparams(("parallel",)),
        name="rwkv_lora",
    )(x, x, mu3, w1, w2, a1, a2, g1, g2, w0, a0)


RW_LEVELS = 6


def _rwkv_masks():
    n = 2 * CHUNK
    t = np.arange(n)[:, None]
    s = np.arange(n)[None, :]
    strict = (t > s).astype(np.float32)
    incl = (t >= s).astype(np.float32)
    tri = np.stack([strict, incl])
    lvl = []
    for lv in range(RW_LEVELS):
        sz = 1 << lv
        lvl.append(((t // (2 * sz) == s // (2 * sz)) & ((t // sz) % 2 == 1) & ((s // sz) % 2 == 0)).astype(np.float32))
    t64 = incl[:CHUNK, :CHUNK]
    return jnp.asarray(tri), jnp.asarray(np.stack(lvl)), jnp.asarray(t64)


def _rwkv_core_body(r_ref, k_ref, v_ref, wl_ref, a_ref, g_ref, kk_ref, ka_ref, rk_ref, lng_ref, lnb_ref,
                    tri_ref, lvl_ref, t64_ref, out_ref, st_ref):
    c = pl.program_id(2)

    @pl.when(c == 0)
    def _():
        st_ref[...] = jnp.zeros_like(st_ref)

    strict, incl = tri_ref[0], tri_ref[1]
    eye = incl - strict
    t64 = t64_ref[...].astype(BF16)
    lo = lax.broadcasted_iota(jnp.int32, (1, LANES), 1) < RW_HEAD
    inv_head = 1.0 / RW_HEAD

    def stack(x):
        return jnp.concatenate([jnp.where(lo, x, 0.0), jnp.where(lo, 0.0, x)], axis=0).astype(BF16)

    pairs = range(RW_PAIRS_PER_STEP)
    sls = [slice(q * LANES, (q + 1) * LANES) for q in pairs]

    ld = lambda ref, sl: ref[:, sl].astype(F32)
    kk_sq = []
    for sl in sls:
        kk = ld(k_ref, sl) * kk_ref[:, sl]
        kk_sq.append(_half_sums(lo, kk * kk))
    cums = [_dot_01(t64, -jnp.exp(wl_ref[:, sl])) for sl in sls]

    ops = []
    for sl, ss, cum in zip(sls, kk_sq, cums):
        k, a = ld(k_ref, sl), a_ref[:, sl]
        kk = k * kk_ref[:, sl] / jnp.maximum(jnp.sqrt(ss), 1e-12)
        k2 = k * (1.0 + (a - 1.0) * ka_ref[:, sl])
        lw = -jnp.exp(wl_ref[:, sl])
        cl = cum[CHUNK - 1:CHUNK, :]
        e_neg = jnp.exp(-cum)
        e_end = jnp.exp(cl - cum)
        kb = kk * a
        ops.append(dict(
            a_s=stack(-kk * jnp.exp(cum - lw)),
            r_s=stack(ld(r_ref, sl) * jnp.exp(cum)),
            b_s=stack(kb * e_neg), k_s=stack(k2 * e_neg), v_s=stack(ld(v_ref, sl)),
            bh_s=stack(kb * e_end), kh_s=stack(k2 * e_end),
            w_end=jnp.exp(cl), k2=k2))

    n2 = 2 * CHUNK
    quads = [(2 * i, 2 * i + 1) for i in range(RW_PAIRS_PER_STEP // 2)]
    zero = jnp.zeros((n2, n2), BF16)
    rows2 = lambda top, bot: jnp.concatenate([top, bot], axis=0)
    side = lambda xa, xb: jnp.concatenate([xa, xb], axis=1)
    bdiag = lambda xa, xb: rows2(side(xa, zero), side(zero, xb))
    half = lambda x, j: x[:, j * n2:(j + 1) * n2]
    col = lambda key, q: side(ops[q[0]][key], ops[q[1]][key])

    gmask = rows2(side(strict, strict), side(incl, incl))
    gram = [gmask * _dot_nt(rows2(o["a_s"], o["r_s"]), rows2(o["b_s"], o["k_s"])) for o in ops]
    a_ab = [g[:n2, :n2] for g in gram]
    a_ak = [g[:n2, n2:].astype(BF16) for g in gram]
    r_b = [g[n2:, :n2].astype(BF16) for g in gram]
    r_k = [g[n2:, n2:].astype(BF16) for g in gram]

    t_inv = [eye + m * lvl_ref[0] for m in a_ab]
    for lv in range(1, RW_LEVELS):
        tb = [t.astype(BF16) for t in t_inv]
        off = [(m * lvl_ref[lv]).astype(BF16) for m in a_ab]
        xs = [_dot(side(tb[a], tb[b]), bdiag(off[a], off[b])).astype(BF16) for a, b in quads]
        upd = [_dot(x, bdiag(tb[a], tb[b])) for x, (a, b) in zip(xs, quads)]
        t_inv = [t + half(upd[p // 2], p % 2) for p, t in enumerate(t_inv)]
    tb = [t.astype(BF16) for t in t_inv]

    s_old = [st_ref[q] for q in pairs]
    s_b = [s.astype(BF16) for s in s_old]
    p12 = [_dot_nt(rows2(col("a_s", q), col("r_s", q)), bdiag(s_b[q[0]], s_b[q[1]]))
           + _dot(rows2(side(a_ak[q[0]], a_ak[q[1]]), side(r_k[q[0]], r_k[q[1]])), bdiag(ops[q[0]]["v_s"], ops[q[1]]["v_s"]))
           for q in quads]
    rhs = [p[:n2].astype(BF16) for p in p12]
    u_b = [_dot(side(tb[a], tb[b]), bdiag(half(x, 0), half(x, 1))).astype(BF16) for x, (a, b) in zip(rhs, quads)]
    y_q = [p[n2:] + _dot(side(r_b[a], r_b[b]), bdiag(half(u, 0), half(u, 1))) for p, u, (a, b) in zip(p12, u_b, quads)]
    inc = [_dot_tn(rows2(u, col("v_s", q)), rows2(col("bh_s", q), col("kh_s", q))) for u, q in zip(u_b, quads)]
    for z, (a, b) in zip(inc, quads):
        st_ref[a] = s_old[a] * ops[a]["w_end"] + z[:n2, :n2]
        st_ref[b] = s_old[b] * ops[b]["w_end"] + z[n2:, n2:]

    y_s = [half(y_q[p // 2], p % 2) for p in pairs]
    ys = [y[:CHUNK] + y[CHUNK:] for y in y_s]
    mus = [_half_sums(lo, y) * inv_head for y in ys]
    ds = [y - mu for y, mu in zip(ys, mus)]
    var = [_half_sums(lo, d * d) * inv_head for d in ds]
    bonus = [_half_sums(lo, ld(r_ref, sl) * o["k2"] * rk_ref[:, sl]) for sl, o in zip(sls, ops)]
    for sl, d, vr, bn in zip(sls, ds, var, bonus):
        yn = d * lax.rsqrt(vr + RW_GN_EPS) * lng_ref[:, sl] + lnb_ref[:, sl]
        out_ref[:, sl] = ((yn + bn * ld(v_ref, sl)) * ld(g_ref, sl)).astype(out_ref.dtype)


def rwkv_core(rkv, wl, a, g, k_k, k_a, r_k, ln_g, ln_b, bsz, seq):
    _, m, d = rkv.shape
    width = RW_PAIRS_PER_STEP * LANES
    nc = seq // CHUNK
    tri, lvl, t64 = _rwkv_masks()
    row = lambda b, p, c: (b * nc + c, p)
    rkv_spec = lambda j: pl.BlockSpec((None, CHUNK, width), lambda b, p, c: (j, b * nc + c, p))
    act = pl.BlockSpec((CHUNK, width), row)
    par = pl.BlockSpec((1, width), lambda b, p, c: (0, p))
    full = lambda arr: pl.BlockSpec(arr.shape, lambda b, p, c: (0,) * arr.ndim)
    return pl.pallas_call(
        _rwkv_core_body,
        grid=(bsz, d // width, nc),
        in_specs=[rkv_spec(0), rkv_spec(1), rkv_spec(2), act, act, act, par, par, par, par, par,
                  full(tri), full(lvl), full(t64)],
        out_specs=act,
        out_shape=jax.ShapeDtypeStruct((m, d), BF16),
        scratch_shapes=[pltpu.VMEM((RW_PAIRS_PER_STEP, LANES, LANES), F32)],
        compiler_params=_cparams(("parallel", "parallel", "arbitrary")),
        name="rwkv_core",
    )(rkv, rkv, rkv, wl, a, g, k_k, k_a, r_k, ln_g, ln_b, tri, lvl, t64)


def _pad_to(x, axis, size):
    pad = [(0, 0)] * x.ndim
    pad[axis] = (0, size - x.shape[axis])
    return jnp.pad(x, pad)


def rwkv_layer(h, bsz, seq, mu, w_in, w0, w1, w2, a0, a1, a2, g1, g2, k_k, k_a, r_k, gn_g, gn_b, w_out,
               ln_g, ln_b, alpha):
    d = h.shape[1]
    v2 = lambda t: t.reshape(1, d)
    mu_rkv = jnp.stack([mu[0], mu[2], mu[3]]).reshape(3, 1, d)
    mu_lora = jnp.stack([mu[1], mu[4], mu[5]])
    rkv = rwkv_proj(h, mu_rkv, w_in.astype(BF16), seq)
    wl, a, g = rwkv_lora(
        h, mu_lora,
        _pad_to(w1, 1, LANES).astype(BF16), _pad_to(w2, 0, LANES).astype(BF16),
        _pad_to(a1, 1, LANES).astype(BF16), _pad_to(a2, 0, LANES).astype(BF16),
        g1.astype(BF16), g2.astype(BF16), v2(w0), v2(a0), seq)
    z = rwkv_core(rkv, wl, a, g, v2(k_k), v2(k_a), v2(r_k), v2(gn_g), v2(gn_b), bsz, seq)
    return out_proj_ln(z, w_out.astype(BF16), h, ln_g, ln_b, alpha)


def _ssd_consts(heads):
    lanes = heads * SSD_HEAD
    head_of = np.arange(lanes) // SSD_HEAD
    pos_of = np.arange(lanes) % SSD_HEAD
    expand = (np.arange(LANES)[:, None] == head_of[None, :]).astype(np.float32)
    eye = (np.arange(CHUNK)[:, None] == pos_of[None, :]).astype(np.float32)
    tril = (np.arange(CHUNK)[:, None] >= pos_of[None, :]).astype(np.float32)
    return jnp.asarray(expand, BF16), jnp.asarray(eye), jnp.asarray(tril)


def _expand_heads(x, e01):
    hi = x.astype(BF16)
    low = (x - hi.astype(F32)).astype(BF16)
    return _dot(hi, e01) + _dot(low, e01)


def _ssd_body(xbc_ref, halo_ref, z_ref, dtp_ref, cw_ref, cb_ref, dtb_ref, alog_ref, dsk_ref, ng_ref, t64_ref,
              e01_ref, eye_ref, tril_ref, out_ref, st_ref, xc_ref, dt_ref, da_ref, *, tq, d_inner):
    s = pl.program_id(1)

    @pl.when(s == 0)
    def _():
        st_ref[...] = jnp.zeros_like(st_ref)

    halo = jnp.where(s == 0, 0.0, halo_ref[...].astype(F32)[BF16_ROWS - SUBLANES:, :])
    xc_ref[...] = _silu(_causal_conv4(xbc_ref[...].astype(F32), halo, cw_ref[...], cb_ref[...]))
    dt_all = _softplus(dtp_ref[...] + dtb_ref[...])
    dt_ref[...] = dt_all
    da_ref[...] = dt_all * (-jnp.exp(alog_ref[...]))

    t64b = t64_ref[...].astype(BF16)
    lo = lax.broadcasted_iota(jnp.int32, (1, LANES), 1) < SSD_HEAD
    gn = SSD_GROUPS * SSD_STATE
    gw = d_inner // SSD_GROUPS
    pairs_per_group = gw // LANES

    def chunk(ci, carry):
        r0 = pl.multiple_of(ci * CHUNK, CHUNK)
        rows = pl.ds(r0, CHUNK)
        e01 = e01_ref[...]
        cs = _dot_01(t64b, da_ref[rows, :])
        dte = _expand_heads(dt_ref[rows, :], e01)
        cse = _expand_heads(cs, e01)
        cs_row = jnp.sum(eye_ref[...] * cse, axis=0, keepdims=True)
        decay = jnp.exp(jnp.minimum(cse - cs_row, 0.0)) * tril_ref[...]
        cs_last = cse[CHUNK - 1:CHUNK, :]
        ecs = jnp.exp(cse)
        to_end = jnp.exp(cs_last - cse)
        e_last = jnp.exp(cs_last)
        for g in range(SSD_GROUPS):
            gsl = slice(g * gw, (g + 1) * gw)
            bg = xc_ref[rows, d_inner + g * SSD_STATE:d_inner + (g + 1) * SSD_STATE]
            cg = xc_ref[rows, d_inner + gn + g * SSD_STATE:d_inner + gn + (g + 1) * SSD_STATE].astype(BF16)
            bgb = bg.astype(BF16)
            cb_rep = _dot_nt(cg, jnp.concatenate([bgb] * (gw // SSD_HEAD), axis=0))
            mat = (cb_rep * decay[:, gsl]).astype(BF16)
            bg_t = bg.T.astype(BF16)
            ps = [g * pairs_per_group + q for q in range(pairs_per_group)]
            sls = [slice(p * LANES, (p + 1) * LANES) for p in ps]
            xs = [xc_ref[rows, sl] for sl in sls]
            xdts = [x * dte[:, sl] for x, sl in zip(xs, sls)]
            x_st = [jnp.concatenate([jnp.where(lo, xdt, 0.0), jnp.where(lo, 0.0, xdt)], axis=0).astype(BF16)
                    for xdt in xdts]
            x_end = [(xdt * to_end[:, sl]).astype(BF16) for xdt, sl in zip(xdts, sls)]
            sts = [st_ref[p] for p in ps]
            intra = [_dot(mat[:, q * LANES:(q + 1) * LANES], xs_) for q, xs_ in enumerate(x_st)]
            inter = [_dot(cg, st.astype(BF16)) for st in sts]
            upd = [_dot(bg_t, xe) for xe in x_end]
            for p, sl, st, u in zip(ps, sls, sts, upd):
                st_ref[p] = st * e_last[:, sl] + u
            ys = []
            ssq = jnp.zeros((CHUNK, 1), F32)
            for sl, x, ya, yb in zip(sls, xs, intra, inter):
                y = ya + yb * ecs[:, sl] + x * dsk_ref[:, sl]
                y = y * _silu(z_ref[rows, sl].astype(F32))
                ssq = ssq + jnp.sum(y * y, axis=-1, keepdims=True)
                ys.append(y)
            inv = lax.rsqrt(ssq * (1.0 / gw) + 1e-5)
            for sl, y in zip(sls, ys):
                out_ref[rows, sl] = (y * inv * ng_ref[:, sl]).astype(out_ref.dtype)
        return carry

    lax.fori_loop(0, tq // CHUNK, chunk, 0)


def ssd_core(xbc_pre, z, dt_pre, conv_w, conv_b, dt_bias, a_log, d_skip, norm_g, bsz, seq, *, tq=SSD_TQ):
    n, conv_dim = xbc_pre.shape
    d_inner = z.shape[1]
    tq = min(tq, seq)
    nblk = seq // tq
    per = tq // BF16_ROWS
    npairs = d_inner // LANES
    t64 = _rwkv_masks()[2]
    e01, eye, tril = _ssd_consts(d_inner // SSD_HEAD)
    row = lambda b, s: (b * nblk + s, 0)
    full = lambda arr: pl.BlockSpec(arr.shape, lambda b, s: (0,) * arr.ndim)
    return pl.pallas_call(
        functools.partial(_ssd_body, tq=tq, d_inner=d_inner),
        grid=(bsz, nblk),
        in_specs=[
            pl.BlockSpec((tq, conv_dim), row),
            pl.BlockSpec((BF16_ROWS, conv_dim), lambda b, s: (jnp.maximum((b * nblk + s) * per - 1, 0), 0)),
            pl.BlockSpec((tq, d_inner), row),
            pl.BlockSpec((tq, LANES), row),
            full(conv_w), full(conv_b), full(dt_bias), full(a_log), full(d_skip), full(norm_g), full(t64),
            full(e01), full(eye), full(tril),
        ],
        out_specs=pl.BlockSpec((tq, d_inner), row),
        out_shape=jax.ShapeDtypeStruct((n, d_inner), BF16),
        scratch_shapes=[pltpu.VMEM((npairs, SSD_STATE, LANES), F32), pltpu.VMEM((tq, conv_dim), F32),
                        pltpu.VMEM((tq, LANES), F32), pltpu.VMEM((tq, LANES), F32)],
        compiler_params=_cparams(("arbitrary", "arbitrary")),
        name="ssd_core",
    )(xbc_pre, xbc_pre, z, dt_pre, conv_w, conv_b, dt_bias, a_log, d_skip, norm_g, t64, e01, eye, tril)


def ssd_layer(h, hb, bsz, seq, w_in, conv_w, conv_b, dt_bias, a_log, d_skip, norm_g, w_out, ln_g, ln_b, alpha):
    d_inner = w_out.shape[0]
    conv_dim = conv_w.shape[1]
    w_in = w_in.astype(BF16)
    z = proj(hb, w_in[:, :d_inner], BF16, "ssd_proj_z")
    xbc_pre = proj(hb, w_in[:, d_inner:d_inner + conv_dim], BF16, "ssd_proj_xbc")
    dt_pre = proj(hb, _pad_to(w_in[:, d_inner + conv_dim:], 1, LANES), F32, "ssd_proj_dt")
    pad_row = lambda t: _pad_to(t.reshape(1, -1), 1, LANES)
    d_skip_lanes = jnp.repeat(d_skip, SSD_HEAD).reshape(1, -1)
    y = ssd_core(xbc_pre, z, dt_pre, conv_w, conv_b.reshape(1, -1), pad_row(dt_bias), pad_row(a_log),
                 d_skip_lanes, norm_g.reshape(1, -1), bsz, seq)
    return out_proj_ln(y, w_out.astype(BF16), h, ln_g, ln_b, alpha)


ROPE_HALF = 16


def _rope_table_body(pos_ref, freq_ref, cos_ref, sin_ref):
    ang = pos_ref[...].astype(F32) * freq_ref[...]
    lane = lax.broadcasted_iota(jnp.int32, (1, LANES), 1)
    sn = jnp.sin(ang)
    cos_ref[...] = jnp.cos(ang)
    sin_ref[...] = jnp.where(lane < ROPE_HALF, -sn, jnp.where(lane < 2 * ROPE_HALF, sn, 0.0))


def rope_tables(pos, freq, *, tm=ROPE_TM):
    n = pos.shape[0]
    tm = min(tm, n)
    band = pl.BlockSpec((tm, LANES), lambda i: (i, 0))
    return pl.pallas_call(
        _rope_table_body,
        grid=(n // tm,),
        in_specs=[pl.BlockSpec((tm, 1), lambda i: (i, 0)), pl.BlockSpec((1, LANES), lambda i: (0, 0))],
        out_specs=[band, band],
        out_shape=[jax.ShapeDtypeStruct((n, LANES), F32)] * 2,
        compiler_params=_cparams(("parallel",)),
        name="rope_tables",
    )(pos, freq)


def _post_rope(scale, acc, cos, sin):
    lane = lax.broadcasted_iota(jnp.int32, (1, LANES), 1)
    outs = []
    for j in range(acc.shape[1] // LANES):
        t = acc[:, j * LANES:(j + 1) * LANES]
        partner = jnp.where(lane < ROPE_HALF, pltpu.roll(t, LANES - ROPE_HALF, axis=1), pltpu.roll(t, ROPE_HALF, axis=1))
        outs.append((t * cos + partner * sin) * scale)
    return jnp.concatenate(outs, axis=1)


def _attn_body(qi_ref, ki_ref, last_ref, q_ref, k_ref, v_ref, lq1_ref, lk1_ref, lq2_ref, lk2_ref, sg_ref,
               out_ref, m_ref, l_ref, acc_ref, *, tq, tk, hd):
    p = pl.program_id(2)
    qi, ki = qi_ref[p], ki_ref[p]

    @pl.when(ki == 0)
    def _():
        m_ref[...] = jnp.full_like(m_ref, NEG_BIG)
        l_ref[...] = jnp.zeros_like(l_ref)
        acc_ref[...] = jnp.zeros_like(acc_ref)

    sub = min(ATTN_SUB_ROWS, tq)
    n_sub = tq // sub

    def update(masked):
        def n_keys(r):
            return (r + 1) * sub if masked and tq == tk else tk

        def qk(r):
            rows = slice(r * sub, (r + 1) * sub)
            return [_dot_nt(q_ref[rows, j * hd:(j + 1) * hd], k_ref[:n_keys(r), j * hd:(j + 1) * hd])
                    for j in range(2)]

        scores = qk(0)
        for r in range(n_sub):
            nxt = qk(r + 1) if r + 1 < n_sub else None
            rows = slice(r * sub, (r + 1) * sub)
            if masked:
                row0 = qi * tq + r * sub
                q_chunk = jnp.right_shift(row0 + lax.broadcasted_iota(jnp.int32, (sub, 1), 0), CHUNK_SHIFT)
                k_chunk = jnp.right_shift(ki * tk + lax.broadcasted_iota(jnp.int32, (1, n_keys(r)), 1), CHUNK_SHIFT)
                visible = k_chunk <= q_chunk
                scores = [jnp.where(visible, s, NEG_BIG) for s in scores]
            v = v_ref[:n_keys(r), :]
            for j in range(2):
                m_old = m_ref[j, rows, :]
                m_new = jnp.maximum(m_old, jnp.max(scores[j], axis=-1, keepdims=True))
                pe = jnp.exp2(scores[j] - m_new)
                scale = jnp.exp2(m_old - m_new)
                l_ref[j, rows, :] = scale * l_ref[j, rows, :] + jnp.sum(pe, axis=-1, keepdims=True)
                acc_ref[j, rows, :] = scale * acc_ref[j, rows, :] + _dot(pe.astype(BF16), v)
                m_ref[j, rows, :] = m_new
            scores = nxt

    needs_mask = (ki + 1) * tk > qi * tq + CHUNK

    @pl.when(needs_mask)
    def _():
        update(True)

    @pl.when(jnp.logical_not(needs_mask))
    def _():
        update(False)

    @pl.when(last_ref[p] == 1)
    def _():
        lam = (jnp.exp(jnp.sum(lq1_ref[...] * lk1_ref[...], axis=-1, keepdims=True))
               - jnp.exp(jnp.sum(lq2_ref[...] * lk2_ref[...], axis=-1, keepdims=True)) + DIF_LAMBDA_INIT)
        o = acc_ref[0] / l_ref[0] - lam * (acc_ref[1] / l_ref[1])
        o = o * lax.rsqrt(jnp.mean(o * o, axis=-1, keepdims=True) + 1e-5) * sg_ref[...]
        out_ref[...] = (o * (1.0 - DIF_LAMBDA_INIT)).astype(out_ref.dtype)


def diff_attention(q, k, v, lq1, lk1, lq2, lk2, subln_g, bsz, seq, *, tq=ATTN_TQ, tk=ATTN_TK):
    n, d = q.shape
    hw = d // DIF_HEADS
    hd = hw // 2
    tq, tk = min(tq, seq), min(tk, seq)
    assert tq % CHUNK == 0 and tk % CHUNK == 0 and CHUNK == 1 << CHUNK_SHIFT
    nq, nk = seq // tq, seq // tk
    n_kv = lambda a: -(-((a + 1) * tq) // tk)
    pairs = [(a, b) for a in range(nq) for b in range(n_kv(a))]
    qi = jnp.asarray([a for a, _ in pairs], jnp.int32)
    ki = jnp.asarray([b for _, b in pairs], jnp.int32)
    last = jnp.asarray([1 if b == n_kv(a) - 1 else 0 for a, b in pairs], jnp.int32)
    vec = pl.BlockSpec((1, hd), lambda b, h, p, qi, ki, last: (0, 0))
    grid_spec = pltpu.PrefetchScalarGridSpec(
        num_scalar_prefetch=3,
        grid=(bsz, DIF_HEADS, len(pairs)),
        in_specs=[
            pl.BlockSpec((tq, hw), lambda b, h, p, qi, ki, last: (b * nq + qi[p], h)),
            pl.BlockSpec((tk, hw), lambda b, h, p, qi, ki, last: (b * nk + ki[p], h)),
            pl.BlockSpec((tk, hw), lambda b, h, p, qi, ki, last: (b * nk + ki[p], h)),
            vec, vec, vec, vec,
            pl.BlockSpec((1, hw), lambda b, h, p, qi, ki, last: (0, 0)),
        ],
        out_specs=pl.BlockSpec((tq, hw), lambda b, h, p, qi, ki, last: (b * nq + qi[p], h)),
        scratch_shapes=[pltpu.VMEM((2, tq, 1), F32), pltpu.VMEM((2, tq, 1), F32), pltpu.VMEM((2, tq, hw), F32)],
    )
    return pl.pallas_call(
        functools.partial(_attn_body, tq=tq, tk=tk, hd=hd),
        grid_spec=grid_spec,
        out_shape=jax.ShapeDtypeStruct((n, d), BF16),
        compiler_params=_cparams(("parallel", "parallel", "arbitrary")),
        name="diff_attention",
    )(qi, ki, last, q, k, v, lq1, lk1, lq2, lk2, subln_g)


def diff_layer(h, hb, positions, bsz, seq, w_in, lq1, lk1, lq2, lk2, subln_g, w_out, ln_g, ln_b, alpha):
    d = h.shape[1]
    hd = d // DIF_HEADS // 2
    rope = hd // 4
    assert hd == LANES and rope == 2 * ROPE_HALF
    inv_freq = ROPE_THETA ** (-jnp.arange(0, rope, 2, dtype=F32) / rope)
    freq = jnp.concatenate([inv_freq, inv_freq, jnp.zeros((LANES - rope,), F32)]).reshape(1, LANES)
    cos, sin = rope_tables(positions.reshape(bsz * seq, 1), freq)
    w_in = w_in.astype(BF16)
    rope_mm = lambda w, scale, name: proj(hb, w, BF16, name, post=functools.partial(_post_rope, scale),
                                          bands=[cos, sin])
    q = rope_mm(w_in[:, :d], hd ** -0.5 * math.log2(math.e), "dif_proj_q")
    k = rope_mm(w_in[:, d:2 * d], 1.0, "dif_proj_k")
    v = proj(hb, w_in[:, 2 * d:], BF16, "dif_proj_v")
    v2 = lambda t: t.reshape(1, -1)
    o = diff_attention(q, k, v, v2(lq1), v2(lk1), v2(lq2), v2(lk2), v2(subln_g), bsz, seq)
    return out_proj_ln(o, w_out.astype(BF16), h, ln_g, ln_b, alpha)


def _post_gelu(acc):
    c = math.sqrt(2.0 / math.pi)
    return 0.5 * acc * (1.0 + jnp.tanh(c * (acc + 0.044715 * acc * acc * acc)))


def _lru_body(u_ref, halo_ref, gate_ref, cw_ref, cb_ref, wa_ref, ba_ref, wx_ref, bx_ref, lam_ref,
              out_ref, h_ref, a_ref, b_ref, *, tq, width):
    s = pl.program_id(1)

    @pl.when(s == 0)
    def _():
        h_ref[...] = jnp.zeros_like(h_ref)

    halo = jnp.where(s == 0, 0.0, halo_ref[...])
    u = _causal_conv4(u_ref[...], halo, cw_ref[...], cb_ref[...])

    nsp = _softplus(-lam_ref[...])
    blk = width // LRU_BLOCKS
    for j in range(LRU_BLOCKS):
        sl = slice(j * blk, (j + 1) * blk)
        uj = u[:, sl]
        ub = uj.astype(BF16)
        r = jax.nn.sigmoid(_dot(ub, wa_ref[j]) + ba_ref[:, sl])
        i = jax.nn.sigmoid(_dot(ub, wx_ref[j]) + bx_ref[:, sl])
        log_a = -LRU_C * r * nsp[:, sl]
        a_ref[:, sl] = jnp.exp(log_a)
        b_ref[:, sl] = jnp.sqrt(1.0 - jnp.exp(2.0 * log_a)) * (i * uj)

    def step(t, h):
        h = a_ref[pl.ds(t, 1), :] * h + b_ref[pl.ds(t, 1), :]
        b_ref[pl.ds(t, 1), :] = h
        return h

    h_ref[0:1, :] = lax.fori_loop(0, tq, step, h_ref[0:1, :], unroll=8)
    out_ref[...] = (gate_ref[...].astype(F32) * b_ref[...]).astype(out_ref.dtype)


def lru_core(u_pre, gate, conv_w, conv_b, w_a, b_a, w_x, b_x, lam, bsz, seq, *, tq=LRU_TQ):
    n, width = u_pre.shape
    tq = min(tq, seq)
    nblk = seq // tq
    per = tq // SUBLANES
    row = lambda b, s: (b * nblk + s, 0)
    full = lambda arr: pl.BlockSpec(arr.shape, lambda b, s: (0,) * arr.ndim)
    return pl.pallas_call(
        functools.partial(_lru_body, tq=tq, width=width),
        grid=(bsz, nblk),
        in_specs=[
            pl.BlockSpec((tq, width), row),
            pl.BlockSpec((SUBLANES, width), lambda b, s: (jnp.maximum((b * nblk + s) * per - 1, 0), 0)),
            pl.BlockSpec((tq, width), row),
            full(conv_w), full(conv_b), full(w_a), full(b_a), full(w_x), full(b_x), full(lam),
        ],
        out_specs=pl.BlockSpec((tq, width), row),
        out_shape=jax.ShapeDtypeStruct((n, width), BF16),
        scratch_shapes=[pltpu.VMEM((SUBLANES, width), F32), pltpu.VMEM((tq, width), F32),
                        pltpu.VMEM((tq, width), F32)],
        compiler_params=_cparams(("arbitrary", "arbitrary")),
        name="lru_core",
    )(u_pre, u_pre, gate, conv_w, conv_b, w_a, b_a, w_x, b_x, lam)


def rglru_layer(h, hb, bsz, seq, w_in, conv_w, conv_b, w_a, b_a, w_x, b_x, lam, w_out, ln_g, ln_b, alpha):
    width = w_in.shape[1] // 2
    w_in = w_in.astype(BF16)
    gate = proj(hb, w_in[:, :width], BF16, "lru_proj_gate", post=_post_gelu)
    u_pre = proj(hb, w_in[:, width:], F32, "lru_proj_u")
    z = lru_core(u_pre, gate, conv_w, conv_b.reshape(1, -1), w_a.astype(BF16), b_a.reshape(1, -1),
                 w_x.astype(BF16), b_x.reshape(1, -1), lam.reshape(1, -1), bsz, seq)
    return out_proj_ln(z, w_out.astype(BF16), h, ln_g, ln_b, alpha)


def kernel(x, positions, rwkv_mu, rwkv_w_in, rwkv_w0, rwkv_w1, rwkv_w2, rwkv_a0, rwkv_a1, rwkv_a2, rwkv_g1, rwkv_g2, rwkv_k_k, rwkv_k_a, rwkv_r_k, rwkv_ln_g, rwkv_ln_b, rwkv_w_out, ssd_w_in, ssd_conv_w, ssd_conv_b, ssd_dt_bias, ssd_a_log, ssd_d, ssd_norm_g, ssd_w_out, dif_w_in, dif_lq1, dif_lk1, dif_lq2, dif_lk2, dif_subln_g, dif_w_out, lru_w_in, lru_conv_w, lru_conv_b, lru_w_a, lru_b_a, lru_w_x, lru_b_x, lru_lam, lru_w_out, ffn_w_in, ffn_w_out, ln_g, ln_b):
    bsz, seq, d = x.shape
    depth = ffn_w_in.shape[0]
    alpha = (2 * depth) ** 0.25
    h = x.reshape(bsz * seq, d)
    hb = None
    for i in range(depth):
        m = i % 4
        g0, b0 = ln_g[i, 0].reshape(1, d), ln_b[i, 0].reshape(1, d)
        g1, b1 = ln_g[i, 1].reshape(1, d), ln_b[i, 1].reshape(1, d)
        if m == 0:
            h, hb = rwkv_layer(h, bsz, seq, rwkv_mu, rwkv_w_in, rwkv_w0, rwkv_w1, rwkv_w2, rwkv_a0, rwkv_a1,
                               rwkv_a2, rwkv_g1, rwkv_g2, rwkv_k_k, rwkv_k_a, rwkv_r_k, rwkv_ln_g, rwkv_ln_b,
                               rwkv_w_out, g0, b0, alpha)
        elif m == 1:
            h, hb = ssd_layer(h, hb, bsz, seq, ssd_w_in, ssd_conv_w, ssd_conv_b, ssd_dt_bias, ssd_a_log, ssd_d,
                              ssd_norm_g, ssd_w_out, g0, b0, alpha)
        elif m == 2:
            h, hb = diff_layer(h, hb, positions, bsz, seq, dif_w_in, dif_lq1, dif_lk1, dif_lq2, dif_lk2,
                               dif_subln_g, dif_w_out, g0, b0, alpha)
        else:
            h, hb = rglru_layer(h, hb, bsz, seq, lru_w_in, lru_conv_w, lru_conv_b, lru_w_a, lru_b_a, lru_w_x,
                                lru_b_x, lru_lam, lru_w_out, g0, b0, alpha)
        h, hb = ffn_ln(h, hb, ffn_w_in[i].astype(BF16), ffn_w_out[i].astype(BF16), g1, b1, alpha)
    return h.reshape(bsz, seq, d)
```

```python
import functools
import math

import numpy as np
import jax
import jax.numpy as jnp
from jax import lax
from jax.experimental import pallas as pl
from jax.experimental.pallas import tpu as pltpu

F32 = jnp.float32
BF16 = jnp.bfloat16

LANES = 128
SUBLANES = 8
BF16_ROWS = 16
VMEM_LIMIT_BYTES = 56 * 1024 * 1024

PROJ_TM = 1024
PROJ_TN = 1024
FFN_TM = 512
FFN_TF = 512
OUT_PROJ_TM = 512
OUT_PROJ_TK = 2048
OUT_PROJ_SUB = 128
RWKV_PROJ_TM = 512
RWKV_LORA_TM = 256
RW_PAIRS_PER_STEP = 16
SSD_TQ = 256
LRU_TQ = 512
ROPE_TM = 1024
ATTN_TQ = 2048
ATTN_TK = 2048
ATTN_SUB_ROWS = 256

LN_EPS = 1e-5
CHUNK = 64
CHUNK_SHIFT = 6
RW_HEAD = 64
RW_GN_EPS = 64e-5
SSD_HEAD = 64
SSD_GROUPS = 8
SSD_STATE = 128
DIF_HEADS = 8
ROPE_THETA = 500000.0
DIF_LAYER = 2
DIF_LAMBDA_INIT = 0.8 - 0.6 * math.exp(-0.3 * DIF_LAYER)
LRU_BLOCKS = 16
LRU_C = 8.0
NEG_BIG = -1e30


def _cparams(sem):
    return pltpu.CompilerParams(dimension_semantics=sem, vmem_limit_bytes=VMEM_LIMIT_BYTES)


def _dot(a, b):
    return jnp.dot(a, b, preferred_element_type=F32)


def _dot_nt(a, b):
    return lax.dot_general(a, b, (((1,), (1,)), ((), ())), preferred_element_type=F32)


def _dot_tn(a, b):
    return lax.dot_general(a, b, (((0,), (0,)), ((), ())), preferred_element_type=F32)


def _dot_01(m01, x):
    hi = x.astype(BF16)
    r1 = x - hi.astype(F32)
    mid = r1.astype(BF16)
    low = (r1 - mid.astype(F32)).astype(BF16)
    return _dot(m01, hi) + _dot(m01, mid) + _dot(m01, low)


def _half_sums(lo_mask, x):
    s_lo = jnp.sum(jnp.where(lo_mask, x, 0.0), axis=-1, keepdims=True)
    s_hi = jnp.sum(jnp.where(lo_mask, 0.0, x), axis=-1, keepdims=True)
    return jnp.where(lo_mask, s_lo, s_hi)


def _layer_norm(t, g, b):
    mu = jnp.mean(t, -1, keepdims=True)
    d = t - mu
    var = jnp.mean(d * d, -1, keepdims=True)
    return d * lax.rsqrt(var + LN_EPS) * g + b


def _neg_softplus_neg(z):
    return jnp.minimum(z, 0.0) - jnp.log(1.0 + jnp.exp(-jnp.abs(z)))


def _softplus(z):
    return jnp.maximum(z, 0.0) + jnp.log(1.0 + jnp.exp(-jnp.abs(z)))


def _silu(z):
    return z * jax.nn.sigmoid(z)


def _shifted(x, halo, d):
    head = pltpu.roll(jnp.concatenate([halo, x[:SUBLANES]], axis=0), d, axis=0)[SUBLANES:]
    if x.shape[0] == SUBLANES:
        return head
    return jnp.concatenate([head, pltpu.roll(x, d, axis=0)[SUBLANES:]], axis=0)


def _causal_conv4(x, halo, cw, cb):
    out = x * cw[3:4, :] + cb
    for d in (1, 2, 3):
        out = out + _shifted(x, halo, d) * cw[3 - d:4 - d, :]
    return out


def _proj_body(*refs, n_bands, post):
    a, w = refs[0], refs[1]
    bands = refs[2:2 + n_bands]
    out = refs[2 + n_bands]
    out[...] = post(_dot(a[...], w[...]), *[r[...] for r in bands]).astype(out.dtype)


def proj(hb, w, out_dtype, name, *, post=lambda acc: acc, bands=()):
    m, kdim = hb.shape
    n = w.shape[1]
    tm, tn = min(PROJ_TM, m), min(PROJ_TN, n)
    assert m % tm == 0 and n % tn == 0
    in_specs = [pl.BlockSpec((tm, kdim), lambda i, j: (i, 0)), pl.BlockSpec((kdim, tn), lambda i, j: (0, j))]
    in_specs += [pl.BlockSpec((tm, b.shape[1]), lambda i, j: (i, 0)) for b in bands]
    return pl.pallas_call(
        functools.partial(_proj_body, n_bands=len(bands), post=post),
        grid=(m // tm, n // tn),
        in_specs=in_specs,
        out_specs=pl.BlockSpec((tm, tn), lambda i, j: (i, j)),
        out_shape=jax.ShapeDtypeStruct((m, n), out_dtype),
        compiler_params=_cparams(("parallel", "parallel")),
        name=name,
    )(hb, w, *bands)


def _out_proj_body(z_ref, w_ref, h_ref, g_ref, b_ref, out_ref, outb_ref, *scratch, nk, alpha, tm):
    def finish():
        for r in range(tm // OUT_PROJ_SUB):
            rows = slice(r * OUT_PROJ_SUB, (r + 1) * OUT_PROJ_SUB)
            t = _dot(z_ref[rows, :], w_ref[...])
            if nk > 1:
                t = t + scratch[0][rows, :]
            o = _layer_norm(alpha * h_ref[rows, :] + t, g_ref[...], b_ref[...])
            out_ref[rows, :] = o
            outb_ref[rows, :] = o.astype(BF16)

    if nk == 1:
        finish()
    else:
        acc = scratch[0]
        k = pl.program_id(1)

        @pl.when(k == 0)
        def _():
            acc[...] = _dot(z_ref[...], w_ref[...])

        @pl.when(jnp.logical_and(k > 0, k < nk - 1))
        def _():
            acc[...] += _dot(z_ref[...], w_ref[...])

        pl.when(k == nk - 1)(finish)


def out_proj_ln(z, w, h, g, b, alpha, *, tm=OUT_PROJ_TM, tk=OUT_PROJ_TK):
    m, kdim = z.shape
    n = w.shape[1]
    tm = min(tm, m)
    nk = kdim // tk
    assert kdim % tk == 0 and m % tm == 0 and tm % OUT_PROJ_SUB == 0
    row = pl.BlockSpec((tm, n), lambda i, k: (i, 0))
    vec = pl.BlockSpec((1, n), lambda i, k: (0, 0))
    return pl.pallas_call(
        functools.partial(_out_proj_body, nk=nk, alpha=alpha, tm=tm),
        grid=(m // tm, nk),
        in_specs=[pl.BlockSpec((tm, tk), lambda i, k: (i, k)), pl.BlockSpec((tk, n), lambda i, k: (k, 0)),
                  row, vec, vec],
        out_specs=[row, row],
        out_shape=[jax.ShapeDtypeStruct((m, n), F32), jax.ShapeDtypeStruct((m, n), BF16)],
        scratch_shapes=[pltpu.VMEM((tm, n), F32)] if nk > 1 else [],
        compiler_params=_cparams(("parallel", "arbitrary")),
        name="out_proj_ln",
    )(z, w, h, g, b)


def _ffn_body(x_ref, xb_ref, wg_ref, wu_ref, wo_ref, g_ref, b_ref, out_ref, outb_ref, acc_ref, *, nf, alpha):
    f = pl.program_id(1)

    @pl.when(f == 0)
    def _():
        acc_ref[...] = jnp.zeros_like(acc_ref)

    xb = xb_ref[...]
    gate = _dot(xb, wg_ref[...])
    up = _dot(xb, wu_ref[...])
    act = (_silu(gate) * up).astype(BF16)
    acc_ref[...] += _dot(act, wo_ref[...])

    @pl.when(f == nf - 1)
    def _():
        out = _layer_norm(alpha * x_ref[...] + acc_ref[...], g_ref[...], b_ref[...])
        out_ref[...] = out
        outb_ref[...] = out.astype(BF16)


def ffn_ln(x, xb, w_in, w_out, g, b, alpha, *, tm=FFN_TM, tf=FFN_TF):
    m, d = x.shape
    dff = w_out.shape[0]
    tm = min(tm, m)
    nf = dff // tf
    assert dff % tf == 0 and m % tm == 0
    row = pl.BlockSpec((tm, d), lambda i, f: (i, 0))
    return pl.pallas_call(
        functools.partial(_ffn_body, nf=nf, alpha=alpha),
        grid=(m // tm, nf),
        in_specs=[
            row, row,
            pl.BlockSpec((d, tf), lambda i, f: (0, f)),
            pl.BlockSpec((d, tf), lambda i, f: (0, nf + f)),
            pl.BlockSpec((tf, d), lambda i, f: (f, 0)),
            pl.BlockSpec((1, d), lambda i, f: (0, 0)),
            pl.BlockSpec((1, d), lambda i, f: (0, 0)),
        ],
        out_specs=[row, row],
        out_shape=[jax.ShapeDtypeStruct((m, d), F32), jax.ShapeDtypeStruct((m, d), BF16)],
        scratch_shapes=[pltpu.VMEM((tm, d), F32)],
        compiler_params=_cparams(("parallel", "arbitrary")),
        name="ffn_ln",
    )(x, xb, w_in, w_in, w_out, g, b)


def _rwkv_proj_body(x_ref, halo_ref, mu_ref, w_ref, out_ref, *, tm, seq):
    i = pl.program_id(0)
    x = x_ref[...]
    halo = jnp.where((i * tm) % seq == 0, 0.0, halo_ref[...])
    xx = _shifted(x, halo, 1) - x
    lhs = (x + xx * mu_ref[...]).astype(BF16)
    out_ref[...] = _dot(lhs, w_ref[...]).astype(out_ref.dtype)


def rwkv_proj(x, mu3, w3, seq, *, tm=RWKV_PROJ_TM):
    m, d = x.shape
    n = w3.shape[2]
    tm = min(tm, seq)
    per = tm // SUBLANES
    return pl.pallas_call(
        functools.partial(_rwkv_proj_body, tm=tm, seq=seq),
        grid=(m // tm, 3),
        in_specs=[
            pl.BlockSpec((tm, d), lambda i, j: (i, 0)),
            pl.BlockSpec((SUBLANES, d), lambda i, j: (jnp.maximum(i * per - 1, 0), 0)),
            pl.BlockSpec((None, 1, d), lambda i, j: (j, 0, 0)),
            pl.BlockSpec((None, d, n), lambda i, j: (j, 0, 0)),
        ],
        out_specs=pl.BlockSpec((None, tm, n), lambda i, j: (j, i, 0)),
        out_shape=jax.ShapeDtypeStruct((3, m, n), BF16),
        compiler_params=_cparams(("parallel", "arbitrary")),
        name="rwkv_proj",
    )(x, x, mu3, w3)


def _rwkv_lora_body(x_ref, halo_ref, mu_ref, w1_ref, w2_ref, a1_ref, a2_ref, g1_ref, g2_ref, w0_ref, a0_ref,
                    wl_ref, a_ref, g_ref, *, tm, seq):
    i = pl.program_id(0)
    x = x_ref[...]
    halo = jnp.where((i * tm) % seq == 0, 0.0, halo_ref[...])
    xx = _shifted(x, halo, 1) - x
    mu = mu_ref[...]
    xw = (x + xx * mu[0:1, :]).astype(BF16)
    xa = (x + xx * mu[1:2, :]).astype(BF16)
    xg = (x + xx * mu[2:3, :]).astype(BF16)
    zw = w0_ref[...] + _dot(jnp.tanh(_dot(xw, w1_ref[...])).astype(BF16), w2_ref[...])
    wl_ref[...] = _neg_softplus_neg(zw) - 0.5
    a_ref[...] = jax.nn.sigmoid(a0_ref[...] + _dot(_dot(xa, a1_ref[...]).astype(BF16), a2_ref[...]))
    g_ref[...] = _dot(jax.nn.sigmoid(_dot(xg, g1_ref[...])).astype(BF16), g2_ref[...]).astype(g_ref.dtype)


def rwkv_lora(x, mu3, w1, w2, a1, a2, g1, g2, w0, a0, seq, *, tm=RWKV_LORA_TM):
    m, d = x.shape
    tm = min(tm, seq)
    per = tm // SUBLANES
    full = lambda arr: pl.BlockSpec(arr.shape, lambda i: (0,) * arr.ndim)
    row = pl.BlockSpec((tm, d), lambda i: (i, 0))
    return pl.pallas_call(
        functools.partial(_rwkv_lora_body, tm=tm, seq=seq),
        grid=(m // tm,),
        in_specs=[row, pl.BlockSpec((SUBLANES, d), lambda i: (jnp.maximum(i * per - 1, 0), 0)),
                  full(mu3), full(w1), full(w2), full(a1), full(a2), full(g1), full(g2), full(w0), full(a0)],
        out_specs=[row, row, row],
        out_shape=[jax.ShapeDtypeStruct((m, d), F32), jax.ShapeDtypeStruct((m, d), F32),
                   jax.ShapeDtypeStruct((m, d), BF16)],
        compiler_params=_cparams(("parallel",)),
        name="rwkv_lora",
    )(x, x, mu3, w1, w2, a1, a2, g1, g2, w0, a0)


RW_LEVELS = 6


def _rwkv_masks():
    n = 2 * CHUNK
    t = np.arange(n)[:, None]
    s = np.arange(n)[None, :]
    strict = (t > s).astype(np.float32)
    incl = (t >= s).astype(np.float32)
    tri = np.stack([strict, incl])
    lvl = []
    for lv in range(RW_LEVELS):
        sz = 1 << lv
        lvl.append(((t // (2 * sz) == s // (2 * sz)) & ((t // sz) % 2 == 1) & ((s // sz) % 2 == 0)).astype(np.float32))
    t64 = incl[:CHUNK, :CHUNK]
    return jnp.asarray(tri), jnp.asarray(np.stack(lvl)), jnp.asarray(t64)


def _rwkv_core_body(r_ref, k_ref, v_ref, wl_ref, a_ref, g_ref, kk_ref, ka_ref, rk_ref, lng_ref, lnb_ref,
                    tri_ref, lvl_ref, t64_ref, out_ref, st_ref):
    c = pl.program_id(2)

    @pl.when(c == 0)
    def _():
        st_ref[...] = jnp.zeros_like(st_ref)

    strict, incl = tri_ref[0], tri_ref[1]
    eye = incl - strict
    t64 = t64_ref[...].astype(BF16)
    lo = lax.broadcasted_iota(jnp.int32, (1, LANES), 1) < RW_HEAD
    inv_head = 1.0 / RW_HEAD

    def stack(x):
        return jnp.concatenate([jnp.where(lo, x, 0.0), jnp.where(lo, 0.0, x)], axis=0).astype(BF16)

    pairs = range(RW_PAIRS_PER_STEP)
    sls = [slice(q * LANES, (q + 1) * LANES) for q in pairs]

    ld = lambda ref, sl: ref[:, sl].astype(F32)
    kk_sq = []
    for sl in sls:
        kk = ld(k_ref, sl) * kk_ref[:, sl]
        kk_sq.append(_half_sums(lo, kk * kk))
    cums = [_dot_01(t64, -jnp.exp(wl_ref[:, sl])) for sl in sls]

    ops = []
    for sl, ss, cum in zip(sls, kk_sq, cums):
        k, a = ld(k_ref, sl), a_ref[:, sl]
        kk = k * kk_ref[:, sl] / jnp.maximum(jnp.sqrt(ss), 1e-12)
        k2 = k * (1.0 + (a - 1.0) * ka_ref[:, sl])
        lw = -jnp.exp(wl_ref[:, sl])
        cl = cum[CHUNK - 1:CHUNK, :]
        e_neg = jnp.exp(-cum)
        e_end = jnp.exp(cl - cum)
        kb = kk * a
        ops.append(dict(
            a_s=stack(-kk * jnp.exp(cum - lw)),
            r_s=stack(ld(r_ref, sl) * jnp.exp(cum)),
            b_s=stack(kb * e_neg), k_s=stack(k2 * e_neg), v_s=stack(ld(v_ref, sl)),
            bh_s=stack(kb * e_end), kh_s=stack(k2 * e_end),
            w_end=jnp.exp(cl), k2=k2))

    n2 = 2 * CHUNK
    quads = [(2 * i, 2 * i + 1) for i in range(RW_PAIRS_PER_STEP // 2)]
    zero = jnp.zeros((n2, n2), BF16)
    rows2 = lambda top, bot: jnp.concatenate([top, bot], axis=0)
    side = lambda xa, xb: jnp.concatenate([xa, xb], axis=1)
    bdiag = lambda xa, xb: rows2(side(xa, zero), side(zero, xb))
    half = lambda x, j: x[:, j * n2:(j + 1) * n2]
    col = lambda key, q: side(ops[q[0]][key], ops[q[1]][key])

    gmask = rows2(side(strict, strict), side(incl, incl))
    gram = [gmask * _dot_nt(rows2(o["a_s"], o["r_s"]), rows2(o["b_s"], o["k_s"])) for o in ops]
    a_ab = [g[:n2, :n2] for g in gram]
    a_ak = [g[:n2, n2:].astype(BF16) for g in gram]
    r_b = [g[n2:, :n2].astype(BF16) for g in gram]
    r_k = [g[n2:, n2:].astype(BF16) for g in gram]

    t_inv = [eye + m * lvl_ref[0] for m in a_ab]
    for lv in range(1, RW_LEVELS):
        tb = [t.astype(BF16) for t in t_inv]
        off = [(m * lvl_ref[lv]).astype(BF16) for m in a_ab]
        xs = [_dot(side(tb[a], tb[b]), bdiag(off[a], off[b])).astype(BF16) for a, b in quads]
        upd = [_dot(x, bdiag(tb[a], tb[b])) for x, (a, b) in zip(xs, quads)]
        t_inv = [t + half(upd[p // 2], p % 2) for p, t in enumerate(t_inv)]
    tb = [t.astype(BF16) for t in t_inv]

    s_old = [st_ref[q] for q in pairs]
    s_b = [s.astype(BF16) for s in s_old]
    p12 = [_dot_nt(rows2(col("a_s", q), col("r_s", q)), bdiag(s_b[q[0]], s_b[q[1]]))
           + _dot(rows2(side(a_ak[q[0]], a_ak[q[1]]), side(r_k[q[0]], r_k[q[1]])), bdiag(ops[q[0]]["v_s"], ops[q[1]]["v_s"]))
           for q in quads]
    rhs = [p[:n2].astype(BF16) for p in p12]
    u_b = [_dot(side(tb[a], tb[b]), bdiag(half(x, 0), half(x, 1))).astype(BF16) for x, (a, b) in zip(rhs, quads)]
    y_q = [p[n2:] + _dot(side(r_b[a], r_b[b]), bdiag(half(u, 0), half(u, 1))) for p, u, (a, b) in zip(p12, u_b, quads)]
    inc = [_dot_tn(rows2(u, col("v_s", q)), rows2(col("bh_s", q), col("kh_s", q))) for u, q in zip(u_b, quads)]
    for z, (a, b) in zip(inc, quads):
        st_ref[a] = s_old[a] * ops[a]["w_end"] + z[:n2, :n2]
        st_ref[b] = s_old[b] * ops[b]["w_end"] + z[n2:, n2:]

    y_s = [half(y_q[p // 2], p % 2) for p in pairs]
    ys = [y[:CHUNK] + y[CHUNK:] for y in y_s]
    mus = [_half_sums(lo, y) * inv_head for y in ys]
    ds = [y - mu for y, mu in zip(ys, mus)]
    var = [_half_sums(lo, d * d) * inv_head for d in ds]
    bonus = [_half_sums(lo, ld(r_ref, sl) * o["k2"] * rk_ref[:, sl]) for sl, o in zip(sls, ops)]
    for sl, d, vr, bn in zip(sls, ds, var, bonus):
        yn = d * lax.rsqrt(vr + RW_GN_EPS) * lng_ref[:, sl] + lnb_ref[:, sl]
        out_ref[:, sl] = ((yn + bn * ld(v_ref, sl)) * ld(g_ref, sl)).astype(out_ref.dtype)


def rwkv_core(rkv, wl, a, g, k_k, k_a, r_k, ln_g, ln_b, bsz, seq):
    _, m, d = rkv.shape
    width = RW_PAIRS_PER_STEP * LANES
    nc = seq // CHUNK
    tri, lvl, t64 = _rwkv_masks()
    row = lambda b, p, c: (b * nc + c, p)
    rkv_spec = lambda j: pl.BlockSpec((None, CHUNK, width), lambda b, p, c: (j, b * nc + c, p))
    act = pl.BlockSpec((CHUNK, width), row)
    par = pl.BlockSpec((1, width), lambda b, p, c: (0, p))
    full = lambda arr: pl.BlockSpec(arr.shape, lambda b, p, c: (0,) * arr.ndim)
    return pl.pallas_call(
        _rwkv_core_body,
        grid=(bsz, d // width, nc),
        in_specs=[rkv_spec(0), rkv_spec(1), rkv_spec(2), act, act, act, par, par, par, par, par,
                  full(tri), full(lvl), full(t64)],
        out_specs=act,
        out_shape=jax.ShapeDtypeStruct((m, d), BF16),
        scratch_shapes=[pltpu.VMEM((RW_PAIRS_PER_STEP, LANES, LANES), F32)],
        compiler_params=_cparams(("parallel", "parallel", "arbitrary")),
        name="rwkv_core",
    )(rkv, rkv, rkv, wl, a, g, k_k, k_a, r_k, ln_g, ln_b, tri, lvl, t64)


def _pad_to(x, axis, size):
    pad = [(0, 0)] * x.ndim
    pad[axis] = (0, size - x.shape[axis])
    return jnp.pad(x, pad)


def rwkv_layer(h, bsz, seq, mu, w_in, w0, w1, w2, a0, a1, a2, g1, g2, k_k, k_a, r_k, gn_g, gn_b, w_out,
               ln_g, ln_b, alpha):
    d = h.shape[1]
    v2 = lambda t: t.reshape(1, d)
    mu_rkv = jnp.stack([mu[0], mu[2], mu[3]]).reshape(3, 1, d)
    mu_lora = jnp.stack([mu[1], mu[4], mu[5]])
    rkv = rwkv_proj(h, mu_rkv, w_in.astype(BF16), seq)
    wl, a, g = rwkv_lora(
        h, mu_lora,
        _pad_to(w1, 1, LANES).astype(BF16), _pad_to(w2, 0, LANES).astype(BF16),
        _pad_to(a1, 1, LANES).astype(BF16), _pad_to(a2, 0, LANES).astype(BF16),
        g1.astype(BF16), g2.astype(BF16), v2(w0), v2(a0), seq)
    z = rwkv_core(rkv, wl, a, g, v2(k_k), v2(k_a), v2(r_k), v2(gn_g), v2(gn_b), bsz, seq)
    return out_proj_ln(z, w_out.astype(BF16), h, ln_g, ln_b, alpha)


def _ssd_consts(heads):
    lanes = heads * SSD_HEAD
    head_of = np.arange(lanes) // SSD_HEAD
    pos_of = np.arange(lanes) % SSD_HEAD
    expand = (np.arange(LANES)[:, None] == head_of[None, :]).astype(np.float32)
    eye = (np.arange(CHUNK)[:, None] == pos_of[None, :]).astype(np.float32)
    tril = (np.arange(CHUNK)[:, None] >= pos_of[None, :]).astype(np.float32)
    return jnp.asarray(expand, BF16), jnp.asarray(eye), jnp.asarray(tril)


def _expand_heads(x, e01):
    hi = x.astype(BF16)
    low = (x - hi.astype(F32)).astype(BF16)
    return _dot(hi, e01) + _dot(low, e01)


def _ssd_body(xbc_ref, halo_ref, z_ref, dtp_ref, cw_ref, cb_ref, dtb_ref, alog_ref, dsk_ref, ng_ref, t64_ref,
              e01_ref, eye_ref, tril_ref, out_ref, st_ref, xc_ref, dt_ref, da_ref, *, tq, d_inner):
    s = pl.program_id(1)

    @pl.when(s == 0)
    def _():
        st_ref[...] = jnp.zeros_like(st_ref)

    halo = jnp.where(s == 0, 0.0, halo_ref[...].astype(F32)[BF16_ROWS - SUBLANES:, :])
    xc_ref[...] = _silu(_causal_conv4(xbc_ref[...].astype(F32), halo, cw_ref[...], cb_ref[...]))
    dt_all = _softplus(dtp_ref[...] + dtb_ref[...])
    dt_ref[...] = dt_all
    da_ref[...] = dt_all * (-jnp.exp(alog_ref[...]))

    t64b = t64_ref[...].astype(BF16)
    lo = lax.broadcasted_iota(jnp.int32, (1, LANES), 1) < SSD_HEAD
    gn = SSD_GROUPS * SSD_STATE
    gw = d_inner // SSD_GROUPS
    pairs_per_group = gw // LANES

    def chunk(ci, carry):
        r0 = pl.multiple_of(ci * CHUNK, CHUNK)
        rows = pl.ds(r0, CHUNK)
        e01 = e01_ref[...]
        cs = _dot_01(t64b, da_ref[rows, :])
        dte = _expand_heads(dt_ref[rows, :], e01)
        cse = _expand_heads(cs, e01)
        cs_row = jnp.sum(eye_ref[...] * cse, axis=0, keepdims=True)
        decay = jnp.exp(jnp.minimum(cse - cs_row, 0.0)) * tril_ref[...]
        cs_last = cse[CHUNK - 1:CHUNK, :]
        ecs = jnp.exp(cse)
        to_end = jnp.exp(cs_last - cse)
        e_last = jnp.exp(cs_last)
        for g in range(SSD_GROUPS):
            gsl = slice(g * gw, (g + 1) * gw)
            bg = xc_ref[rows, d_inner + g * SSD_STATE:d_inner + (g + 1) * SSD_STATE]
            cg = xc_ref[rows, d_inner + gn + g * SSD_STATE:d_inner + gn + (g + 1) * SSD_STATE].astype(BF16)
            bgb = bg.astype(BF16)
            cb_rep = _dot_nt(cg, jnp.concatenate([bgb] * (gw // SSD_HEAD), axis=0))
            mat = (cb_rep * decay[:, gsl]).astype(BF16)
            bg_t = bg.T.astype(BF16)
            ps = [g * pairs_per_group + q for q in range(pairs_per_group)]
            sls = [slice(p * LANES, (p + 1) * LANES) for p in ps]
            xs = [xc_ref[rows, sl] for sl in sls]
            xdts = [x * dte[:, sl] for x, sl in zip(xs, sls)]
            x_st = [jnp.concatenate([jnp.where(lo, xdt, 0.0), jnp.where(lo, 0.0, xdt)], axis=0).astype(BF16)
                    for xdt in xdts]
            x_end = [(xdt * to_end[:, sl]).astype(BF16) for xdt, sl in zip(xdts, sls)]
            sts = [st_ref[p] for p in ps]
            intra = [_dot(mat[:, q * LANES:(q + 1) * LANES], xs_) for q, xs_ in enumerate(x_st)]
            inter = [_dot(cg, st.astype(BF16)) for st in sts]
            upd = [_dot(bg_t, xe) for xe in x_end]
            for p, sl, st, u in zip(ps, sls, sts, upd):
                st_ref[p] = st * e_last[:, sl] + u
            ys = []
            ssq = jnp.zeros((CHUNK, 1), F32)
            for sl, x, ya, yb in zip(sls, xs, intra, inter):
                y = ya + yb * ecs[:, sl] + x * dsk_ref[:, sl]
                y = y * _silu(z_ref[rows, sl].astype(F32))
                ssq = ssq + jnp.sum(y * y, axis=-1, keepdims=True)
                ys.append(y)
            inv = lax.rsqrt(ssq * (1.0 / gw) + 1e-5)
            for sl, y in zip(sls, ys):
                out_ref[rows, sl] = (y * inv * ng_ref[:, sl]).astype(out_ref.dtype)
        return carry

    lax.fori_loop(0, tq // CHUNK, chunk, 0)


def ssd_core(xbc_pre, z, dt_pre, conv_w, conv_b, dt_bias, a_log, d_skip, norm_g, bsz, seq, *, tq=SSD_TQ):
    n, conv_dim = xbc_pre.shape
    d_inner = z.shape[1]
    tq = min(tq, seq)
    nblk = seq // tq
    per = tq // BF16_ROWS
    npairs = d_inner // LANES
    t64 = _rwkv_masks()[2]
    e01, eye, tril = _ssd_consts(d_inner // SSD_HEAD)
    row = lambda b, s: (b * nblk + s, 0)
    full = lambda arr: pl.BlockSpec(arr.shape, lambda b, s: (0,) * arr.ndim)
    return pl.pallas_call(
        functools.partial(_ssd_body, tq=tq, d_inner=d_inner),
        grid=(bsz, nblk),
        in_specs=[
            pl.BlockSpec((tq, conv_dim), row),
            pl.BlockSpec((BF16_ROWS, conv_dim), lambda b, s: (jnp.maximum((b * nblk + s) * per - 1, 0), 0)),
            pl.BlockSpec((tq, d_inner), row),
            pl.BlockSpec((tq, LANES), row),
            full(conv_w), full(conv_b), full(dt_bias), full(a_log), full(d_skip), full(norm_g), full(t64),
            full(e01), full(eye), full(tril),
        ],
        out_specs=pl.BlockSpec((tq, d_inner), row),
        out_shape=jax.ShapeDtypeStruct((n, d_inner), BF16),
        scratch_shapes=[pltpu.VMEM((npairs, SSD_STATE, LANES), F32), pltpu.VMEM((tq, conv_dim), F32),
                        pltpu.VMEM((tq, LANES), F32), pltpu.VMEM((tq, LANES), F32)],
        compiler_params=_cparams(("arbitrary", "arbitrary")),
        name="ssd_core",
    )(xbc_pre, xbc_pre, z, dt_pre, conv_w, conv_b, dt_bias, a_log, d_skip, norm_g, t64, e01, eye, tril)


def ssd_layer(h, hb, bsz, seq, w_in, conv_w, conv_b, dt_bias, a_log, d_skip, norm_g, w_out, ln_g, ln_b, alpha):
    d_inner = w_out.shape[0]
    conv_dim = conv_w.shape[1]
    w_in = w_in.astype(BF16)
    z = proj(hb, w_in[:, :d_inner], BF16, "ssd_proj_z")
    xbc_pre = proj(hb, w_in[:, d_inner:d_inner + conv_dim], BF16, "ssd_proj_xbc")
    dt_pre = proj(hb, _pad_to(w_in[:, d_inner + conv_dim:], 1, LANES), F32, "ssd_proj_dt")
    pad_row = lambda t: _pad_to(t.reshape(1, -1), 1, LANES)
    d_skip_lanes = jnp.repeat(d_skip, SSD_HEAD).reshape(1, -1)
    y = ssd_core(xbc_pre, z, dt_pre, conv_w, conv_b.reshape(1, -1), pad_row(dt_bias), pad_row(a_log),
                 d_skip_lanes, norm_g.reshape(1, -1), bsz, seq)
    return out_proj_ln(y, w_out.astype(BF16), h, ln_g, ln_b, alpha)


ROPE_HALF = 16


def _rope_table_body(pos_ref, freq_ref, cos_ref, sin_ref):
    ang = pos_ref[...].astype(F32) * freq_ref[...]
    lane = lax.broadcasted_iota(jnp.int32, (1, LANES), 1)
    sn = jnp.sin(ang)
    cos_ref[...] = jnp.cos(ang)
    sin_ref[...] = jnp.where(lane < ROPE_HALF, -sn, jnp.where(lane < 2 * ROPE_HALF, sn, 0.0))


def rope_tables(pos, freq, *, tm=ROPE_TM):
    n = pos.shape[0]
    tm = min(tm, n)
    band = pl.BlockSpec((tm, LANES), lambda i: (i, 0))
    return pl.pallas_call(
        _rope_table_body,
        grid=(n // tm,),
        in_specs=[pl.BlockSpec((tm, 1), lambda i: (i, 0)), pl.BlockSpec((1, LANES), lambda i: (0, 0))],
        out_specs=[band, band],
        out_shape=[jax.ShapeDtypeStruct((n, LANES), F32)] * 2,
        compiler_params=_cparams(("parallel",)),
        name="rope_tables",
    )(pos, freq)


def _post_rope(scale, acc, cos, sin):
    lane = lax.broadcasted_iota(jnp.int32, (1, LANES), 1)
    outs = []
    for j in range(acc.shape[1] // LANES):
        t = acc[:, j * LANES:(j + 1) * LANES]
        partner = jnp.where(lane < ROPE_HALF, pltpu.roll(t, LANES - ROPE_HALF, axis=1), pltpu.roll(t, ROPE_HALF, axis=1))
        outs.append((t * cos + partner * sin) * scale)
    return jnp.concatenate(outs, axis=1)


def _attn_body(qi_ref, ki_ref, last_ref, q_ref, k_ref, v_ref, lq1_ref, lk1_ref, lq2_ref, lk2_ref, sg_ref,
               out_ref, m_ref, l_ref, acc_ref, *, tq, tk, hd):
    p = pl.program_id(2)
    qi, ki = qi_ref[p], ki_ref[p]

    @pl.when(ki == 0)
    def _():
        m_ref[...] = jnp.full_like(m_ref, NEG_BIG)
        l_ref[...] = jnp.zeros_like(l_ref)
        acc_ref[...] = jnp.zeros_like(acc_ref)

    sub = min(ATTN_SUB_ROWS, tq)
    n_sub = tq // sub

    def update(masked):
        def n_keys(r):
            return (r + 1) * sub if masked and tq == tk else tk

        def qk(r):
            rows = slice(r * sub, (r + 1) * sub)
            return [_dot_nt(q_ref[rows, j * hd:(j + 1) * hd], k_ref[:n_keys(r), j * hd:(j + 1) * hd])
                    for j in range(2)]

        scores = qk(0)
        for r in range(n_sub):
            nxt = qk(r + 1) if r + 1 < n_sub else None
            rows = slice(r * sub, (r + 1) * sub)
            if masked:
                row0 = qi * tq + r * sub
                q_chunk = jnp.right_shift(row0 + lax.broadcasted_iota(jnp.int32, (sub, 1), 0), CHUNK_SHIFT)
                k_chunk = jnp.right_shift(ki * tk + lax.broadcasted_iota(jnp.int32, (1, n_keys(r)), 1), CHUNK_SHIFT)
                visible = k_chunk <= q_chunk
                scores = [jnp.where(visible, s, NEG_BIG) for s in scores]
            v = v_ref[:n_keys(r), :]
            for j in range(2):
                m_old = m_ref[j, rows, :]
                m_new = jnp.maximum(m_old, jnp.max(scores[j], axis=-1, keepdims=True))
                pe = jnp.exp2(scores[j] - m_new)
                scale = jnp.exp2(m_old - m_new)
                l_ref[j, rows, :] = scale * l_ref[j, rows, :] + jnp.sum(pe, axis=-1, keepdims=True)
                acc_ref[j, rows, :] = scale * acc_ref[j, rows, :] + _dot(pe.astype(BF16), v)
                m_ref[j, rows, :] = m_new
            scores = nxt

    needs_mask = (ki + 1) * tk > qi * tq + CHUNK

    @pl.when(needs_mask)
    def _():
        update(True)

    @pl.when(jnp.logical_not(needs_mask))
    def _():
        update(False)

    @pl.when(last_ref[p] == 1)
    def _():
        lam = (jnp.exp(jnp.sum(lq1_ref[...] * lk1_ref[...], axis=-1, keepdims=True))
               - jnp.exp(jnp.sum(lq2_ref[...] * lk2_ref[...], axis=-1, keepdims=True)) + DIF_LAMBDA_INIT)
        o = acc_ref[0] / l_ref[0] - lam * (acc_ref[1] / l_ref[1])
        o = o * lax.rsqrt(jnp.mean(o * o, axis=-1, keepdims=True) + 1e-5) * sg_ref[...]
        out_ref[...] = (o * (1.0 - DIF_LAMBDA_INIT)).astype(out_ref.dtype)


def diff_attention(q, k, v, lq1, lk1, lq2, lk2, subln_g, bsz, seq, *, tq=ATTN_TQ, tk=ATTN_TK):
    n, d = q.shape
    hw = d // DIF_HEADS
    hd = hw // 2
    tq, tk = min(tq, seq), min(tk, seq)
    assert tq % CHUNK == 0 and tk % CHUNK == 0 and CHUNK == 1 << CHUNK_SHIFT
    nq, nk = seq // tq, seq // tk
    n_kv = lambda a: -(-((a + 1) * tq) // tk)
    pairs = [(a, b) for a in range(nq) for b in range(n_kv(a))]
    qi = jnp.asarray([a for a, _ in pairs], jnp.int32)
    ki = jnp.asarray([b for _, b in pairs], jnp.int32)
    last = jnp.asarray([1 if b == n_kv(a) - 1 else 0 for a, b in pairs], jnp.int32)
    vec = pl.BlockSpec((1, hd), lambda b, h, p, qi, ki, last: (0, 0))
    grid_spec = pltpu.PrefetchScalarGridSpec(
        num_scalar_prefetch=3,
        grid=(bsz, DIF_HEADS, len(pairs)),
        in_specs=[
            pl.BlockSpec((tq, hw), lambda b, h, p, qi, ki, last: (b * nq + qi[p], h)),
            pl.BlockSpec((tk, hw), lambda b, h, p, qi, ki, last: (b * nk + ki[p], h)),
            pl.BlockSpec((tk, hw), lambda b, h, p, qi, ki, last: (b * nk + ki[p], h)),
            vec, vec, vec, vec,
            pl.BlockSpec((1, hw), lambda b, h, p, qi, ki, last: (0, 0)),
        ],
        out_specs=pl.BlockSpec((tq, hw), lambda b, h, p, qi, ki, last: (b * nq + qi[p], h)),
        scratch_shapes=[pltpu.VMEM((2, tq, 1), F32), pltpu.VMEM((2, tq, 1), F32), pltpu.VMEM((2, tq, hw), F32)],
    )
    return pl.pallas_call(
        functools.partial(_attn_body, tq=tq, tk=tk, hd=hd),
        grid_spec=grid_spec,
        out_shape=jax.ShapeDtypeStruct((n, d), BF16),
        compiler_params=_cparams(("parallel", "parallel", "arbitrary")),
        name="diff_attention",
    )(qi, ki, last, q, k, v, lq1, lk1, lq2, lk2, subln_g)


def diff_layer(h, hb, positions, bsz, seq, w_in, lq1, lk1, lq2, lk2, subln_g, w_out, ln_g, ln_b, alpha):
    d = h.shape[1]
    hd = d // DIF_HEADS // 2
    rope = hd // 4
    assert hd == LANES and rope == 2 * ROPE_HALF
    inv_freq = ROPE_THETA ** (-jnp.arange(0, rope, 2, dtype=F32) / rope)
    freq = jnp.concatenate([inv_freq, inv_freq, jnp.zeros((LANES - rope,), F32)]).reshape(1, LANES)
    cos, sin = rope_tables(positions.reshape(bsz * seq, 1), freq)
    w_in = w_in.astype(BF16)
    rope_mm = lambda w, scale, name: proj(hb, w, BF16, name, post=functools.partial(_post_rope, scale),
                                          bands=[cos, sin])
    q = rope_mm(w_in[:, :d], hd ** -0.5 * math.log2(math.e), "dif_proj_q")
    k = rope_mm(w_in[:, d:2 * d], 1.0, "dif_proj_k")
    v = proj(hb, w_in[:, 2 * d:], BF16, "dif_proj_v")
    v2 = lambda t: t.reshape(1, -1)
    o = diff_attention(q, k, v, v2(lq1), v2(lk1), v2(lq2), v2(lk2), v2(subln_g), bsz, seq)
    return out_proj_ln(o, w_out.astype(BF16), h, ln_g, ln_b, alpha)


def _post_gelu(acc):
    c = math.sqrt(2.0 / math.pi)
    return 0.5 * acc * (1.0 + jnp.tanh(c * (acc + 0.044715 * acc * acc * acc)))


def _lru_body(u_ref, halo_ref, gate_ref, cw_ref, cb_ref, wa_ref, ba_ref, wx_ref, bx_ref, lam_ref,
              out_ref, h_ref, a_ref, b_ref, *, tq, width):
    s = pl.program_id(1)

    @pl.when(s == 0)
    def _():
        h_ref[...] = jnp.zeros_like(h_ref)

    halo = jnp.where(s == 0, 0.0, halo_ref[...])
    u = _causal_conv4(u_ref[...], halo, cw_ref[...], cb_ref[...])

    nsp = _softplus(-lam_ref[...])
    blk = width // LRU_BLOCKS
    for j in range(LRU_BLOCKS):
        sl = slice(j * blk, (j + 1) * blk)
        uj = u[:, sl]
        ub = uj.astype(BF16)
        r = jax.nn.sigmoid(_dot(ub, wa_ref[j]) + ba_ref[:, sl])
        i = jax.nn.sigmoid(_dot(ub, wx_ref[j]) + bx_ref[:, sl])
        log_a = -LRU_C * r * nsp[:, sl]
        a_ref[:, sl] = jnp.exp(log_a)
        b_ref[:, sl] = jnp.sqrt(1.0 - jnp.exp(2.0 * log_a)) * (i * uj)

    def step(t, h):
        h = a_ref[pl.ds(t, 1), :] * h + b_ref[pl.ds(t, 1), :]
        b_ref[pl.ds(t, 1), :] = h
        return h

    h_ref[0:1, :] = lax.fori_loop(0, tq, step, h_ref[0:1, :], unroll=8)
    out_ref[...] = (gate_ref[...].astype(F32) * b_ref[...]).astype(out_ref.dtype)


def lru_core(u_pre, gate, conv_w, conv_b, w_a, b_a, w_x, b_x, lam, bsz, seq, *, tq=LRU_TQ):
    n, width = u_pre.shape
    tq = min(tq, seq)
    nblk = seq // tq
    per = tq // SUBLANES
    row = lambda b, s: (b * nblk + s, 0)
    full = lambda arr: pl.BlockSpec(arr.shape, lambda b, s: (0,) * arr.ndim)
    return pl.pallas_call(
        functools.partial(_lru_body, tq=tq, width=width),
        grid=(bsz, nblk),
        in_specs=[
            pl.BlockSpec((tq, width), row),
            pl.BlockSpec((SUBLANES, width), lambda b, s: (jnp.maximum((b * nblk + s) * per - 1, 0), 0)),
            pl.BlockSpec((tq, width), row),
            full(conv_w), full(conv_b), full(w_a), full(b_a), full(w_x), full(b_x), full(lam),
        ],
        out_specs=pl.BlockSpec((tq, width), row),
        out_shape=jax.ShapeDtypeStruct((n, width), BF16),
        scratch_shapes=[pltpu.VMEM((SUBLANES, width), F32), pltpu.VMEM((tq, width), F32),
                        pltpu.VMEM((tq, width), F32)],
        compiler_params=_cparams(("arbitrary", "arbitrary")),
        name="lru_core",
    )(u_pre, u_pre, gate, conv_w, conv_b, w_a, b_a, w_x, b_x, lam)


def rglru_layer(h, hb, bsz, seq, w_in, conv_w, conv_b, w_a, b_a, w_x, b_x, lam, w_out, ln_g, ln_b, alpha):
    width = w_in.shape[1] // 2
    w_in = w_in.astype(BF16)
    gate = proj(hb, w_in[:, :width], BF16, "lru_proj_gate", post=_post_gelu)
    u_pre = proj(hb, w_in[:, width:], F32, "lru_proj_u")
    z = lru_core(u_pre, gate, conv_w, conv_b.reshape(1, -1), w_a.astype(BF16), b_a.reshape(1, -1),
                 w_x.astype(BF16), b_x.reshape(1, -1), lam.reshape(1, -1), bsz, seq)
    return out_proj_ln(z, w_out.astype(BF16), h, ln_g, ln_b, alpha)


def kernel(x, positions, rwkv_mu, rwkv_w_in, rwkv_w0, rwkv_w1, rwkv_w2, rwkv_a0, rwkv_a1, rwkv_a2, rwkv_g1, rwkv_g2, rwkv_k_k, rwkv_k_a, rwkv_r_k, rwkv_ln_g, rwkv_ln_b, rwkv_w_out, ssd_w_in, ssd_conv_w, ssd_conv_b, ssd_dt_bias, ssd_a_log, ssd_d, ssd_norm_g, ssd_w_out, dif_w_in, dif_lq1, dif_lk1, dif_lq2, dif_lk2, dif_subln_g, dif_w_out, lru_w_in, lru_conv_w, lru_conv_b, lru_w_a, lru_b_a, lru_w_x, lru_b_x, lru_lam, lru_w_out, ffn_w_in, ffn_w_out, ln_g, ln_b):
    bsz, seq, d = x.shape
    depth = ffn_w_in.shape[0]
    alpha = (2 * depth) ** 0.25
    h = x.reshape(bsz * seq, d)
    hb = None
    for i in range(depth):
        m = i % 4
        g0, b0 = ln_g[i, 0].reshape(1, d), ln_b[i, 0].reshape(1, d)
        g1, b1 = ln_g[i, 1].reshape(1, d), ln_b[i, 1].reshape(1, d)
        if m == 0:
            h, hb = rwkv_layer(h, bsz, seq, rwkv_mu, rwkv_w_in, rwkv_w0, rwkv_w1, rwkv_w2, rwkv_a0, rwkv_a1,
                               rwkv_a2, rwkv_g1, rwkv_g2, rwkv_k_k, rwkv_k_a, rwkv_r_k, rwkv_ln_g, rwkv_ln_b,
                               rwkv_w_out, g0, b0, alpha)
        elif m == 1:
            h, hb = ssd_layer(h, hb, bsz, seq, ssd_w_in, ssd_conv_w, ssd_conv_b, ssd_dt_bias, ssd_a_log, ssd_d,
                              ssd_norm_g, ssd_w_out, g0, b0, alpha)
        elif m == 2:
            h, hb = diff_layer(h, hb, positions, bsz, seq, dif_w_in, dif_lq1, dif_lk1, dif_lq2, dif_lk2,
                               dif_subln_g, dif_w_out, g0, b0, alpha)
        else:
            h, hb = rglru_layer(h, hb, bsz, seq, lru_w_in, lru_conv_w, lru_conv_b, lru_w_a, lru_b_a, lru_w_x,
                                lru_b_x, lru_lam, lru_w_out, g0, b0, alpha)
        h, hb = ffn_ln(h, hb, ffn_w_in[i].astype(BF16), ffn_w_out[i].astype(BF16), g1, b1, alpha)
    return h.reshape(bsz, seq, d)
```

```python
import functools
import math

import numpy as np
import jax
import jax.numpy as jnp
from jax import lax
from jax.experimental import pallas as pl
from jax.experimental.pallas import tpu as pltpu

F32 = jnp.float32
BF16 = jnp.bfloat16

LANES = 128
SUBLANES = 8
BF16_ROWS = 16
VMEM_LIMIT_BYTES = 56 * 1024 * 1024

PROJ_TM = 1024
PROJ_TN = 1024
FFN_TM = 512
FFN_TF = 512
FFN_ACT_TM = 2048
FFN_ACT_TF = 512
FFN_ACT_SUB = 512
FFN_DOWN_TK = 2816
OUT_PROJ_TM = 512
OUT_PROJ_TK = 2048
OUT_PROJ_SUB = 128
RWKV_PROJ_TM = 512
RWKV_LORA_TM = 256
RW_PAIRS_PER_STEP = 16
SSD_TQ = 256
LRU_TQ = 512
ROPE_TM = 1024
ATTN_TQ = 2048
ATTN_TK = 2048
ATTN_SUB_ROWS = 256

LN_EPS = 1e-5
CHUNK = 64
CHUNK_SHIFT = 6
RW_HEAD = 64
RW_GN_EPS = 64e-5
SSD_HEAD = 64
SSD_GROUPS = 8
SSD_STATE = 128
DIF_HEADS = 8
ROPE_THETA = 500000.0
DIF_LAYER = 2
DIF_LAMBDA_INIT = 0.8 - 0.6 * math.exp(-0.3 * DIF_LAYER)
LRU_BLOCKS = 16
LRU_C = 8.0
NEG_BIG = -1e30


def _cparams(sem):
    return pltpu.CompilerParams(dimension_semantics=sem, vmem_limit_bytes=VMEM_LIMIT_BYTES)


def _dot(a, b):
    return jnp.dot(a, b, preferred_element_type=F32)


def _dot_nt(a, b):
    return lax.dot_general(a, b, (((1,), (1,)), ((), ())), preferred_element_type=F32)


def _dot_tn(a, b):
    return lax.dot_general(a, b, (((0,), (0,)), ((), ())), preferred_element_type=F32)


def _dot_01(m01, x):
    hi = x.astype(BF16)
    r1 = x - hi.astype(F32)
    mid = r1.astype(BF16)
    low = (r1 - mid.astype(F32)).astype(BF16)
    return _dot(m01, hi) + _dot(m01, mid) + _dot(m01, low)


def _half_sums(lo_mask, x):
    s_lo = jnp.sum(jnp.where(lo_mask, x, 0.0), axis=-1, keepdims=True)
    s_hi = jnp.sum(jnp.where(lo_mask, 0.0, x), axis=-1, keepdims=True)
    return jnp.where(lo_mask, s_lo, s_hi)


def _layer_norm(t, g, b):
    mu = jnp.mean(t, -1, keepdims=True)
    d = t - mu
    var = jnp.mean(d * d, -1, keepdims=True)
    return d * lax.rsqrt(var + LN_EPS) * g + b


def _neg_softplus_neg(z):
    return jnp.minimum(z, 0.0) - jnp.log(1.0 + jnp.exp(-jnp.abs(z)))


def _softplus(z):
    return jnp.maximum(z, 0.0) + jnp.log(1.0 + jnp.exp(-jnp.abs(z)))


def _silu(z):
    return z * jax.nn.sigmoid(z)


def _shifted(x, halo, d):
    head = pltpu.roll(jnp.concatenate([halo, x[:SUBLANES]], axis=0), d, axis=0)[SUBLANES:]
    if x.shape[0] == SUBLANES:
        return head
    return jnp.concatenate([head, pltpu.roll(x, d, axis=0)[SUBLANES:]], axis=0)


def _causal_conv4(x, halo, cw, cb):
    out = x * cw[3:4, :] + cb
    for d in (1, 2, 3):
        out = out + _shifted(x, halo, d) * cw[3 - d:4 - d, :]
    return out


def _proj_body(*refs, n_bands, post):
    a, w = refs[0], refs[1]
    bands = refs[2:2 + n_bands]
    out = refs[2 + n_bands]
    out[...] = post(_dot(a[...], w[...]), *[r[...] for r in bands]).astype(out.dtype)


def proj(hb, w, out_dtype, name, *, post=lambda acc: acc, bands=()):
    m, kdim = hb.shape
    n = w.shape[1]
    tm, tn = min(PROJ_TM, m), min(PROJ_TN, n)
    assert m % tm == 0 and n % tn == 0
    in_specs = [pl.BlockSpec((tm, kdim), lambda i, j: (i, 0)), pl.BlockSpec((kdim, tn), lambda i, j: (0, j))]
    in_specs += [pl.BlockSpec((tm, b.shape[1]), lambda i, j: (i, 0)) for b in bands]
    return pl.pallas_call(
        functools.partial(_proj_body, n_bands=len(bands), post=post),
        grid=(m // tm, n // tn),
        in_specs=in_specs,
        out_specs=pl.BlockSpec((tm, tn), lambda i, j: (i, j)),
        out_shape=jax.ShapeDtypeStruct((m, n), out_dtype),
        compiler_params=_cparams(("parallel", "parallel")),
        name=name,
    )(hb, w, *bands)


def _out_proj_body(z_ref, w_ref, h_ref, g_ref, b_ref, out_ref, outb_ref, *scratch, nk, alpha, tm):
    def finish():
        for r in range(tm // OUT_PROJ_SUB):
            rows = slice(r * OUT_PROJ_SUB, (r + 1) * OUT_PROJ_SUB)
            t = _dot(z_ref[rows, :], w_ref[...])
            if nk > 1:
                t = t + scratch[0][rows, :]
            o = _layer_norm(alpha * h_ref[rows, :] + t, g_ref[...], b_ref[...])
            out_ref[rows, :] = o
            outb_ref[rows, :] = o.astype(BF16)

    if nk == 1:
        finish()
    else:
        acc = scratch[0]
        k = pl.program_id(1)

        @pl.when(k == 0)
        def _():
            acc[...] = _dot(z_ref[...], w_ref[...])

        @pl.when(jnp.logical_and(k > 0, k < nk - 1))
        def _():
            acc[...] += _dot(z_ref[...], w_ref[...])

        pl.when(k == nk - 1)(finish)


def out_proj_ln(z, w, h, g, b, alpha, *, tm=OUT_PROJ_TM, tk=OUT_PROJ_TK, name="out_proj_ln"):
    m, kdim = z.shape
    n = w.shape[1]
    tm = min(tm, m)
    nk = kdim // tk
    assert kdim % tk == 0 and m % tm == 0 and tm % OUT_PROJ_SUB == 0
    row = pl.BlockSpec((tm, n), lambda i, k: (i, 0))
    vec = pl.BlockSpec((1, n), lambda i, k: (0, 0))
    return pl.pallas_call(
        functools.partial(_out_proj_body, nk=nk, alpha=alpha, tm=tm),
        grid=(m // tm, nk),
        in_specs=[pl.BlockSpec((tm, tk), lambda i, k: (i, k)), pl.BlockSpec((tk, n), lambda i, k: (k, 0)),
                  row, vec, vec],
        out_specs=[row, row],
        out_shape=[jax.ShapeDtypeStruct((m, n), F32), jax.ShapeDtypeStruct((m, n), BF16)],
        scratch_shapes=[pltpu.VMEM((tm, n), F32)] if nk > 1 else [],
        compiler_params=_cparams(("parallel", "arbitrary")),
        name=name,
    )(z, w, h, g, b)


def _ffn_act_body(xb_ref, wg_ref, wu_ref, out_ref, *, tm):
    for r in range(tm // FFN_ACT_SUB):
        rows = slice(r * FFN_ACT_SUB, (r + 1) * FFN_ACT_SUB)
        xb = xb_ref[rows, :]
        out_ref[rows, :] = (_silu(_dot(xb, wg_ref[...])) * _dot(xb, wu_ref[...])).astype(out_ref.dtype)


def ffn_act(xb, w_in, dff):
    m, d = xb.shape
    tm, tf = min(FFN_ACT_TM, m), FFN_ACT_TF
    nf = dff // tf
    assert dff % tf == 0 and m % tm == 0 and tm % FFN_ACT_SUB == 0
    return pl.pallas_call(
        functools.partial(_ffn_act_body, tm=tm),
        grid=(m // tm, nf),
        in_specs=[pl.BlockSpec((tm, d), lambda i, f: (i, 0)),
                  pl.BlockSpec((d, tf), lambda i, f: (0, f)),
                  pl.BlockSpec((d, tf), lambda i, f: (0, nf + f))],
        out_specs=pl.BlockSpec((tm, tf), lambda i, f: (i, f)),
        out_shape=jax.ShapeDtypeStruct((m, dff), BF16),
        compiler_params=_cparams(("parallel", "parallel")),
        name="ffn_act",
    )(xb, w_in, w_in)


def ffn_split_ln(x, xb, w_in, w_out, g, b, alpha):
    act = ffn_act(xb, w_in, w_out.shape[0])
    return out_proj_ln(act, w_out, x, g, b, alpha, tk=FFN_DOWN_TK, name="ffn_down_ln")


def _ffn_body(x_ref, xb_ref, wg_ref, wu_ref, wo_ref, g_ref, b_ref, out_ref, outb_ref, acc_ref, *, nf, alpha):
    f = pl.program_id(1)

    @pl.when(f == 0)
    def _():
        acc_ref[...] = jnp.zeros_like(acc_ref)

    xb = xb_ref[...]
    gate = _dot(xb, wg_ref[...])
    up = _dot(xb, wu_ref[...])
    act = (_silu(gate) * up).astype(BF16)
    acc_ref[...] += _dot(act, wo_ref[...])

    @pl.when(f == nf - 1)
    def _():
        out = _layer_norm(alpha * x_ref[...] + acc_ref[...], g_ref[...], b_ref[...])
        out_ref[...] = out
        outb_ref[...] = out.astype(BF16)


def ffn_ln(x, xb, w_in, w_out, g, b, alpha, *, tm=FFN_TM, tf=FFN_TF):
    m, d = x.shape
    dff = w_out.shape[0]
    tm = min(tm, m)
    nf = dff // tf
    assert dff % tf == 0 and m % tm == 0
    row = pl.BlockSpec((tm, d), lambda i, f: (i, 0))
    return pl.pallas_call(
        functools.partial(_ffn_body, nf=nf, alpha=alpha),
        grid=(m // tm, nf),
        in_specs=[
            row, row,
            pl.BlockSpec((d, tf), lambda i, f: (0, f)),
            pl.BlockSpec((d, tf), lambda i, f: (0, nf + f)),
            pl.BlockSpec((tf, d), lambda i, f: (f, 0)),
            pl.BlockSpec((1, d), lambda i, f: (0, 0)),
            pl.BlockSpec((1, d), lambda i, f: (0, 0)),
        ],
        out_specs=[row, row],
        out_shape=[jax.ShapeDtypeStruct((m, d), F32), jax.ShapeDtypeStruct((m, d), BF16)],
        scratch_shapes=[pltpu.VMEM((tm, d), F32)],
        compiler_params=_cparams(("parallel", "arbitrary")),
        name="ffn_ln",
    )(x, xb, w_in, w_in, w_out, g, b)


def _rwkv_proj_body(x_ref, halo_ref, mu_ref, w_ref, out_ref, *, tm, seq):
    i = pl.program_id(0)
    x = x_ref[...]
    halo = jnp.where((i * tm) % seq == 0, 0.0, halo_ref[...])
    xx = _shifted(x, halo, 1) - x
    lhs = (x + xx * mu_ref[...]).astype(BF16)
    out_ref[...] = _dot(lhs, w_ref[...]).astype(out_ref.dtype)


def rwkv_proj(x, mu3, w3, seq, *, tm=RWKV_PROJ_TM):
    m, d = x.shape
    n = w3.shape[2]
    tm = min(tm, seq)
    per = tm // SUBLANES
    return pl.pallas_call(
        functools.partial(_rwkv_proj_body, tm=tm, seq=seq),
        grid=(m // tm, 3),
        in_specs=[
            pl.BlockSpec((tm, d), lambda i, j: (i, 0)),
            pl.BlockSpec((SUBLANES, d), lambda i, j: (jnp.maximum(i * per - 1, 0), 0)),
            pl.BlockSpec((None, 1, d), lambda i, j: (j, 0, 0)),
            pl.BlockSpec((None, d, n), lambda i, j: (j, 0, 0)),
        ],
        out_specs=pl.BlockSpec((None, tm, n), lambda i, j: (j, i, 0)),
        out_shape=jax.ShapeDtypeStruct((3, m, n), BF16),
        compiler_params=_cparams(("parallel", "arbitrary")),
        name="rwkv_proj",
    )(x, x, mu3, w3)


def _rwkv_lora_body(x_ref, halo_ref, mu_ref, w1_ref, w2_ref, a1_ref, a2_ref, g1_ref, g2_ref, w0_ref, a0_ref,
                    wl_ref, a_ref, g_ref, *, tm, seq):
    i = pl.program_id(0)
    x = x_ref[...]
    halo = jnp.where((i * tm) % seq == 0, 0.0, halo_ref[...])
    xx = _shifted(x, halo, 1) - x
    mu = mu_ref[...]
    xw = (x + xx * mu[0:1, :]).astype(BF16)
    xa = (x + xx * mu[1:2, :]).astype(BF16)
    xg = (x + xx * mu[2:3, :]).astype(BF16)
    zw = w0_ref[...] + _dot(jnp.tanh(_dot(xw, w1_ref[...])).astype(BF16), w2_ref[...])
    wl_ref[...] = _neg_softplus_neg(zw) - 0.5
    a_ref[...] = jax.nn.sigmoid(a0_ref[...] + _dot(_dot(xa, a1_ref[...]).astype(BF16), a2_ref[...]))
    g_ref[...] = _dot(jax.nn.sigmoid(_dot(xg, g1_ref[...])).astype(BF16), g2_ref[...]).astype(g_ref.dtype)


def rwkv_lora(x, mu3, w1, w2, a1, a2, g1, g2, w0, a0, seq, *, tm=RWKV_LORA_TM):
    m, d = x.shape
    tm = min(tm, seq)
    per = tm // SUBLANES
    full = lambda arr: pl.BlockSpec(arr.shape, lambda i: (0,) * arr.ndim)
    row = pl.BlockSpec((tm, d), lambda i: (i, 0))
    return pl.pallas_call(
        functools.partial(_rwkv_lora_body, tm=tm, seq=seq),
        grid=(m // tm,),
        in_specs=[row, pl.BlockSpec((SUBLANES, d), lambda i: (jnp.maximum(i * per - 1, 0), 0)),
                  full(mu3), full(w1), full(w2), full(a1), full(a2), full(g1), full(g2), full(w0), full(a0)],
        out_specs=[row, row, row],
        out_shape=[jax.ShapeDtypeStruct((m, d), F32), jax.ShapeDtypeStruct((m, d), F32),
                   jax.ShapeDtypeStruct((m, d), BF16)],
        compiler_params=_cparams(("parallel",)),
        name="rwkv_lora",
    )(x, x, mu3, w1, w2, a1, a2, g1, g2, w0, a0)


RW_LEVELS = 6


def _rwkv_masks():
    n = 2 * CHUNK
    t = np.arange(n)[:, None]
    s = np.arange(n)[None, :]
    strict = (t > s).astype(np.float32)
    incl = (t >= s).astype(np.float32)
    tri = np.stack([strict, incl])
    lvl = []
    for lv in range(RW_LEVELS):
        sz = 1 << lv
        lvl.append(((t // (2 * sz) == s // (2 * sz)) & ((t // sz) % 2 == 1) & ((s // sz) % 2 == 0)).astype(np.float32))
    t64 = incl[:CHUNK, :CHUNK]
    return jnp.asarray(tri), jnp.asarray(np.stack(lvl)), jnp.asarray(t64)


def _rwkv_core_body(r_ref, k_ref, v_ref, wl_ref, a_ref, g_ref, kk_ref, ka_ref, rk_ref, lng_ref, lnb_ref,
                    tri_ref, lvl_ref, t64_ref, out_ref, st_ref):
    c = pl.program_id(2)

    @pl.when(c == 0)
    def _():
        st_ref[...] = jnp.zeros_like(st_ref)

    strict, incl = tri_ref[0], tri_ref[1]
    eye = incl - strict
    t64 = t64_ref[...].astype(BF16)
    lo = lax.broadcasted_iota(jnp.int32, (1, LANES), 1) < RW_HEAD
    inv_head = 1.0 / RW_HEAD

    def stack(x):
        return jnp.concatenate([jnp.where(lo, x, 0.0), jnp.where(lo, 0.0, x)], axis=0).astype(BF16)

    pairs = range(RW_PAIRS_PER_STEP)
    sls = [slice(q * LANES, (q + 1) * LANES) for q in pairs]

    ld = lambda ref, sl: ref[:, sl].astype(F32)
    kk_sq = []
    for sl in sls:
        kk = ld(k_ref, sl) * kk_ref[:, sl]
        kk_sq.append(_half_sums(lo, kk * kk))
    cums = [_dot_01(t64, -jnp.exp(wl_ref[:, sl])) for sl in sls]

    ops = []
    for sl, ss, cum in zip(sls, kk_sq, cums):
        k, a = ld(k_ref, sl), a_ref[:, sl]
        kk = k * kk_ref[:, sl] / jnp.maximum(jnp.sqrt(ss), 1e-12)
        k2 = k * (1.0 + (a - 1.0) * ka_ref[:, sl])
        lw = -jnp.exp(wl_ref[:, sl])
        cl = cum[CHUNK - 1:CHUNK, :]
        e_neg = jnp.exp(-cum)
        e_end = jnp.exp(cl - cum)
        kb = kk * a
        ops.append(dict(
            a_s=stack(-kk * jnp.exp(cum - lw)),
            r_s=stack(ld(r_ref, sl) * jnp.exp(cum)),
            b_s=stack(kb * e_neg), k_s=stack(k2 * e_neg), v_s=stack(ld(v_ref, sl)),
            bh_s=stack(kb * e_end), kh_s=stack(k2 * e_end),
            w_end=jnp.exp(cl), k2=k2))

    n2 = 2 * CHUNK
    quads = [(2 * i, 2 * i + 1) for i in range(RW_PAIRS_PER_STEP // 2)]
    zero = jnp.zeros((n2, n2), BF16)
    rows2 = lambda top, bot: jnp.concatenate([top, bot], axis=0)
    side = lambda xa, xb: jnp.concatenate([xa, xb], axis=1)
    bdiag = lambda xa, xb: rows2(side(xa, zero), side(zero, xb))
    half = lambda x, j: x[:, j * n2:(j + 1) * n2]
    col = lambda key, q: side(ops[q[0]][key], ops[q[1]][key])

    gmask = rows2(side(strict, strict), side(incl, incl))
    gram = [gmask * _dot_nt(rows2(o["a_s"], o["r_s"]), rows2(o["b_s"], o["k_s"])) for o in ops]
    a_ab = [g[:n2, :n2] for g in gram]
    a_ak = [g[:n2, n2:].astype(BF16) for g in gram]
    r_b = [g[n2:, :n2].astype(BF16) for g in gram]
    r_k = [g[n2:, n2:].astype(BF16) for g in gram]

    t_inv = [eye + m * lvl_ref[0] for m in a_ab]
    for lv in range(1, RW_LEVELS):
        tb = [t.astype(BF16) for t in t_inv]
        off = [(m * lvl_ref[lv]).astype(BF16) for m in a_ab]
        xs = [_dot(side(tb[a], tb[b]), bdiag(off[a], off[b])).astype(BF16) for a, b in quads]
        upd = [_dot(x, bdiag(tb[a], tb[b])) for x, (a, b) in zip(xs, quads)]
        t_inv = [t + half(upd[p // 2], p % 2) for p, t in enumerate(t_inv)]
    tb = [t.astype(BF16) for t in t_inv]

    s_old = [st_ref[q] for q in pairs]
    s_b = [s.astype(BF16) for s in s_old]
    p12 = [_dot_nt(rows2(col("a_s", q), col("r_s", q)), bdiag(s_b[q[0]], s_b[q[1]]))
           + _dot(rows2(side(a_ak[q[0]], a_ak[q[1]]), side(r_k[q[0]], r_k[q[1]])), bdiag(ops[q[0]]["v_s"], ops[q[1]]["v_s"]))
           for q in quads]
    rhs = [p[:n2].astype(BF16) for p in p12]
    u_b = [_dot(side(tb[a], tb[b]), bdiag(half(x, 0), half(x, 1))).astype(BF16) for x, (a, b) in zip(rhs, quads)]
    y_q = [p[n2:] + _dot(side(r_b[a], r_b[b]), bdiag(half(u, 0), half(u, 1))) for p, u, (a, b) in zip(p12, u_b, quads)]
    inc = [_dot_tn(rows2(u, col("v_s", q)), rows2(col("bh_s", q), col("kh_s", q))) for u, q in zip(u_b, quads)]
    for z, (a, b) in zip(inc, quads):
        st_ref[a] = s_old[a] * ops[a]["w_end"] + z[:n2, :n2]
        st_ref[b] = s_old[b] * ops[b]["w_end"] + z[n2:, n2:]

    y_s = [half(y_q[p // 2], p % 2) for p in pairs]
    ys = [y[:CHUNK] + y[CHUNK:] for y in y_s]
    mus = [_half_sums(lo, y) * inv_head for y in ys]
    ds = [y - mu for y, mu in zip(ys, mus)]
    var = [_half_sums(lo, d * d) * inv_head for d in ds]
    bonus = [_half_sums(lo, ld(r_ref, sl) * o["k2"] * rk_ref[:, sl]) for sl, o in zip(sls, ops)]
    for sl, d, vr, bn in zip(sls, ds, var, bonus):
        yn = d * lax.rsqrt(vr + RW_GN_EPS) * lng_ref[:, sl] + lnb_ref[:, sl]
        out_ref[:, sl] = ((yn + bn * ld(v_ref, sl)) * ld(g_ref, sl)).astype(out_ref.dtype)


def rwkv_core(rkv, wl, a, g, k_k, k_a, r_k, ln_g, ln_b, bsz, seq):
    _, m, d = rkv.shape
    width = RW_PAIRS_PER_STEP * LANES
    nc = seq // CHUNK
    tri, lvl, t64 = _rwkv_masks()
    row = lambda b, p, c: (b * nc + c, p)
    rkv_spec = lambda j: pl.BlockSpec((None, CHUNK, width), lambda b, p, c: (j, b * nc + c, p))
    act = pl.BlockSpec((CHUNK, width), row)
    par = pl.BlockSpec((1, width), lambda b, p, c: (0, p))
    full = lambda arr: pl.BlockSpec(arr.shape, lambda b, p, c: (0,) * arr.ndim)
    return pl.pallas_call(
        _rwkv_core_body,
        grid=(bsz, d // width, nc),
        in_specs=[rkv_spec(0), rkv_spec(1), rkv_spec(2), act, act, act, par, par, par, par, par,
                  full(tri), full(lvl), full(t64)],
        out_specs=act,
        out_shape=jax.ShapeDtypeStruct((m, d), BF16),
        scratch_shapes=[pltpu.VMEM((RW_PAIRS_PER_STEP, LANES, LANES), F32)],
        compiler_params=_cparams(("parallel", "parallel", "arbitrary")),
        name="rwkv_core",
    )(rkv, rkv, rkv, wl, a, g, k_k, k_a, r_k, ln_g, ln_b, tri, lvl, t64)


def _pad_to(x, axis, size):
    pad = [(0, 0)] * x.ndim
    pad[axis] = (0, size - x.shape[axis])
    return jnp.pad(x, pad)


def rwkv_layer(h, bsz, seq, mu, w_in, w0, w1, w2, a0, a1, a2, g1, g2, k_k, k_a, r_k, gn_g, gn_b, w_out,
               ln_g, ln_b, alpha):
    d = h.shape[1]
    v2 = lambda t: t.reshape(1, d)
    mu_rkv = jnp.stack([mu[0], mu[2], mu[3]]).reshape(3, 1, d)
    mu_lora = jnp.stack([mu[1], mu[4], mu[5]])
    rkv = rwkv_proj(h, mu_rkv, w_in.astype(BF16), seq)
    wl, a, g = rwkv_lora(
        h, mu_lora,
        _pad_to(w1, 1, LANES).astype(BF16), _pad_to(w2, 0, LANES).astype(BF16),
        _pad_to(a1, 1, LANES).astype(BF16), _pad_to(a2, 0, LANES).astype(BF16),
        g1.astype(BF16), g2.astype(BF16), v2(w0), v2(a0), seq)
    z = rwkv_core(rkv, wl, a, g, v2(k_k), v2(k_a), v2(r_k), v2(gn_g), v2(gn_b), bsz, seq)
    return out_proj_ln(z, w_out.astype(BF16), h, ln_g, ln_b, alpha)


def _ssd_consts(heads):
    lanes = heads * SSD_HEAD
    head_of = np.arange(lanes) // SSD_HEAD
    pos_of = np.arange(lanes) % SSD_HEAD
    expand = (np.arange(LANES)[:, None] == head_of[None, :]).astype(np.float32)
    eye = (np.arange(CHUNK)[:, None] == pos_of[None, :]).astype(np.float32)
    tril = (np.arange(CHUNK)[:, None] >= pos_of[None, :]).astype(np.float32)
    return jnp.asarray(expand, BF16), jnp.asarray(eye), jnp.asarray(tril)


def _expand_heads(x, e01):
    hi = x.astype(BF16)
    low = (x - hi.astype(F32)).astype(BF16)
    return _dot(hi, e01) + _dot(low, e01)


def _ssd_body(xbc_ref, halo_ref, z_ref, dtp_ref, cw_ref, cb_ref, dtb_ref, alog_ref, dsk_ref, ng_ref, t64_ref,
              e01_ref, eye_ref, tril_ref, out_ref, st_ref, xc_ref, dt_ref, da_ref, *, tq, d_inner):
    s = pl.program_id(1)

    @pl.when(s == 0)
    def _():
        st_ref[...] = jnp.zeros_like(st_ref)

    halo = jnp.where(s == 0, 0.0, halo_ref[...].astype(F32)[BF16_ROWS - SUBLANES:, :])
    xc_ref[...] = _silu(_causal_conv4(xbc_ref[...].astype(F32), halo, cw_ref[...], cb_ref[...]))
    dt_all = _softplus(dtp_ref[...] + dtb_ref[...])
    dt_ref[...] = dt_all
    da_ref[...] = dt_all * (-jnp.exp(alog_ref[...]))

    t64b = t64_ref[...].astype(BF16)
    lo = lax.broadcasted_iota(jnp.int32, (1, LANES), 1) < SSD_HEAD
    gn = SSD_GROUPS * SSD_STATE
    gw = d_inner // SSD_GROUPS
    pairs_per_group = gw // LANES

    def chunk(ci, carry):
        r0 = pl.multiple_of(ci * CHUNK, CHUNK)
        rows = pl.ds(r0, CHUNK)
        e01 = e01_ref[...]
        cs = _dot_01(t64b, da_ref[rows, :])
        dte = _expand_heads(dt_ref[rows, :], e01)
        cse = _expand_heads(cs, e01)
        cs_row = jnp.sum(eye_ref[...] * cse, axis=0, keepdims=True)
        decay = jnp.exp(jnp.minimum(cse - cs_row, 0.0)) * tril_ref[...]
        cs_last = cse[CHUNK - 1:CHUNK, :]
        ecs = jnp.exp(cse)
        to_end = jnp.exp(cs_last - cse)
        e_last = jnp.exp(cs_last)
        for g in range(SSD_GROUPS):
            gsl = slice(g * gw, (g + 1) * gw)
            bg = xc_ref[rows, d_inner + g * SSD_STATE:d_inner + (g + 1) * SSD_STATE]
            cg = xc_ref[rows, d_inner + gn + g * SSD_STATE:d_inner + gn + (g + 1) * SSD_STATE].astype(BF16)
            bgb = bg.astype(BF16)
            cb_rep = _dot_nt(cg, jnp.concatenate([bgb] * (gw // SSD_HEAD), axis=0))
            mat = (cb_rep * decay[:, gsl]).astype(BF16)
            bg_t = bg.T.astype(BF16)
            ps = [g * pairs_per_group + q for q in range(pairs_per_group)]
            sls = [slice(p * LANES, (p + 1) * LANES) for p in ps]
            xs = [xc_ref[rows, sl] for sl in sls]
            xdts = [x * dte[:, sl] for x, sl in zip(xs, sls)]
            x_st = [jnp.concatenate([jnp.where(lo, xdt, 0.0), jnp.where(lo, 0.0, xdt)], axis=0).astype(BF16)
                    for xdt in xdts]
            x_end = [(xdt * to_end[:, sl]).astype(BF16) for xdt, sl in zip(xdts, sls)]
            sts = [st_ref[p] for p in ps]
            intra = [_dot(mat[:, q * LANES:(q + 1) * LANES], xs_) for q, xs_ in enumerate(x_st)]
            inter = [_dot(cg, st.astype(BF16)) for st in sts]
            upd = [_dot(bg_t, xe) for xe in x_end]
            for p, sl, st, u in zip(ps, sls, sts, upd):
                st_ref[p] = st * e_last[:, sl] + u
            ys = []
            ssq = jnp.zeros((CHUNK, 1), F32)
            for sl, x, ya, yb in zip(sls, xs, intra, inter):
                y = ya + yb * ecs[:, sl] + x * dsk_ref[:, sl]
                y = y * _silu(z_ref[rows, sl].astype(F32))
                ssq = ssq + jnp.sum(y * y, axis=-1, keepdims=True)
                ys.append(y)
            inv = lax.rsqrt(ssq * (1.0 / gw) + 1e-5)
            for sl, y in zip(sls, ys):
                out_ref[rows, sl] = (y * inv * ng_ref[:, sl]).astype(out_ref.dtype)
        return carry

    lax.fori_loop(0, tq // CHUNK, chunk, 0)


def ssd_core(xbc_pre, z, dt_pre, conv_w, conv_b, dt_bias, a_log, d_skip, norm_g, bsz, seq, *, tq=SSD_TQ):
    n, conv_dim = xbc_pre.shape
    d_inner = z.shape[1]
    tq = min(tq, seq)
    nblk = seq // tq
    per = tq // BF16_ROWS
    npairs = d_inner // LANES
    t64 = _rwkv_masks()[2]
    e01, eye, tril = _ssd_consts(d_inner // SSD_HEAD)
    row = lambda b, s: (b * nblk + s, 0)
    full = lambda arr: pl.BlockSpec(arr.shape, lambda b, s: (0,) * arr.ndim)
    return pl.pallas_call(
        functools.partial(_ssd_body, tq=tq, d_inner=d_inner),
        grid=(bsz, nblk),
        in_specs=[
            pl.BlockSpec((tq, conv_dim), row),
            pl.BlockSpec((BF16_ROWS, conv_dim), lambda b, s: (jnp.maximum((b * nblk + s) * per - 1, 0), 0)),
            pl.BlockSpec((tq, d_inner), row),
            pl.BlockSpec((tq, LANES), row),
            full(conv_w), full(conv_b), full(dt_bias), full(a_log), full(d_skip), full(norm_g), full(t64),
            full(e01), full(eye), full(tril),
        ],
        out_specs=pl.BlockSpec((tq, d_inner), row),
        out_shape=jax.ShapeDtypeStruct((n, d_inner), BF16),
        scratch_shapes=[pltpu.VMEM((npairs, SSD_STATE, LANES), F32), pltpu.VMEM((tq, conv_dim), F32),
                        pltpu.VMEM((tq, LANES), F32), pltpu.VMEM((tq, LANES), F32)],
        compiler_params=_cparams(("arbitrary", "arbitrary")),
        name="ssd_core",
    )(xbc_pre, xbc_pre, z, dt_pre, conv_w, conv_b, dt_bias, a_log, d_skip, norm_g, t64, e01, eye, tril)


def ssd_layer(h, hb, bsz, seq, w_in, conv_w, conv_b, dt_bias, a_log, d_skip, norm_g, w_out, ln_g, ln_b, alpha):
    d_inner = w_out.shape[0]
    conv_dim = conv_w.shape[1]
    w_in = w_in.astype(BF16)
    z = proj(hb, w_in[:, :d_inner], BF16, "ssd_proj_z")
    xbc_pre = proj(hb, w_in[:, d_inner:d_inner + conv_dim], BF16, "ssd_proj_xbc")
    dt_pre = proj(hb, _pad_to(w_in[:, d_inner + conv_dim:], 1, LANES), F32, "ssd_proj_dt")
    pad_row = lambda t: _pad_to(t.reshape(1, -1), 1, LANES)
    d_skip_lanes = jnp.repeat(d_skip, SSD_HEAD).reshape(1, -1)
    y = ssd_core(xbc_pre, z, dt_pre, conv_w, conv_b.reshape(1, -1), pad_row(dt_bias), pad_row(a_log),
                 d_skip_lanes, norm_g.reshape(1, -1), bsz, seq)
    return out_proj_ln(y, w_out.astype(BF16), h, ln_g, ln_b, alpha)


ROPE_HALF = 16


def _rope_table_body(pos_ref, freq_ref, cos_ref, sin_ref):
    ang = pos_ref[...].astype(F32) * freq_ref[...]
    lane = lax.broadcasted_iota(jnp.int32, (1, LANES), 1)
    sn = jnp.sin(ang)
    cos_ref[...] = jnp.cos(ang)
    sin_ref[...] = jnp.where(lane < ROPE_HALF, -sn, jnp.where(lane < 2 * ROPE_HALF, sn, 0.0))


def rope_tables(pos, freq, *, tm=ROPE_TM):
    n = pos.shape[0]
    tm = min(tm, n)
    band = pl.BlockSpec((tm, LANES), lambda i: (i, 0))
    return pl.pallas_call(
        _rope_table_body,
        grid=(n // tm,),
        in_specs=[pl.BlockSpec((tm, 1), lambda i: (i, 0)), pl.BlockSpec((1, LANES), lambda i: (0, 0))],
        out_specs=[band, band],
        out_shape=[jax.ShapeDtypeStruct((n, LANES), F32)] * 2,
        compiler_params=_cparams(("parallel",)),
        name="rope_tables",
    )(pos, freq)


def _post_rope(scale, acc, cos, sin):
    lane = lax.broadcasted_iota(jnp.int32, (1, LANES), 1)
    outs = []
    for j in range(acc.shape[1] // LANES):
        t = acc[:, j * LANES:(j + 1) * LANES]
        partner = jnp.where(lane < ROPE_HALF, pltpu.roll(t, LANES - ROPE_HALF, axis=1), pltpu.roll(t, ROPE_HALF, axis=1))
        outs.append((t * cos + partner * sin) * scale)
    return jnp.concatenate(outs, axis=1)


def _attn_body(qi_ref, ki_ref, last_ref, q_ref, k_ref, v_ref, lq1_ref, lk1_ref, lq2_ref, lk2_ref, sg_ref,
               out_ref, m_ref, l_ref, acc_ref, *, tq, tk, hd):
    p = pl.program_id(2)
    qi, ki = qi_ref[p], ki_ref[p]

    @pl.when(ki == 0)
    def _():
        m_ref[...] = jnp.full_like(m_ref, NEG_BIG)
        l_ref[...] = jnp.zeros_like(l_ref)
        acc_ref[...] = jnp.zeros_like(acc_ref)

    sub = min(ATTN_SUB_ROWS, tq)
    n_sub = tq // sub

    def update(masked):
        def n_keys(r):
            return (r + 1) * sub if masked and tq == tk else tk

        def qk(r):
            rows = slice(r * sub, (r + 1) * sub)
            return [_dot_nt(q_ref[rows, j * hd:(j + 1) * hd], k_ref[:n_keys(r), j * hd:(j + 1) * hd])
                    for j in range(2)]

        scores = qk(0)
        for r in range(n_sub):
            nxt = qk(r + 1) if r + 1 < n_sub else None
            rows = slice(r * sub, (r + 1) * sub)
            if masked:
                row0 = qi * tq + r * sub
                q_chunk = jnp.right_shift(row0 + lax.broadcasted_iota(jnp.int32, (sub, 1), 0), CHUNK_SHIFT)
                k_chunk = jnp.right_shift(ki * tk + lax.broadcasted_iota(jnp.int32, (1, n_keys(r)), 1), CHUNK_SHIFT)
                visible = k_chunk <= q_chunk
                scores = [jnp.where(visible, s, NEG_BIG) for s in scores]
            v = v_ref[:n_keys(r), :]
            for j in range(2):
                m_old = m_ref[j, rows, :]
                m_new = jnp.maximum(m_old, jnp.max(scores[j], axis=-1, keepdims=True))
                pe = jnp.exp2(scores[j] - m_new)
                scale = jnp.exp2(m_old - m_new)
                l_ref[j, rows, :] = scale * l_ref[j, rows, :] + jnp.sum(pe, axis=-1, keepdims=True)
                acc_ref[j, rows, :] = scale * acc_ref[j, rows, :] + _dot(pe.astype(BF16), v)
                m_ref[j, rows, :] = m_new
            scores = nxt

    needs_mask = (ki + 1) * tk > qi * tq + CHUNK

    @pl.when(needs_mask)
    def _():
        update(True)

    @pl.when(jnp.logical_not(needs_mask))
    def _():
        update(False)

    @pl.when(last_ref[p] == 1)
    def _():
        lam = (jnp.exp(jnp.sum(lq1_ref[...] * lk1_ref[...], axis=-1, keepdims=True))
               - jnp.exp(jnp.sum(lq2_ref[...] * lk2_ref[...], axis=-1, keepdims=True)) + DIF_LAMBDA_INIT)
        o = acc_ref[0] / l_ref[0] - lam * (acc_ref[1] / l_ref[1])
        o = o * lax.rsqrt(jnp.mean(o * o, axis=-1, keepdims=True) + 1e-5) * sg_ref[...]
        out_ref[...] = (o * (1.0 - DIF_LAMBDA_INIT)).astype(out_ref.dtype)


def diff_attention(q, k, v, lq1, lk1, lq2, lk2, subln_g, bsz, seq, *, tq=ATTN_TQ, tk=ATTN_TK):
    n, d = q.shape
    hw = d // DIF_HEADS
    hd = hw // 2
    tq, tk = min(tq, seq), min(tk, seq)
    assert tq % CHUNK == 0 and tk % CHUNK == 0 and CHUNK == 1 << CHUNK_SHIFT
    nq, nk = seq // tq, seq // tk
    n_kv = lambda a: -(-((a + 1) * tq) // tk)
    pairs = [(a, b) for a in range(nq) for b in range(n_kv(a))]
    qi = jnp.asarray([a for a, _ in pairs], jnp.int32)
    ki = jnp.asarray([b for _, b in pairs], jnp.int32)
    last = jnp.asarray([1 if b == n_kv(a) - 1 else 0 for a, b in pairs], jnp.int32)
    vec = pl.BlockSpec((1, hd), lambda b, h, p, qi, ki, last: (0, 0))
    grid_spec = pltpu.PrefetchScalarGridSpec(
        num_scalar_prefetch=3,
        grid=(bsz, DIF_HEADS, len(pairs)),
        in_specs=[
            pl.BlockSpec((tq, hw), lambda b, h, p, qi, ki, last: (b * nq + qi[p], h)),
            pl.BlockSpec((tk, hw), lambda b, h, p, qi, ki, last: (b * nk + ki[p], h)),
            pl.BlockSpec((tk, hw), lambda b, h, p, qi, ki, last: (b * nk + ki[p], h)),
            vec, vec, vec, vec,
            pl.BlockSpec((1, hw), lambda b, h, p, qi, ki, last: (0, 0)),
        ],
        out_specs=pl.BlockSpec((tq, hw), lambda b, h, p, qi, ki, last: (b * nq + qi[p], h)),
        scratch_shapes=[pltpu.VMEM((2, tq, 1), F32), pltpu.VMEM((2, tq, 1), F32), pltpu.VMEM((2, tq, hw), F32)],
    )
    return pl.pallas_call(
        functools.partial(_attn_body, tq=tq, tk=tk, hd=hd),
        grid_spec=grid_spec,
        out_shape=jax.ShapeDtypeStruct((n, d), BF16),
        compiler_params=_cparams(("parallel", "parallel", "arbitrary")),
        name="diff_attention",
    )(qi, ki, last, q, k, v, lq1, lk1, lq2, lk2, subln_g)


def diff_layer(h, hb, positions, bsz, seq, w_in, lq1, lk1, lq2, lk2, subln_g, w_out, ln_g, ln_b, alpha):
    d = h.shape[1]
    hd = d // DIF_HEADS // 2
    rope = hd // 4
    assert hd == LANES and rope == 2 * ROPE_HALF
    inv_freq = ROPE_THETA ** (-jnp.arange(0, rope, 2, dtype=F32) / rope)
    freq = jnp.concatenate([inv_freq, inv_freq, jnp.zeros((LANES - rope,), F32)]).reshape(1, LANES)
    cos, sin = rope_tables(positions.reshape(bsz * seq, 1), freq)
    w_in = w_in.astype(BF16)
    rope_mm = lambda w, scale, name: proj(hb, w, BF16, name, post=functools.partial(_post_rope, scale),
                                          bands=[cos, sin])
    q = rope_mm(w_in[:, :d], hd ** -0.5 * math.log2(math.e), "dif_proj_q")
    k = rope_mm(w_in[:, d:2 * d], 1.0, "dif_proj_k")
    v = proj(hb, w_in[:, 2 * d:], BF16, "dif_proj_v")
    v2 = lambda t: t.reshape(1, -1)
    o = diff_attention(q, k, v, v2(lq1), v2(lk1), v2(lq2), v2(lk2), v2(subln_g), bsz, seq)
    return out_proj_ln(o, w_out.astype(BF16), h, ln_g, ln_b, alpha)


def _post_gelu(acc):
    c = math.sqrt(2.0 / math.pi)
    return 0.5 * acc * (1.0 + jnp.tanh(c * (acc + 0.044715 * acc * acc * acc)))


def _lru_body(u_ref, halo_ref, gate_ref, cw_ref, cb_ref, wa_ref, ba_ref, wx_ref, bx_ref, lam_ref,
              out_ref, h_ref, a_ref, b_ref, *, tq, width):
    s = pl.program_id(1)

    @pl.when(s == 0)
    def _():
        h_ref[...] = jnp.zeros_like(h_ref)

    halo = jnp.where(s == 0, 0.0, halo_ref[...])
    u = _causal_conv4(u_ref[...], halo, cw_ref[...], cb_ref[...])

    nsp = _softplus(-lam_ref[...])
    blk = width // LRU_BLOCKS
    for j in range(LRU_BLOCKS):
        sl = slice(j * blk, (j + 1) * blk)
        uj = u[:, sl]
        ub = uj.astype(BF16)
        r = jax.nn.sigmoid(_dot(ub, wa_ref[j]) + ba_ref[:, sl])
        i = jax.nn.sigmoid(_dot(ub, wx_ref[j]) + bx_ref[:, sl])
        log_a = -LRU_C * r * nsp[:, sl]
        a_ref[:, sl] = jnp.exp(log_a)
        b_ref[:, sl] = jnp.sqrt(1.0 - jnp.exp(2.0 * log_a)) * (i * uj)

    def step(t, h):
        h = a_ref[pl.ds(t, 1), :] * h + b_ref[pl.ds(t, 1), :]
        b_ref[pl.ds(t, 1), :] = h
        return h

    h_ref[0:1, :] = lax.fori_loop(0, tq, step, h_ref[0:1, :], unroll=8)
    out_ref[...] = (gate_ref[...].astype(F32) * b_ref[...]).astype(out_ref.dtype)


def lru_core(u_pre, gate, conv_w, conv_b, w_a, b_a, w_x, b_x, lam, bsz, seq, *, tq=LRU_TQ):
    n, width = u_pre.shape
    tq = min(tq, seq)
    nblk = seq // tq
    per = tq // SUBLANES
    row = lambda b, s: (b * nblk + s, 0)
    full = lambda arr: pl.BlockSpec(arr.shape, lambda b, s: (0,) * arr.ndim)
    return pl.pallas_call(
        functools.partial(_lru_body, tq=tq, width=width),
        grid=(bsz, nblk),
        in_specs=[
            pl.BlockSpec((tq, width), row),
            pl.BlockSpec((SUBLANES, width), lambda b, s: (jnp.maximum((b * nblk + s) * per - 1, 0), 0)),
            pl.BlockSpec((tq, width), row),
            full(conv_w), full(conv_b), full(w_a), full(b_a), full(w_x), full(b_x), full(lam),
        ],
        out_specs=pl.BlockSpec((tq, width), row),
        out_shape=jax.ShapeDtypeStruct((n, width), BF16),
        scratch_shapes=[pltpu.VMEM((SUBLANES, width), F32), pltpu.VMEM((tq, width), F32),
                        pltpu.VMEM((tq, width), F32)],
        compiler_params=_cparams(("arbitrary", "arbitrary")),
        name="lru_core",
    )(u_pre, u_pre, gate, conv_w, conv_b, w_a, b_a, w_x, b_x, lam)


def rglru_layer(h, hb, bsz, seq, w_in, conv_w, conv_b, w_a, b_a, w_x, b_x, lam, w_out, ln_g, ln_b, alpha):
    width = w_in.shape[1] // 2
    w_in = w_in.astype(BF16)
    gate = proj(hb, w_in[:, :width], BF16, "lru_proj_gate", post=_post_gelu)
    u_pre = proj(hb, w_in[:, width:], F32, "lru_proj_u")
    z = lru_core(u_pre, gate, conv_w, conv_b.reshape(1, -1), w_a.astype(BF16), b_a.reshape(1, -1),
                 w_x.astype(BF16), b_x.reshape(1, -1), lam.reshape(1, -1), bsz, seq)
    return out_proj_ln(z, w_out.astype(BF16), h, ln_g, ln_b, alpha)


def kernel(x, positions, rwkv_mu, rwkv_w_in, rwkv_w0, rwkv_w1, rwkv_w2, rwkv_a0, rwkv_a1, rwkv_a2, rwkv_g1, rwkv_g2, rwkv_k_k, rwkv_k_a, rwkv_r_k, rwkv_ln_g, rwkv_ln_b, rwkv_w_out, ssd_w_in, ssd_conv_w, ssd_conv_b, ssd_dt_bias, ssd_a_log, ssd_d, ssd_norm_g, ssd_w_out, dif_w_in, dif_lq1, dif_lk1, dif_lq2, dif_lk2, dif_subln_g, dif_w_out, lru_w_in, lru_conv_w, lru_conv_b, lru_w_a, lru_b_a, lru_w_x, lru_b_x, lru_lam, lru_w_out, ffn_w_in, ffn_w_out, ln_g, ln_b):
    bsz, seq, d = x.shape
    depth = ffn_w_in.shape[0]
    alpha = (2 * depth) ** 0.25
    h = x.reshape(bsz * seq, d)
    hb = None
    for i in range(depth):
        m = i % 4
        g0, b0 = ln_g[i, 0].reshape(1, d), ln_b[i, 0].reshape(1, d)
        g1, b1 = ln_g[i, 1].reshape(1, d), ln_b[i, 1].reshape(1, d)
        if m == 0:
            h, hb = rwkv_layer(h, bsz, seq, rwkv_mu, rwkv_w_in, rwkv_w0, rwkv_w1, rwkv_w2, rwkv_a0, rwkv_a1,
                               rwkv_a2, rwkv_g1, rwkv_g2, rwkv_k_k, rwkv_k_a, rwkv_r_k, rwkv_ln_g, rwkv_ln_b,
                               rwkv_w_out, g0, b0, alpha)
        elif m == 1:
            h, hb = ssd_layer(h, hb, bsz, seq, ssd_w_in, ssd_conv_w, ssd_conv_b, ssd_dt_bias, ssd_a_log, ssd_d,
                              ssd_norm_g, ssd_w_out, g0, b0, alpha)
        elif m == 2:
            h, hb = diff_layer(h, hb, positions, bsz, seq, dif_w_in, dif_lq1, dif_lk1, dif_lq2, dif_lk2,
                               dif_subln_g, dif_w_out, g0, b0, alpha)
        else:
            h, hb = rglru_layer(h, hb, bsz, seq, lru_w_in, lru_conv_w, lru_conv_b, lru_w_a, lru_b_a, lru_w_x,
                                lru_b_x, lru_lam, lru_w_out, g0, b0, alpha)
        h, hb = ffn_split_ln(h, hb, ffn_w_in[i].astype(BF16), ffn_w_out[i].astype(BF16), g1, b1, alpha)
    return h.reshape(bsz, seq, d)
```

```python
import functools
import math

import numpy as np
import jax
import jax.numpy as jnp
from jax import lax
from jax.experimental import pallas as pl
from jax.experimental.pallas import tpu as pltpu

F32 = jnp.float32
BF16 = jnp.bfloat16

LANES = 128
SUBLANES = 8
BF16_ROWS = 16
VMEM_LIMIT_BYTES = 56 * 1024 * 1024

PROJ_TM = 2048
PROJ_TN = 1024
FFN_ACT_TM = 2048
FFN_ACT_TF = 512
FFN_ACT_SUB = 512
FFN_DOWN_TK = 2816
OUT_PROJ_TM = 512
OUT_PROJ_TK = 2048
OUT_PROJ_SUB = 128
RWKV_PROJ_TM = 1024
RWKV_LORA_TM = 256
RW_PAIRS_PER_STEP = 16
SSD_TQ = 256
LRU_TQ = 512
ROPE_TM = 1024
ATTN_TQ = 2048
ATTN_TK = 2048
ATTN_SUB_ROWS = 256

LN_EPS = 1e-5
CHUNK = 64
CHUNK_SHIFT = 6
RW_HEAD = 64
RW_GN_EPS = 64e-5
SSD_HEAD = 64
SSD_GROUPS = 8
SSD_STATE = 128
DIF_HEADS = 8
ROPE_THETA = 500000.0
DIF_LAYER = 2
DIF_LAMBDA_INIT = 0.8 - 0.6 * math.exp(-0.3 * DIF_LAYER)
LRU_BLOCKS = 16
LRU_C = 8.0
NEG_BIG = -1e30


def _cparams(sem):
    return pltpu.CompilerParams(dimension_semantics=sem, vmem_limit_bytes=VMEM_LIMIT_BYTES)


def _dot(a, b):
    return jnp.dot(a, b, preferred_element_type=F32)


def _dot_nt(a, b):
    return lax.dot_general(a, b, (((1,), (1,)), ((), ())), preferred_element_type=F32)


def _dot_tn(a, b):
    return lax.dot_general(a, b, (((0,), (0,)), ((), ())), preferred_element_type=F32)


def _dot_01(m01, x):
    hi = x.astype(BF16)
    r1 = x - hi.astype(F32)
    mid = r1.astype(BF16)
    low = (r1 - mid.astype(F32)).astype(BF16)
    return _dot(m01, hi) + _dot(m01, mid) + _dot(m01, low)


def _half_sums(lo_mask, x):
    s_lo = jnp.sum(jnp.where(lo_mask, x, 0.0), axis=-1, keepdims=True)
    s_hi = jnp.sum(jnp.where(lo_mask, 0.0, x), axis=-1, keepdims=True)
    return jnp.where(lo_mask, s_lo, s_hi)


def _layer_norm(t, g, b):
    mu = jnp.mean(t, -1, keepdims=True)
    d = t - mu
    var = jnp.mean(d * d, -1, keepdims=True)
    return d * lax.rsqrt(var + LN_EPS) * g + b


def _neg_softplus_neg(z):
    return jnp.minimum(z, 0.0) - jnp.log(1.0 + jnp.exp(-jnp.abs(z)))


def _softplus(z):
    return jnp.maximum(z, 0.0) + jnp.log(1.0 + jnp.exp(-jnp.abs(z)))


def _silu(z):
    return z * jax.nn.sigmoid(z)


def _shifted(x, halo, d):
    head = pltpu.roll(jnp.concatenate([halo, x[:SUBLANES]], axis=0), d, axis=0)[SUBLANES:]
    if x.shape[0] == SUBLANES:
        return head
    return jnp.concatenate([head, pltpu.roll(x, d, axis=0)[SUBLANES:]], axis=0)


def _causal_conv4(x, halo, cw, cb):
    out = x * cw[3:4, :] + cb
    for d in (1, 2, 3):
        out = out + _shifted(x, halo, d) * cw[3 - d:4 - d, :]
    return out


def _proj_body(*refs, n_bands, post):
    a, w = refs[0], refs[1]
    bands = refs[2:2 + n_bands]
    out = refs[2 + n_bands]
    out[...] = post(_dot(a[...], w[...]), *[r[...] for r in bands]).astype(out.dtype)


def proj(hb, w, out_dtype, name, *, post=lambda acc: acc, bands=()):
    m, kdim = hb.shape
    n = w.shape[1]
    tm, tn = min(PROJ_TM, m), min(PROJ_TN, n)
    assert m % tm == 0 and n % tn == 0
    in_specs = [pl.BlockSpec((tm, kdim), lambda i, j: (i, 0)), pl.BlockSpec((kdim, tn), lambda i, j: (0, j))]
    in_specs += [pl.BlockSpec((tm, b.shape[1]), lambda i, j: (i, 0)) for b in bands]
    return pl.pallas_call(
        functools.partial(_proj_body, n_bands=len(bands), post=post),
        grid=(m // tm, n // tn),
        in_specs=in_specs,
        out_specs=pl.BlockSpec((tm, tn), lambda i, j: (i, j)),
        out_shape=jax.ShapeDtypeStruct((m, n), out_dtype),
        compiler_params=_cparams(("parallel", "parallel")),
        name=name,
    )(hb, w, *bands)


def _out_proj_body(z_ref, w_ref, h_ref, g_ref, b_ref, out_ref, outb_ref, *scratch, nk, alpha, tm):
    def finish():
        for r in range(tm // OUT_PROJ_SUB):
            rows = slice(r * OUT_PROJ_SUB, (r + 1) * OUT_PROJ_SUB)
            t = _dot(z_ref[rows, :], w_ref[...])
            if nk > 1:
                t = t + scratch[0][rows, :]
            o = _layer_norm(alpha * h_ref[rows, :] + t, g_ref[...], b_ref[...])
            out_ref[rows, :] = o
            outb_ref[rows, :] = o.astype(BF16)

    if nk == 1:
        finish()
    else:
        acc = scratch[0]
        k = pl.program_id(1)

        @pl.when(k == 0)
        def _():
            acc[...] = _dot(z_ref[...], w_ref[...])

        @pl.when(jnp.logical_and(k > 0, k < nk - 1))
        def _():
            acc[...] += _dot(z_ref[...], w_ref[...])

        pl.when(k == nk - 1)(finish)


def out_proj_ln(z, w, h, g, b, alpha, *, tm=OUT_PROJ_TM, tk=OUT_PROJ_TK, name="out_proj_ln"):
    m, kdim = z.shape
    n = w.shape[1]
    tm = min(tm, m)
    nk = kdim // tk
    assert kdim % tk == 0 and m % tm == 0 and tm % OUT_PROJ_SUB == 0
    row = pl.BlockSpec((tm, n), lambda i, k: (i, 0))
    vec = pl.BlockSpec((1, n), lambda i, k: (0, 0))
    return pl.pallas_call(
        functools.partial(_out_proj_body, nk=nk, alpha=alpha, tm=tm),
        grid=(m // tm, nk),
        in_specs=[pl.BlockSpec((tm, tk), lambda i, k: (i, k)), pl.BlockSpec((tk, n), lambda i, k: (k, 0)),
                  row, vec, vec],
        out_specs=[row, row],
        out_shape=[jax.ShapeDtypeStruct((m, n), F32), jax.ShapeDtypeStruct((m, n), BF16)],
        scratch_shapes=[pltpu.VMEM((tm, n), F32)] if nk > 1 else [],
        compiler_params=_cparams(("parallel", "arbitrary")),
        name=name,
    )(z, w, h, g, b)


def _ffn_act_body(xb_ref, wg_ref, wu_ref, out_ref, *, tm):
    for r in range(tm // FFN_ACT_SUB):
        rows = slice(r * FFN_ACT_SUB, (r + 1) * FFN_ACT_SUB)
        xb = xb_ref[rows, :]
        out_ref[rows, :] = (_silu(_dot(xb, wg_ref[...])) * _dot(xb, wu_ref[...])).astype(out_ref.dtype)


def ffn_act(xb, w_in, dff):
    m, d = xb.shape
    tm, tf = min(FFN_ACT_TM, m), FFN_ACT_TF
    nf = dff // tf
    assert dff % tf == 0 and m % tm == 0 and tm % FFN_ACT_SUB == 0
    return pl.pallas_call(
        functools.partial(_ffn_act_body, tm=tm),
        grid=(m // tm, nf),
        in_specs=[pl.BlockSpec((tm, d), lambda i, f: (i, 0)),
                  pl.BlockSpec((d, tf), lambda i, f: (0, f)),
                  pl.BlockSpec((d, tf), lambda i, f: (0, nf + f))],
        out_specs=pl.BlockSpec((tm, tf), lambda i, f: (i, f)),
        out_shape=jax.ShapeDtypeStruct((m, dff), BF16),
        compiler_params=_cparams(("parallel", "parallel")),
        name="ffn_act",
    )(xb, w_in, w_in)


def ffn_split_ln(x, xb, w_in, w_out, g, b, alpha):
    act = ffn_act(xb, w_in, w_out.shape[0])
    return out_proj_ln(act, w_out, x, g, b, alpha, tk=FFN_DOWN_TK, name="ffn_down_ln")


def _rwkv_proj_body(x_ref, halo_ref, mu_ref, w_ref, out_ref, *, tm, seq):
    i = pl.program_id(0)
    x = x_ref[...]
    halo = jnp.where((i * tm) % seq == 0, 0.0, halo_ref[...])
    xx = _shifted(x, halo, 1) - x
    lhs = (x + xx * mu_ref[...]).astype(BF16)
    out_ref[...] = _dot(lhs, w_ref[...]).astype(out_ref.dtype)


def rwkv_proj(x, mu3, w3, seq, *, tm=RWKV_PROJ_TM):
    m, d = x.shape
    n = w3.shape[2]
    tm = min(tm, seq)
    per = tm // SUBLANES
    return pl.pallas_call(
        functools.partial(_rwkv_proj_body, tm=tm, seq=seq),
        grid=(m // tm, 3),
        in_specs=[
            pl.BlockSpec((tm, d), lambda i, j: (i, 0)),
            pl.BlockSpec((SUBLANES, d), lambda i, j: (jnp.maximum(i * per - 1, 0), 0)),
            pl.BlockSpec((None, 1, d), lambda i, j: (j, 0, 0)),
            pl.BlockSpec((None, d, n), lambda i, j: (j, 0, 0)),
        ],
        out_specs=pl.BlockSpec((None, tm, n), lambda i, j: (j, i, 0)),
        out_shape=jax.ShapeDtypeStruct((3, m, n), BF16),
        compiler_params=_cparams(("parallel", "arbitrary")),
        name="rwkv_proj",
    )(x, x, mu3, w3)


def _rwkv_lora_body(x_ref, halo_ref, mu_ref, w1_ref, w2_ref, a1_ref, a2_ref, g1_ref, g2_ref, w0_ref, a0_ref,
                    wl_ref, a_ref, g_ref, *, tm, seq):
    i = pl.program_id(0)
    x = x_ref[...]
    halo = jnp.where((i * tm) % seq == 0, 0.0, halo_ref[...])
    xx = _shifted(x, halo, 1) - x
    mu = mu_ref[...]
    xw = (x + xx * mu[0:1, :]).astype(BF16)
    xa = (x + xx * mu[1:2, :]).astype(BF16)
    xg = (x + xx * mu[2:3, :]).astype(BF16)
    zw = w0_ref[...] + _dot(jnp.tanh(_dot(xw, w1_ref[...])).astype(BF16), w2_ref[...])
    wl_ref[...] = _neg_softplus_neg(zw) - 0.5
    a_ref[...] = jax.nn.sigmoid(a0_ref[...] + _dot(_dot(xa, a1_ref[...]).astype(BF16), a2_ref[...]))
    g_ref[...] = _dot(jax.nn.sigmoid(_dot(xg, g1_ref[...])).astype(BF16), g2_ref[...]).astype(g_ref.dtype)


def rwkv_lora(x, mu3, w1, w2, a1, a2, g1, g2, w0, a0, seq, *, tm=RWKV_LORA_TM):
    m, d = x.shape
    tm = min(tm, seq)
    per = tm // SUBLANES
    full = lambda arr: pl.BlockSpec(arr.shape, lambda i: (0,) * arr.ndim)
    row = pl.BlockSpec((tm, d), lambda i: (i, 0))
    return pl.pallas_call(
        functools.partial(_rwkv_lora_body, tm=tm, seq=seq),
        grid=(m // tm,),
        in_specs=[row, pl.BlockSpec((SUBLANES, d), lambda i: (jnp.maximum(i * per - 1, 0), 0)),
                  full(mu3), full(w1), full(w2), full(a1), full(a2), full(g1), full(g2), full(w0), full(a0)],
        out_specs=[row, row, row],
        out_shape=[jax.ShapeDtypeStruct((m, d), F32), jax.ShapeDtypeStruct((m, d), F32),
                   jax.ShapeDtypeStruct((m, d), BF16)],
        compiler_params=_cparams(("parallel",)),
        name="rwkv_lora",
    )(x, x, mu3, w1, w2, a1, a2, g1, g2, w0, a0)


RW_LEVELS = 6


def _rwkv_masks():
    n = 2 * CHUNK
    t = np.arange(n)[:, None]
    s = np.arange(n)[None, :]
    strict = (t > s).astype(np.float32)
    incl = (t >= s).astype(np.float32)
    tri = np.stack([strict, incl])
    lvl = []
    for lv in range(RW_LEVELS):
        sz = 1 << lv
        lvl.append(((t // (2 * sz) == s // (2 * sz)) & ((t // sz) % 2 == 1) & ((s // sz) % 2 == 0)).astype(np.float32))
    t64 = incl[:CHUNK, :CHUNK]
    return jnp.asarray(tri), jnp.asarray(np.stack(lvl)), jnp.asarray(t64)


def _rwkv_core_body(r_ref, k_ref, v_ref, wl_ref, a_ref, g_ref, kk_ref, ka_ref, rk_ref, lng_ref, lnb_ref,
                    tri_ref, lvl_ref, t64_ref, out_ref, st_ref):
    c = pl.program_id(2)

    @pl.when(c == 0)
    def _():
        st_ref[...] = jnp.zeros_like(st_ref)

    strict, incl = tri_ref[0], tri_ref[1]
    eye = incl - strict
    t64 = t64_ref[...].astype(BF16)
    lo = lax.broadcasted_iota(jnp.int32, (1, LANES), 1) < RW_HEAD
    inv_head = 1.0 / RW_HEAD

    def stack(x):
        return jnp.concatenate([jnp.where(lo, x, 0.0), jnp.where(lo, 0.0, x)], axis=0).astype(BF16)

    pairs = range(RW_PAIRS_PER_STEP)
    sls = [slice(q * LANES, (q + 1) * LANES) for q in pairs]

    ld = lambda ref, sl: ref[:, sl].astype(F32)
    kk_sq = []
    for sl in sls:
        kk = ld(k_ref, sl) * kk_ref[:, sl]
        kk_sq.append(_half_sums(lo, kk * kk))
    cums = [_dot_01(t64, -jnp.exp(wl_ref[:, sl])) for sl in sls]

    ops = []
    for sl, ss, cum in zip(sls, kk_sq, cums):
        k, a = ld(k_ref, sl), a_ref[:, sl]
        kk = k * kk_ref[:, sl] / jnp.maximum(jnp.sqrt(ss), 1e-12)
        k2 = k * (1.0 + (a - 1.0) * ka_ref[:, sl])
        lw = -jnp.exp(wl_ref[:, sl])
        cl = cum[CHUNK - 1:CHUNK, :]
        e_neg = jnp.exp(-cum)
        e_end = jnp.exp(cl - cum)
        kb = kk * a
        ops.append(dict(
            a_s=stack(-kk * jnp.exp(cum - lw)),
            r_s=stack(ld(r_ref, sl) * jnp.exp(cum)),
            b_s=stack(kb * e_neg), k_s=stack(k2 * e_neg), v_s=stack(ld(v_ref, sl)),
            bh_s=stack(kb * e_end), kh_s=stack(k2 * e_end),
            w_end=jnp.exp(cl), k2=k2))

    n2 = 2 * CHUNK
    quads = [(2 * i, 2 * i + 1) for i in range(RW_PAIRS_PER_STEP // 2)]
    zero = jnp.zeros((n2, n2), BF16)
    rows2 = lambda top, bot: jnp.concatenate([top, bot], axis=0)
    side = lambda xa, xb: jnp.concatenate([xa, xb], axis=1)
    bdiag = lambda xa, xb: rows2(side(xa, zero), side(zero, xb))
    half = lambda x, j: x[:, j * n2:(j + 1) * n2]
    col = lambda key, q: side(ops[q[0]][key], ops[q[1]][key])

    gmask = rows2(side(strict, strict), side(incl, incl))
    gram = [gmask * _dot_nt(rows2(o["a_s"], o["r_s"]), rows2(o["b_s"], o["k_s"])) for o in ops]
    a_ab = [g[:n2, :n2] for g in gram]
    a_ak = [g[:n2, n2:].astype(BF16) for g in gram]
    r_b = [g[n2:, :n2].astype(BF16) for g in gram]
    r_k = [g[n2:, n2:].astype(BF16) for g in gram]

    t_inv = [eye + m * lvl_ref[0] for m in a_ab]
    for lv in range(1, RW_LEVELS):
        tb = [t.astype(BF16) for t in t_inv]
        off = [(m * lvl_ref[lv]).astype(BF16) for m in a_ab]
        xs = [_dot(side(tb[a], tb[b]), bdiag(off[a], off[b])).astype(BF16) for a, b in quads]
        upd = [_dot(x, bdiag(tb[a], tb[b])) for x, (a, b) in zip(xs, quads)]
        t_inv = [t + half(upd[p // 2], p % 2) for p, t in enumerate(t_inv)]
    tb = [t.astype(BF16) for t in t_inv]

    s_old = [st_ref[q] for q in pairs]
    s_b = [s.astype(BF16) for s in s_old]
    p12 = [_dot_nt(rows2(col("a_s", q), col("r_s", q)), bdiag(s_b[q[0]], s_b[q[1]]))
           + _dot(rows2(side(a_ak[q[0]], a_ak[q[1]]), side(r_k[q[0]], r_k[q[1]])), bdiag(ops[q[0]]["v_s"], ops[q[1]]["v_s"]))
           for q in quads]
    rhs = [p[:n2].astype(BF16) for p in p12]
    u_b = [_dot(side(tb[a], tb[b]), bdiag(half(x, 0), half(x, 1))).astype(BF16) for x, (a, b) in zip(rhs, quads)]
    y_q = [p[n2:] + _dot(side(r_b[a], r_b[b]), bdiag(half(u, 0), half(u, 1))) for p, u, (a, b) in zip(p12, u_b, quads)]
    inc = [_dot_tn(rows2(u, col("v_s", q)), rows2(col("bh_s", q), col("kh_s", q))) for u, q in zip(u_b, quads)]
    for z, (a, b) in zip(inc, quads):
        st_ref[a] = s_old[a] * ops[a]["w_end"] + z[:n2, :n2]
        st_ref[b] = s_old[b] * ops[b]["w_end"] + z[n2:, n2:]

    y_s = [half(y_q[p // 2], p % 2) for p in pairs]
    ys = [y[:CHUNK] + y[CHUNK:] for y in y_s]
    mus = [_half_sums(lo, y) * inv_head for y in ys]
    ds = [y - mu for y, mu in zip(ys, mus)]
    var = [_half_sums(lo, d * d) * inv_head for d in ds]
    bonus = [_half_sums(lo, ld(r_ref, sl) * o["k2"] * rk_ref[:, sl]) for sl, o in zip(sls, ops)]
    for sl, d, vr, bn in zip(sls, ds, var, bonus):
        yn = d * lax.rsqrt(vr + RW_GN_EPS) * lng_ref[:, sl] + lnb_ref[:, sl]
        out_ref[:, sl] = ((yn + bn * ld(v_ref, sl)) * ld(g_ref, sl)).astype(out_ref.dtype)


def rwkv_core(rkv, wl, a, g, k_k, k_a, r_k, ln_g, ln_b, bsz, seq):
    _, m, d = rkv.shape
    width = RW_PAIRS_PER_STEP * LANES
    nc = seq // CHUNK
    tri, lvl, t64 = _rwkv_masks()
    row = lambda b, p, c: (b * nc + c, p)
    rkv_spec = lambda j: pl.BlockSpec((None, CHUNK, width), lambda b, p, c: (j, b * nc + c, p))
    act = pl.BlockSpec((CHUNK, width), row)
    par = pl.BlockSpec((1, width), lambda b, p, c: (0, p))
    full = lambda arr: pl.BlockSpec(arr.shape, lambda b, p, c: (0,) * arr.ndim)
    return pl.pallas_call(
        _rwkv_core_body,
        grid=(bsz, d // width, nc),
        in_specs=[rkv_spec(0), rkv_spec(1), rkv_spec(2), act, act, act, par, par, par, par, par,
                  full(tri), full(lvl), full(t64)],
        out_specs=act,
        out_shape=jax.ShapeDtypeStruct((m, d), BF16),
        scratch_shapes=[pltpu.VMEM((RW_PAIRS_PER_STEP, LANES, LANES), F32)],
        compiler_params=_cparams(("parallel", "parallel", "arbitrary")),
        name="rwkv_core",
    )(rkv, rkv, rkv, wl, a, g, k_k, k_a, r_k, ln_g, ln_b, tri, lvl, t64)


def _pad_to(x, axis, size):
    pad = [(0, 0)] * x.ndim
    pad[axis] = (0, size - x.shape[axis])
    return jnp.pad(x, pad)


def rwkv_layer(h, bsz, seq, mu, w_in, w0, w1, w2, a0, a1, a2, g1, g2, k_k, k_a, r_k, gn_g, gn_b, w_out,
               ln_g, ln_b, alpha):
    d = h.shape[1]
    v2 = lambda t: t.reshape(1, d)
    mu_rkv = jnp.stack([mu[0], mu[2], mu[3]]).reshape(3, 1, d)
    mu_lora = jnp.stack([mu[1], mu[4], mu[5]])
    rkv = rwkv_proj(h, mu_rkv, w_in.astype(BF16), seq)
    wl, a, g = rwkv_lora(
        h, mu_lora,
        _pad_to(w1, 1, LANES).astype(BF16), _pad_to(w2, 0, LANES).astype(BF16),
        _pad_to(a1, 1, LANES).astype(BF16), _pad_to(a2, 0, LANES).astype(BF16),
        g1.astype(BF16), g2.astype(BF16), v2(w0), v2(a0), seq)
    z = rwkv_core(rkv, wl, a, g, v2(k_k), v2(k_a), v2(r_k), v2(gn_g), v2(gn_b), bsz, seq)
    return out_proj_ln(z, w_out.astype(BF16), h, ln_g, ln_b, alpha)


def _ssd_consts(heads):
    lanes = heads * SSD_HEAD
    head_of = np.arange(lanes) // SSD_HEAD
    pos_of = np.arange(lanes) % SSD_HEAD
    expand = (np.arange(LANES)[:, None] == head_of[None, :]).astype(np.float32)
    eye = (np.arange(CHUNK)[:, None] == pos_of[None, :]).astype(np.float32)
    tril = (np.arange(CHUNK)[:, None] >= pos_of[None, :]).astype(np.float32)
    return jnp.asarray(expand, BF16), jnp.asarray(eye), jnp.asarray(tril)


def _expand_heads(x, e01):
    hi = x.astype(BF16)
    low = (x - hi.astype(F32)).astype(BF16)
    return _dot(hi, e01) + _dot(low, e01)


def _ssd_body(xbc_ref, halo_ref, z_ref, dtp_ref, cw_ref, cb_ref, dtb_ref, alog_ref, dsk_ref, ng_ref, t64_ref,
              e01_ref, eye_ref, tril_ref, out_ref, st_ref, xc_ref, dt_ref, da_ref, *, tq, d_inner):
    s = pl.program_id(1)

    @pl.when(s == 0)
    def _():
        st_ref[...] = jnp.zeros_like(st_ref)

    halo = jnp.where(s == 0, 0.0, halo_ref[...].astype(F32)[BF16_ROWS - SUBLANES:, :])
    xc_ref[...] = _silu(_causal_conv4(xbc_ref[...].astype(F32), halo, cw_ref[...], cb_ref[...]))
    dt_all = _softplus(dtp_ref[...] + dtb_ref[...])
    dt_ref[...] = dt_all
    da_ref[...] = dt_all * (-jnp.exp(alog_ref[...]))

    t64b = t64_ref[...].astype(BF16)
    lo = lax.broadcasted_iota(jnp.int32, (1, LANES), 1) < SSD_HEAD
    gn = SSD_GROUPS * SSD_STATE
    gw = d_inner // SSD_GROUPS
    pairs_per_group = gw // LANES

    def chunk(ci, carry):
        r0 = pl.multiple_of(ci * CHUNK, CHUNK)
        rows = pl.ds(r0, CHUNK)
        e01 = e01_ref[...]
        cs = _dot_01(t64b, da_ref[rows, :])
        dte = _expand_heads(dt_ref[rows, :], e01)
        cse = _expand_heads(cs, e01)
        cs_row = jnp.sum(eye_ref[...] * cse, axis=0, keepdims=True)
        decay = jnp.exp(jnp.minimum(cse - cs_row, 0.0)) * tril_ref[...]
        cs_last = cse[CHUNK - 1:CHUNK, :]
        ecs = jnp.exp(cse)
        to_end = jnp.exp(cs_last - cse)
        e_last = jnp.exp(cs_last)
        for g in range(SSD_GROUPS):
            gsl = slice(g * gw, (g + 1) * gw)
            bg = xc_ref[rows, d_inner + g * SSD_STATE:d_inner + (g + 1) * SSD_STATE]
            cg = xc_ref[rows, d_inner + gn + g * SSD_STATE:d_inner + gn + (g + 1) * SSD_STATE].astype(BF16)
            bgb = bg.astype(BF16)
            cb_rep = _dot_nt(cg, jnp.concatenate([bgb] * (gw // SSD_HEAD), axis=0))
            mat = (cb_rep * decay[:, gsl]).astype(BF16)
            bg_t = bg.T.astype(BF16)
            ps = [g * pairs_per_group + q for q in range(pairs_per_group)]
            sls = [slice(p * LANES, (p + 1) * LANES) for p in ps]
            xs = [xc_ref[rows, sl] for sl in sls]
            xdts = [x * dte[:, sl] for x, sl in zip(xs, sls)]
            x_st = [jnp.concatenate([jnp.where(lo, xdt, 0.0), jnp.where(lo, 0.0, xdt)], axis=0).astype(BF16)
                    for xdt in xdts]
            x_end = [(xdt * to_end[:, sl]).astype(BF16) for xdt, sl in zip(xdts, sls)]
            sts = [st_ref[p] for p in ps]
            intra = [_dot(mat[:, q * LANES:(q + 1) * LANES], xs_) for q, xs_ in enumerate(x_st)]
            inter = [_dot(cg, st.astype(BF16)) for st in sts]
            upd = [_dot(bg_t, xe) for xe in x_end]
            for p, sl, st, u in zip(ps, sls, sts, upd):
                st_ref[p] = st * e_last[:, sl] + u
            ys = []
            ssq = jnp.zeros((CHUNK, 1), F32)
            for sl, x, ya, yb in zip(sls, xs, intra, inter):
                y = ya + yb * ecs[:, sl] + x * dsk_ref[:, sl]
                y = y * _silu(z_ref[rows, sl].astype(F32))
                ssq = ssq + jnp.sum(y * y, axis=-1, keepdims=True)
                ys.append(y)
            inv = lax.rsqrt(ssq * (1.0 / gw) + 1e-5)
            for sl, y in zip(sls, ys):
                out_ref[rows, sl] = (y * inv * ng_ref[:, sl]).astype(out_ref.dtype)
        return carry

    lax.fori_loop(0, tq // CHUNK, chunk, 0)


def ssd_core(xbc_pre, z, dt_pre, conv_w, conv_b, dt_bias, a_log, d_skip, norm_g, bsz, seq, *, tq=SSD_TQ):
    n, conv_dim = xbc_pre.shape
    d_inner = z.shape[1]
    tq = min(tq, seq)
    nblk = seq // tq
    per = tq // BF16_ROWS
    npairs = d_inner // LANES
    t64 = _rwkv_masks()[2]
    e01, eye, tril = _ssd_consts(d_inner // SSD_HEAD)
    row = lambda b, s: (b * nblk + s, 0)
    full = lambda arr: pl.BlockSpec(arr.shape, lambda b, s: (0,) * arr.ndim)
    return pl.pallas_call(
        functools.partial(_ssd_body, tq=tq, d_inner=d_inner),
        grid=(bsz, nblk),
        in_specs=[
            pl.BlockSpec((tq, conv_dim), row),
            pl.BlockSpec((BF16_ROWS, conv_dim), lambda b, s: (jnp.maximum((b * nblk + s) * per - 1, 0), 0)),
            pl.BlockSpec((tq, d_inner), row),
            pl.BlockSpec((tq, LANES), row),
            full(conv_w), full(conv_b), full(dt_bias), full(a_log), full(d_skip), full(norm_g), full(t64),
            full(e01), full(eye), full(tril),
        ],
        out_specs=pl.BlockSpec((tq, d_inner), row),
        out_shape=jax.ShapeDtypeStruct((n, d_inner), BF16),
        scratch_shapes=[pltpu.VMEM((npairs, SSD_STATE, LANES), F32), pltpu.VMEM((tq, conv_dim), F32),
                        pltpu.VMEM((tq, LANES), F32), pltpu.VMEM((tq, LANES), F32)],
        compiler_params=_cparams(("arbitrary", "arbitrary")),
        name="ssd_core",
    )(xbc_pre, xbc_pre, z, dt_pre, conv_w, conv_b, dt_bias, a_log, d_skip, norm_g, t64, e01, eye, tril)


def ssd_layer(h, hb, bsz, seq, w_in, conv_w, conv_b, dt_bias, a_log, d_skip, norm_g, w_out, ln_g, ln_b, alpha):
    d_inner = w_out.shape[0]
    conv_dim = conv_w.shape[1]
    w_in = w_in.astype(BF16)
    z = proj(hb, w_in[:, :d_inner], BF16, "ssd_proj_z")
    xbc_pre = proj(hb, w_in[:, d_inner:d_inner + conv_dim], BF16, "ssd_proj_xbc")
    dt_pre = proj(hb, _pad_to(w_in[:, d_inner + conv_dim:], 1, LANES), F32, "ssd_proj_dt")
    pad_row = lambda t: _pad_to(t.reshape(1, -1), 1, LANES)
    d_skip_lanes = jnp.repeat(d_skip, SSD_HEAD).reshape(1, -1)
    y = ssd_core(xbc_pre, z, dt_pre, conv_w, conv_b.reshape(1, -1), pad_row(dt_bias), pad_row(a_log),
                 d_skip_lanes, norm_g.reshape(1, -1), bsz, seq)
    return out_proj_ln(y, w_out.astype(BF16), h, ln_g, ln_b, alpha)


ROPE_HALF = 16


def _rope_table_body(pos_ref, freq_ref, cos_ref, sin_ref):
    ang = pos_ref[...].astype(F32) * freq_ref[...]
    lane = lax.broadcasted_iota(jnp.int32, (1, LANES), 1)
    sn = jnp.sin(ang)
    cos_ref[...] = jnp.cos(ang)
    sin_ref[...] = jnp.where(lane < ROPE_HALF, -sn, jnp.where(lane < 2 * ROPE_HALF, sn, 0.0))


def rope_tables(pos, freq, *, tm=ROPE_TM):
    n = pos.shape[0]
    tm = min(tm, n)
    band = pl.BlockSpec((tm, LANES), lambda i: (i, 0))
    return pl.pallas_call(
        _rope_table_body,
        grid=(n // tm,),
        in_specs=[pl.BlockSpec((tm, 1), lambda i: (i, 0)), pl.BlockSpec((1, LANES), lambda i: (0, 0))],
        out_specs=[band, band],
        out_shape=[jax.ShapeDtypeStruct((n, LANES), F32)] * 2,
        compiler_params=_cparams(("parallel",)),
        name="rope_tables",
    )(pos, freq)


def _post_rope(scale, acc, cos, sin):
    lane = lax.broadcasted_iota(jnp.int32, (1, LANES), 1)
    outs = []
    for j in range(acc.shape[1] // LANES):
        t = acc[:, j * LANES:(j + 1) * LANES]
        partner = jnp.where(lane < ROPE_HALF, pltpu.roll(t, LANES - ROPE_HALF, axis=1), pltpu.roll(t, ROPE_HALF, axis=1))
        outs.append((t * cos + partner * sin) * scale)
    return jnp.concatenate(outs, axis=1)


def _attn_body(qi_ref, ki_ref, last_ref, q_ref, k_ref, v_ref, lq1_ref, lk1_ref, lq2_ref, lk2_ref, sg_ref,
               out_ref, m_ref, l_ref, acc_ref, *, tq, tk, hd):
    p = pl.program_id(2)
    qi, ki = qi_ref[p], ki_ref[p]

    @pl.when(ki == 0)
    def _():
        m_ref[...] = jnp.full_like(m_ref, NEG_BIG)
        l_ref[...] = jnp.zeros_like(l_ref)
        acc_ref[...] = jnp.zeros_like(acc_ref)

    sub = min(ATTN_SUB_ROWS, tq)
    n_sub = tq // sub

    def update(masked):
        def n_keys(r):
            return (r + 1) * sub if masked and tq == tk else tk

        def qk(r):
            rows = slice(r * sub, (r + 1) * sub)
            return [_dot_nt(q_ref[rows, j * hd:(j + 1) * hd], k_ref[:n_keys(r), j * hd:(j + 1) * hd])
                    for j in range(2)]

        scores = qk(0)
        for r in range(n_sub):
            nxt = qk(r + 1) if r + 1 < n_sub else None
            rows = slice(r * sub, (r + 1) * sub)
            if masked:
                row0 = qi * tq + r * sub
                q_chunk = jnp.right_shift(row0 + lax.broadcasted_iota(jnp.int32, (sub, 1), 0), CHUNK_SHIFT)
                k_chunk = jnp.right_shift(ki * tk + lax.broadcasted_iota(jnp.int32, (1, n_keys(r)), 1), CHUNK_SHIFT)
                visible = k_chunk <= q_chunk
                scores = [jnp.where(visible, s, NEG_BIG) for s in scores]
            v = v_ref[:n_keys(r), :]
            for j in range(2):
                m_old = m_ref[j, rows, :]
                m_new = jnp.maximum(m_old, jnp.max(scores[j], axis=-1, keepdims=True))
                pe = jnp.exp2(scores[j] - m_new)
                scale = jnp.exp2(m_old - m_new)
                l_ref[j, rows, :] = scale * l_ref[j, rows, :] + jnp.sum(pe, axis=-1, keepdims=True)
                acc_ref[j, rows, :] = scale * acc_ref[j, rows, :] + _dot(pe.astype(BF16), v)
                m_ref[j, rows, :] = m_new
            scores = nxt

    needs_mask = (ki + 1) * tk > qi * tq + CHUNK

    @pl.when(needs_mask)
    def _():
        update(True)

    @pl.when(jnp.logical_not(needs_mask))
    def _():
        update(False)

    @pl.when(last_ref[p] == 1)
    def _():
        lam = (jnp.exp(jnp.sum(lq1_ref[...] * lk1_ref[...], axis=-1, keepdims=True))
               - jnp.exp(jnp.sum(lq2_ref[...] * lk2_ref[...], axis=-1, keepdims=True)) + DIF_LAMBDA_INIT)
        o = acc_ref[0] / l_ref[0] - lam * (acc_ref[1] / l_ref[1])
        o = o * lax.rsqrt(jnp.mean(o * o, axis=-1, keepdims=True) + 1e-5) * sg_ref[...]
        out_ref[...] = (o * (1.0 - DIF_LAMBDA_INIT)).astype(out_ref.dtype)


def diff_attention(q, k, v, lq1, lk1, lq2, lk2, subln_g, bsz, seq, *, tq=ATTN_TQ, tk=ATTN_TK):
    n, d = q.shape
    hw = d // DIF_HEADS
    hd = hw // 2
    tq, tk = min(tq, seq), min(tk, seq)
    assert tq % CHUNK == 0 and tk % CHUNK == 0 and CHUNK == 1 << CHUNK_SHIFT
    nq, nk = seq // tq, seq // tk
    n_kv = lambda a: -(-((a + 1) * tq) // tk)
    pairs = [(a, b) for a in range(nq) for b in range(n_kv(a))]
    qi = jnp.asarray([a for a, _ in pairs], jnp.int32)
    ki = jnp.asarray([b for _, b in pairs], jnp.int32)
    last = jnp.asarray([1 if b == n_kv(a) - 1 else 0 for a, b in pairs], jnp.int32)
    vec = pl.BlockSpec((1, hd), lambda b, h, p, qi, ki, last: (0, 0))
    grid_spec = pltpu.PrefetchScalarGridSpec(
        num_scalar_prefetch=3,
        grid=(bsz, DIF_HEADS, len(pairs)),
        in_specs=[
            pl.BlockSpec((tq, hw), lambda b, h, p, qi, ki, last: (b * nq + qi[p], h)),
            pl.BlockSpec((tk, hw), lambda b, h, p, qi, ki, last: (b * nk + ki[p], h)),
            pl.BlockSpec((tk, hw), lambda b, h, p, qi, ki, last: (b * nk + ki[p], h)),
            vec, vec, vec, vec,
            pl.BlockSpec((1, hw), lambda b, h, p, qi, ki, last: (0, 0)),
        ],
        out_specs=pl.BlockSpec((tq, hw), lambda b, h, p, qi, ki, last: (b * nq + qi[p], h)),
        scratch_shapes=[pltpu.VMEM((2, tq, 1), F32), pltpu.VMEM((2, tq, 1), F32), pltpu.VMEM((2, tq, hw), F32)],
    )
    return pl.pallas_call(
        functools.partial(_attn_body, tq=tq, tk=tk, hd=hd),
        grid_spec=grid_spec,
        out_shape=jax.ShapeDtypeStruct((n, d), BF16),
        compiler_params=_cparams(("parallel", "parallel", "arbitrary")),
        name="diff_attention",
    )(qi, ki, last, q, k, v, lq1, lk1, lq2, lk2, subln_g)


def diff_layer(h, hb, positions, bsz, seq, w_in, lq1, lk1, lq2, lk2, subln_g, w_out, ln_g, ln_b, alpha):
    d = h.shape[1]
    hd = d // DIF_HEADS // 2
    rope = hd // 4
    assert hd == LANES and rope == 2 * ROPE_HALF
    inv_freq = ROPE_THETA ** (-jnp.arange(0, rope, 2, dtype=F32) / rope)
    freq = jnp.concatenate([inv_freq, inv_freq, jnp.zeros((LANES - rope,), F32)]).reshape(1, LANES)
    cos, sin = rope_tables(positions.reshape(bsz * seq, 1), freq)
    w_in = w_in.astype(BF16)
    rope_mm = lambda w, scale, name: proj(hb, w, BF16, name, post=functools.partial(_post_rope, scale),
                                          bands=[cos, sin])
    q = rope_mm(w_in[:, :d], hd ** -0.5 * math.log2(math.e), "dif_proj_q")
    k = rope_mm(w_in[:, d:2 * d], 1.0, "dif_proj_k")
    v = proj(hb, w_in[:, 2 * d:], BF16, "dif_proj_v")
    v2 = lambda t: t.reshape(1, -1)
    o = diff_attention(q, k, v, v2(lq1), v2(lk1), v2(lq2), v2(lk2), v2(subln_g), bsz, seq)
    return out_proj_ln(o, w_out.astype(BF16), h, ln_g, ln_b, alpha)


def _post_gelu(acc):
    c = math.sqrt(2.0 / math.pi)
    return 0.5 * acc * (1.0 + jnp.tanh(c * (acc + 0.044715 * acc * acc * acc)))


def _lru_body(u_ref, halo_ref, gate_ref, cw_ref, cb_ref, wa_ref, ba_ref, wx_ref, bx_ref, lam_ref,
              out_ref, h_ref, a_ref, b_ref, *, tq, width):
    s = pl.program_id(1)

    @pl.when(s == 0)
    def _():
        h_ref[...] = jnp.zeros_like(h_ref)

    halo = jnp.where(s == 0, 0.0, halo_ref[...])
    u = _causal_conv4(u_ref[...], halo, cw_ref[...], cb_ref[...])

    nsp = _softplus(-lam_ref[...])
    blk = width // LRU_BLOCKS
    for j in range(LRU_BLOCKS):
        sl = slice(j * blk, (j + 1) * blk)
        uj = u[:, sl]
        ub = uj.astype(BF16)
        r = jax.nn.sigmoid(_dot(ub, wa_ref[j]) + ba_ref[:, sl])
        i = jax.nn.sigmoid(_dot(ub, wx_ref[j]) + bx_ref[:, sl])
        log_a = -LRU_C * r * nsp[:, sl]
        a_ref[:, sl] = jnp.exp(log_a)
        b_ref[:, sl] = jnp.sqrt(1.0 - jnp.exp(2.0 * log_a)) * (i * uj)

    def step(t, h):
        h = a_ref[pl.ds(t, 1), :] * h + b_ref[pl.ds(t, 1), :]
        b_ref[pl.ds(t, 1), :] = h
        return h

    h_ref[0:1, :] = lax.fori_loop(0, tq, step, h_ref[0:1, :], unroll=8)
    out_ref[...] = (gate_ref[...].astype(F32) * b_ref[...]).astype(out_ref.dtype)


def lru_core(u_pre, gate, conv_w, conv_b, w_a, b_a, w_x, b_x, lam, bsz, seq, *, tq=LRU_TQ):
    n, width = u_pre.shape
    tq = min(tq, seq)
    nblk = seq // tq
    per = tq // SUBLANES
    row = lambda b, s: (b * nblk + s, 0)
    full = lambda arr: pl.BlockSpec(arr.shape, lambda b, s: (0,) * arr.ndim)
    return pl.pallas_call(
        functools.partial(_lru_body, tq=tq, width=width),
        grid=(bsz, nblk),
        in_specs=[
            pl.BlockSpec((tq, width), row),
            pl.BlockSpec((SUBLANES, width), lambda b, s: (jnp.maximum((b * nblk + s) * per - 1, 0), 0)),
            pl.BlockSpec((tq, width), row),
            full(conv_w), full(conv_b), full(w_a), full(b_a), full(w_x), full(b_x), full(lam),
        ],
        out_specs=pl.BlockSpec((tq, width), row),
        out_shape=jax.ShapeDtypeStruct((n, width), BF16),
        scratch_shapes=[pltpu.VMEM((SUBLANES, width), F32), pltpu.VMEM((tq, width), F32),
                        pltpu.VMEM((tq, width), F32)],
        compiler_params=_cparams(("arbitrary", "arbitrary")),
        name="lru_core",
    )(u_pre, u_pre, gate, conv_w, conv_b, w_a, b_a, w_x, b_x, lam)


def rglru_layer(h, hb, bsz, seq, w_in, conv_w, conv_b, w_a, b_a, w_x, b_x, lam, w_out, ln_g, ln_b, alpha):
    width = w_in.shape[1] // 2
    w_in = w_in.astype(BF16)
    gate = proj(hb, w_in[:, :width], BF16, "lru_proj_gate", post=_post_gelu)
    u_pre = proj(hb, w_in[:, width:], F32, "lru_proj_u")
    z = lru_core(u_pre, gate, conv_w, conv_b.reshape(1, -1), w_a.astype(BF16), b_a.reshape(1, -1),
                 w_x.astype(BF16), b_x.reshape(1, -1), lam.reshape(1, -1), bsz, seq)
    return out_proj_ln(z, w_out.astype(BF16), h, ln_g, ln_b, alpha)


def kernel(x, positions, rwkv_mu, rwkv_w_in, rwkv_w0, rwkv_w1, rwkv_w2, rwkv_a0, rwkv_a1, rwkv_a2, rwkv_g1, rwkv_g2, rwkv_k_k, rwkv_k_a, rwkv_r_k, rwkv_ln_g, rwkv_ln_b, rwkv_w_out, ssd_w_in, ssd_conv_w, ssd_conv_b, ssd_dt_bias, ssd_a_log, ssd_d, ssd_norm_g, ssd_w_out, dif_w_in, dif_lq1, dif_lk1, dif_lq2, dif_lk2, dif_subln_g, dif_w_out, lru_w_in, lru_conv_w, lru_conv_b, lru_w_a, lru_b_a, lru_w_x, lru_b_x, lru_lam, lru_w_out, ffn_w_in, ffn_w_out, ln_g, ln_b):
    bsz, seq, d = x.shape
    depth = ffn_w_in.shape[0]
    alpha = (2 * depth) ** 0.25
    h = x.reshape(bsz * seq, d)
    hb = None
    for i in range(depth):
        m = i % 4
        g0, b0 = ln_g[i, 0].reshape(1, d), ln_b[i, 0].reshape(1, d)
        g1, b1 = ln_g[i, 1].reshape(1, d), ln_b[i, 1].reshape(1, d)
        if m == 0:
            h, hb = rwkv_layer(h, bsz, seq, rwkv_mu, rwkv_w_in, rwkv_w0, rwkv_w1, rwkv_w2, rwkv_a0, rwkv_a1,
                               rwkv_a2, rwkv_g1, rwkv_g2, rwkv_k_k, rwkv_k_a, rwkv_r_k, rwkv_ln_g, rwkv_ln_b,
                               rwkv_w_out, g0, b0, alpha)
        elif m == 1:
            h, hb = ssd_layer(h, hb, bsz, seq, ssd_w_in, ssd_conv_w, ssd_conv_b, ssd_dt_bias, ssd_a_log, ssd_d,
                              ssd_norm_g, ssd_w_out, g0, b0, alpha)
        elif m == 2:
            h, hb = diff_layer(h, hb, positions, bsz, seq, dif_w_in, dif_lq1, dif_lk1, dif_lq2, dif_lk2,
                               dif_subln_g, dif_w_out, g0, b0, alpha)
        else:
            h, hb = rglru_layer(h, hb, bsz, seq, lru_w_in, lru_conv_w, lru_conv_b, lru_w_a, lru_b_a, lru_w_x,
                                lru_b_x, lru_lam, lru_w_out, g0, b0, alpha)
        h, hb = ffn_split_ln(h, hb, ffn_w_in[i].astype(BF16), ffn_w_out[i].astype(BF16), g1, b1, alpha)
    return h.reshape(bsz, seq, d)
```
